```python
import math
import jax, jax.numpy as jnp
from jax import lax
import numpy as np

D_MODEL = 1024
BATCH = 1
SEQ = 16384
DEPTH = 2
DEC_BATCH = 8
DEC_SEQ = 64
PAST_LEN = 1024

CHUNK = 64
N_MIXERS = 2
N_ATTN_LAYERS = (DEPTH + 1) // 2
N_DELTA_LAYERS = DEPTH // 2

N_HEADS = 16
N_KV_HEADS = 2
HEAD_DIM = 64
GROUP = N_HEADS // N_KV_HEADS
WINDOW = 128
WIN_CHUNKS = WINDOW // CHUNK
N_BAND = WIN_CHUNKS + 1
CACHE_ROWS = WINDOW
QKV_DIM = (N_HEADS + 2 * N_KV_HEADS) * HEAD_DIM

DN_QK_HEADS = 8
DN_V_HEADS = 16
DN_DK = 128
DN_DV = 128
DN_KEY_DIM = DN_QK_HEADS * DN_DK
DN_VAL_DIM = DN_V_HEADS * DN_DV
CONV_WIDTH = 4
CONV_DIM = 2 * DN_KEY_DIM + DN_VAL_DIM
DN_IN_DIM = CONV_DIM + DN_VAL_DIM + 2 * DN_V_HEADS
DN_BLOCK = 64

D_FF = ((8 * D_MODEL + 3 * 256 - 1) // (3 * 256)) * 256
PLE_DIM = 256

ALPHA = (2 * DEPTH) ** 0.25
BETA = (8 * DEPTH) ** -0.25

LN_EPS = 1e-5
RMS_EPS = 1e-6
NEG_INF = -1e30

kernel_name = "hybrid_swa_sink_gated_deltanet_stream_step"


def layer_norm(x, g, b):
    xf = x.astype(jnp.float32)
    mu = jnp.mean(xf, -1, keepdims=True)
    var = jnp.mean(jnp.square(xf - mu), -1, keepdims=True)
    return ((xf - mu) * lax.rsqrt(var + LN_EPS) * g.astype(jnp.float32) + b.astype(jnp.float32)).astype(x.dtype)


def channel_block(x, p, w_gu, w_down, w_gate, w_proj, g, b):
    gate, up = jnp.split(x @ w_gu, 2, axis=-1)
    f = (jax.nn.silu(gate) * up) @ w_down
    e = jax.nn.sigmoid(x @ w_gate) * (p @ w_proj)
    return layer_norm(ALPHA * x + f + e, g, b)


def attn_qkv(x, w_qkv, b_qkv):
    B, L, _ = x.shape
    qkv = x @ w_qkv + b_qkv
    q, k, v = jnp.split(qkv, [N_HEADS * HEAD_DIM, (N_HEADS + N_KV_HEADS) * HEAD_DIM], axis=-1)
    return (q.reshape(B, L, N_KV_HEADS, GROUP, HEAD_DIM),
            k.reshape(B, L, N_KV_HEADS, HEAD_DIM),
            v.reshape(B, L, N_KV_HEADS, HEAD_DIM))


def sink_attention(qb, kb, vb, valid, sinks):
    s = jnp.einsum('bnqkgd,bnskd->bnkgqs', qb, kb).astype(jnp.float32) * (HEAD_DIM ** -0.5)
    s = jnp.where(valid[None, :, None, None, None, :], s, NEG_INF)
    sink = jnp.broadcast_to(sinks.astype(jnp.float32).reshape(1, 1, N_KV_HEADS, GROUP, 1, 1), s.shape[:-1] + (1,))
    p = jax.nn.softmax(jnp.concatenate([s, sink], axis=-1), axis=-1)[..., :-1]
    return jnp.einsum('bnkgqs,bnskd->bnqkgd', p.astype(vb.dtype), vb)


def swa_prompt(x, w_qkv, b_qkv, sinks, w_o):
    B, L, _ = x.shape
    nb = L // CHUNK
    q, k, v = attn_qkv(x, w_qkv, b_qkv)
    pad = ((0, 0), (WINDOW, 0), (0, 0), (0, 0))
    kc = jnp.pad(k, pad).reshape(B, nb + WIN_CHUNKS, CHUNK, N_KV_HEADS, HEAD_DIM)
    vc = jnp.pad(v, pad).reshape(B, nb + WIN_CHUNKS, CHUNK, N_KV_HEADS, HEAD_DIM)
    kb = jnp.concatenate([kc[:, j:j + nb] for j in range(N_BAND)], axis=2)
    vb = jnp.concatenate([vc[:, j:j + nb] for j in range(N_BAND)], axis=2)
    key_chunk = jnp.arange(nb)[:, None] + jnp.arange(N_BAND)[None, :] - WIN_CHUNKS
    valid = jnp.repeat(key_chunk >= 0, CHUNK, axis=1)
    o = sink_attention(q.reshape(B, nb, CHUNK, N_KV_HEADS, GROUP, HEAD_DIM), kb, vb, valid, sinks)
    y = o.reshape(B, L, N_HEADS * HEAD_DIM) @ w_o
    return y, k[:, L - CACHE_ROWS:], v[:, L - CACHE_ROWS:]


def swa_sample(x, cache_k, cache_v, w_qkv, b_qkv, sinks, w_o):
    B, T, _ = x.shape
    q, k, v = attn_qkv(x, w_qkv, b_qkv)
    k_all = jnp.concatenate([cache_k.astype(k.dtype), k], axis=1)
    v_all = jnp.concatenate([cache_v.astype(v.dtype), v], axis=1)
    valid = jnp.ones((1, CACHE_ROWS + T), dtype=bool)
    o = sink_attention(q[:, None], k_all[:, None], v_all[:, None], valid, sinks)
    y = o.reshape(B, T, N_HEADS * HEAD_DIM) @ w_o
    return y, k_all[:, T:], v_all[:, T:]


def l2norm(t):
    tf = t.astype(jnp.float32)
    return tf * lax.rsqrt(jnp.sum(tf * tf, -1, keepdims=True) + 1e-6)


def causal_dwconv(xpad, w):
    return lax.conv_general_dilated(xpad, w[:, None, :].astype(xpad.dtype), (1,), 'VALID',
                                    dimension_numbers=('NWC', 'WIO', 'NWC'),
                                    feature_group_count=xpad.shape[-1])


def gated_delta_rule(q, k, v, g, beta, s0):
    B, L, H, _ = q.shape
    c = min(DN_BLOCK, L)
    n = L // c

    def blocks(t):
        t = t.reshape((B, n, c, H) + t.shape[3:])
        return jnp.moveaxis(t, (1, 3), (0, 2))

    qb, kb, vb, bb = blocks(q), blocks(k), blocks(v), blocks(beta)
    gb = jnp.cumsum(blocks(g), axis=-1)
    causal = jnp.tril(jnp.ones((c, c), dtype=bool))
    strict = jnp.tril(jnp.ones((c, c), dtype=bool), -1)
    diff = gb[..., :, None] - gb[..., None, :]
    decay = jnp.where(causal, jnp.exp(jnp.where(causal, diff, 0.0)), 0.0)
    kbeta = kb * bb[..., None]
    lmat = jnp.where(strict, jnp.einsum('nbhid,nbhjd->nbhij', kbeta, kb) * decay, 0.0)
    a = lmat + jnp.eye(c, dtype=lmat.dtype)
    rhs = jnp.concatenate([vb * bb[..., None], kbeta * jnp.exp(gb)[..., None]], axis=-1)
    sol = lax.linalg.triangular_solve(a, rhs, left_side=True, lower=True)
    u, w = sol[..., :DN_DV], sol[..., DN_DV:]
    qk = jnp.einsum('nbhid,nbhjd->nbhij', qb, kb) * decay

    def step(s, inp):
        q_i, k_i, u_i, w_i, qk_i, g_i = inp
        v_new = u_i - jnp.einsum('bhck,bhkv->bhcv', w_i, s)
        o = (jnp.einsum('bhck,bhkv->bhcv', q_i * jnp.exp(g_i)[..., None], s)
             + jnp.einsum('bhij,bhjv->bhiv', qk_i, v_new))
        g_last = g_i[..., -1]
        s = (s * jnp.exp(g_last)[..., None, None]
             + jnp.einsum('bhck,bhcv->bhkv', k_i * jnp.exp(g_last[..., None] - g_i)[..., None], v_new))
        return s, o

    s, o = lax.scan(step, s0, (qb, kb, u, w, qk, gb))
    o = jnp.moveaxis(o, (0, 2), (1, 3)).reshape(B, L, H, DN_DV)
    return o, s


def gated_delta_mixer(x, conv_state, s0, w_in, conv_w, a_log, dt_bias, norm_w, w_out):
    B, L, _ = x.shape
    f32 = jnp.float32
    proj = x @ w_in
    mix, z, a, b = jnp.split(proj, [CONV_DIM, CONV_DIM + DN_VAL_DIM, CONV_DIM + DN_VAL_DIM + DN_V_HEADS], axis=-1)
    xpad = jnp.concatenate([conv_state.astype(mix.dtype), mix], axis=1)
    new_conv = xpad[:, L:]
    mix = jax.nn.silu(causal_dwconv(xpad, conv_w))
    q, k, v = jnp.split(mix, [DN_KEY_DIM, 2 * DN_KEY_DIM], axis=-1)
    rep = DN_V_HEADS // DN_QK_HEADS
    q = jnp.repeat(l2norm(q.reshape(B, L, DN_QK_HEADS, DN_DK)), rep, axis=2) * (DN_DK ** -0.5)
    k = jnp.repeat(l2norm(k.reshape(B, L, DN_QK_HEADS, DN_DK)), rep, axis=2)
    v = v.reshape(B, L, DN_V_HEADS, DN_DV).astype(f32)
    beta = jax.nn.sigmoid(b.astype(f32))
    g = -jnp.exp(a_log.astype(f32)) * jax.nn.softplus(a.astype(f32) + dt_bias.astype(f32))
    o, s = gated_delta_rule(q, k, v, g, beta, s0.astype(f32))
    o = o * lax.rsqrt(jnp.mean(o * o, -1, keepdims=True) + RMS_EPS) * norm_w.astype(f32)
    o = o * jax.nn.silu(z.astype(f32).reshape(B, L, DN_V_HEADS, DN_DV))
    y = o.astype(x.dtype).reshape(B, L, DN_VAL_DIM) @ w_out
    return y, new_conv, s.astype(x.dtype)


def setup_inputs(seed: int = 0) -> dict:
    key = jax.random.key(seed)
    ks = jax.random.split(key, 24)
    f32 = jnp.float32

    def nrm(kk, shape, scale):
        return jax.random.normal(kk, shape, f32) * scale

    na, nd = N_ATTN_LAYERS, N_DELTA_LAYERS
    dt = jnp.exp(jax.random.uniform(ks[12], (nd, DN_V_HEADS), f32, math.log(1e-3), math.log(1e-1)))
    return {
        "x_prompt": nrm(ks[0], (BATCH, SEQ, D_MODEL), 1.0),
        "x_sample": nrm(ks[1], (DEC_BATCH, DEC_SEQ, D_MODEL), 1.0),
        "cache_k": nrm(ks[2], (na, DEC_BATCH, CACHE_ROWS, N_KV_HEADS, HEAD_DIM), 1.0),
        "cache_v": nrm(ks[3], (na, DEC_BATCH, CACHE_ROWS, N_KV_HEADS, HEAD_DIM), 1.0),
        "state_conv": nrm(ks[4], (nd, DEC_BATCH, CONV_WIDTH - 1, CONV_DIM), 1.0),
        "state_ssm": nrm(ks[5], (nd, DEC_BATCH, DN_V_HEADS, DN_DK, DN_DV), 0.1),
        "p_prompt": nrm(ks[6], (DEPTH, BATCH, SEQ, PLE_DIM), 1.0),
        "p_sample": nrm(ks[7], (DEPTH, DEC_BATCH, DEC_SEQ, PLE_DIM), 1.0),
        "w_qkv": nrm(ks[8], (na, D_MODEL, QKV_DIM), D_MODEL ** -0.5),
        "b_qkv": nrm(ks[9], (na, QKV_DIM), 0.02),
        "attn_sinks": nrm(ks[10], (na, N_HEADS), 0.5),
        "w_attn_out": nrm(ks[11], (na, N_HEADS * HEAD_DIM, D_MODEL), BETA * (N_HEADS * HEAD_DIM) ** -0.5),
        "w_dn_in": nrm(ks[13], (nd, D_MODEL, DN_IN_DIM), D_MODEL ** -0.5),
        "dn_conv_w": nrm(ks[14], (nd, CONV_WIDTH, CONV_DIM), CONV_WIDTH ** -0.5),
        "dn_a_log": jnp.log(jax.random.uniform(ks[15], (nd, DN_V_HEADS), f32, 1.0, 16.0)),
        "dn_dt_bias": dt + jnp.log(-jnp.expm1(-dt)),
        "dn_norm_w": 1.0 + nrm(ks[16], (nd, DN_DV), 0.02),
        "w_dn_out": nrm(ks[17], (nd, DN_VAL_DIM, D_MODEL), BETA * DN_VAL_DIM ** -0.5),
        "w_ffn_gu": nrm(ks[18], (DEPTH, D_MODEL, 2 * D_FF), D_MODEL ** -0.5),
        "w_ffn_down": nrm(ks[19], (DEPTH, D_FF, D_MODEL), BETA * D_FF ** -0.5),
        "w_ple_gate": nrm(ks[20], (DEPTH, D_MODEL, D_MODEL), D_MODEL ** -0.5),
        "w_ple_proj": nrm(ks[21], (DEPTH, PLE_DIM, D_MODEL), BETA * PLE_DIM ** -0.5),
        "ln_g": 1.0 + nrm(ks[22], (DEPTH, 2, D_MODEL), 0.02),
        "ln_b": nrm(ks[23], (DEPTH, 2, D_MODEL), 0.02),
    }


def reference(x_prompt, x_sample, cache_k, cache_v, state_conv, state_ssm, p_prompt, p_sample,
              w_qkv, b_qkv, attn_sinks, w_attn_out,
              w_dn_in, dn_conv_w, dn_a_log, dn_dt_bias, dn_norm_w, w_dn_out,
              w_ffn_gu, w_ffn_down, w_ple_gate, w_ple_proj, ln_g, ln_b):
    xp, xs = x_prompt, x_sample
    kp_l, vp_l, ks_l, vs_l = [], [], [], []
    cp_l, sp_l, cs_l, ss_l = [], [], [], []
    for i in range(DEPTH):
        j = i // N_MIXERS
        if i % N_MIXERS == 0:
            hp, kp, vp = swa_prompt(xp, w_qkv[j], b_qkv[j], attn_sinks[j], w_attn_out[j])
            hs, kk, vv = swa_sample(xs, cache_k[j], cache_v[j], w_qkv[j], b_qkv[j], attn_sinks[j], w_attn_out[j])
            kp_l.append(kp); vp_l.append(vp); ks_l.append(kk); vs_l.append(vv)
        else:
            bp = xp.shape[0]
            conv0 = jnp.zeros((bp, CONV_WIDTH - 1, CONV_DIM), xp.dtype)
            s0 = jnp.zeros((bp, DN_V_HEADS, DN_DK, DN_DV), xp.dtype)
            hp, cp, sp = gated_delta_mixer(xp, conv0, s0, w_dn_in[j], dn_conv_w[j], dn_a_log[j],
                                           dn_dt_bias[j], dn_norm_w[j], w_dn_out[j])
            hs, cs, ss = gated_delta_mixer(xs, state_conv[j], state_ssm[j], w_dn_in[j], dn_conv_w[j], dn_a_log[j],
                                           dn_dt_bias[j], dn_norm_w[j], w_dn_out[j])
            cp_l.append(cp); sp_l.append(sp); cs_l.append(cs); ss_l.append(ss)
        xp = layer_norm(ALPHA * xp + hp, ln_g[i, 0], ln_b[i, 0])
        xs = layer_norm(ALPHA * xs + hs, ln_g[i, 0], ln_b[i, 0])
        xp = channel_block(xp, p_prompt[i], w_ffn_gu[i], w_ffn_down[i], w_ple_gate[i], w_ple_proj[i], ln_g[i, 1], ln_b[i, 1])
        xs = channel_block(xs, p_sample[i], w_ffn_gu[i], w_ffn_down[i], w_ple_gate[i], w_ple_proj[i], ln_g[i, 1], ln_b[i, 1])
    return (xp, xs,
            jnp.stack(kp_l), jnp.stack(vp_l), jnp.stack(ks_l), jnp.stack(vs_l),
            jnp.stack(cp_l), jnp.stack(sp_l), jnp.stack(cs_l), jnp.stack(ss_l))
```

```python
import functools

import jax
import jax.numpy as jnp
from jax import lax
from jax.experimental import pallas as pl
from jax.experimental.pallas import tpu as pltpu

F32 = jnp.float32
BF16 = jnp.bfloat16

D_MODEL = 1024
SEQ = 16384
DEC_BATCH = 8
DEC_SEQ = 64
TP = SEQ
TS = DEC_BATCH * DEC_SEQ
T = TP + TS
CHUNK = 64
NPC = TP // CHUNK
NCH = T // CHUNK

N_HEADS = 16
N_KV_HEADS = 2
HEAD_DIM = 64
GROUP = N_HEADS // N_KV_HEADS
WINDOW = 128
KV_DIM = 2 * N_KV_HEADS * HEAD_DIM
Q_DIM = N_HEADS * HEAD_DIM

DN_QK_HEADS = 8
DN_V_HEADS = 16
DN_DK = 128
DN_DV = 128
DN_KEY_DIM = DN_QK_HEADS * DN_DK
DN_VAL_DIM = DN_V_HEADS * DN_DV
CONV_DIM = 2 * DN_KEY_DIM + DN_VAL_DIM
CONV_WIDTH = 4

D_FF = 2816
PLE_DIM = 256
DEPTH = 2
ALPHA = (2 * DEPTH) ** 0.25
LN_EPS = 1e-5
RMS_EPS = 1e-6
NEG_INF = -1e30

LANES = 128
SUBLANES = 8
TM = 512
VMEM_LIMIT = 56 * 1024 * 1024


def _params(vmem=VMEM_LIMIT):
    return pltpu.CompilerParams(dimension_semantics=("arbitrary",), vmem_limit_bytes=vmem)


def _full(shape):
    nd = len(shape)
    return pl.BlockSpec(shape, lambda i: (0,) * nd, pipeline_mode=pl.Buffered(1))


def _rows(tm, width):
    return pl.BlockSpec((tm, width), lambda i: (i, 0))


def _sigmoid(x):
    return 1.0 / (1.0 + jnp.exp(-x))


def _layer_norm(y, g, b):
    mu = jnp.mean(y, axis=-1, keepdims=True)
    d = y - mu
    var = jnp.mean(d * d, axis=-1, keepdims=True)
    return d * lax.rsqrt(var + LN_EPS) * g + b


def _qkv_kernel(x_ref, w_ref, b_ref, q_ref, kv_ref):
    xb = x_ref[...].astype(BF16)
    y = jnp.dot(xb, w_ref[...], preferred_element_type=F32) + b_ref[...]
    q_ref[...] = y[:, :Q_DIM].astype(BF16)
    kv_ref[...] = y[:, Q_DIM:]


def _qkv_proj(x, w, b):
    return pl.pallas_call(
        _qkv_kernel,
        grid=(T // TM,),
        in_specs=[_rows(TM, D_MODEL), _full(w.shape), _full(b.shape)],
        out_specs=[_rows(TM, Q_DIM), _rows(TM, KV_DIM)],
        out_shape=[jax.ShapeDtypeStruct((T, Q_DIM), BF16), jax.ShapeDtypeStruct((T, KV_DIM), F32)],
        compiler_params=_params(),
        name="qkv_proj",
    )(x, w, b)


def _attn_kernel(sink_ref, q_ref, kv0_ref, kv1_ref, kv2_ref, o_ref):
    g = pl.program_id(0)
    band = 3 * CHUNK
    start = jnp.where(g < NPC, jnp.maximum(2 - g, 0) * CHUNK, 0)
    valid = lax.broadcasted_iota(jnp.int32, (CHUNK, band), 1) >= start
    kv = jnp.concatenate([kv0_ref[...], kv1_ref[...], kv2_ref[...]], axis=0).astype(BF16)
    q = q_ref[...]
    for kh in range(N_KV_HEADS):
        kk = kv[:, kh * HEAD_DIM:(kh + 1) * HEAD_DIM]
        vv = kv[:, (N_KV_HEADS + kh) * HEAD_DIM:(N_KV_HEADS + kh + 1) * HEAD_DIM]
        for gi in range(GROUP):
            h = kh * GROUP + gi
            qh = q[:, h * HEAD_DIM:(h + 1) * HEAD_DIM]
            s = lax.dot_general(qh, kk, (((1,), (1,)), ((), ())), preferred_element_type=F32)
            s = jnp.where(valid, s * (HEAD_DIM ** -0.5), NEG_INF)
            sink = sink_ref[h]
            m = jnp.maximum(jnp.max(s, axis=-1, keepdims=True), sink)
            p = jnp.exp(s - m)
            denom = jnp.sum(p, axis=-1, keepdims=True) + jnp.exp(sink - m)
            p = p / denom
            oh = jnp.dot(p.astype(BF16), vv, preferred_element_type=F32)
            o_ref[:, h * HEAD_DIM:(h + 1) * HEAD_DIM] = oh.astype(BF16)


def _kv_band_map(j):
    def index_map(g):
        prompt_blk = jnp.maximum(g - 2 + j, 0)
        sample_blk = NPC + 3 * (g - NPC) + j
        return (jnp.where(g < NPC, prompt_blk, sample_blk), 0)
    return index_map


def _attention(sinks, q, kvx):
    return pl.pallas_call(
        _attn_kernel,
        grid=(NCH,),
        in_specs=[pl.BlockSpec(memory_space=pltpu.SMEM),
                  _rows(CHUNK, Q_DIM),
                  pl.BlockSpec((CHUNK, KV_DIM), _kv_band_map(0)),
                  pl.BlockSpec((CHUNK, KV_DIM), _kv_band_map(1)),
                  pl.BlockSpec((CHUNK, KV_DIM), _kv_band_map(2))],
        out_specs=_rows(CHUNK, Q_DIM),
        out_shape=jax.ShapeDtypeStruct((T, Q_DIM), BF16),
        compiler_params=_params(),
        name="swa_attn",
    )(sinks, q, kvx, kvx, kvx)


def _proj_ln_kernel(a_ref, w_ref, x_ref, g_ref, b_ref, o_ref):
    y = jnp.dot(a_ref[...], w_ref[...], preferred_element_type=F32)
    o_ref[...] = _layer_norm(ALPHA * x_ref[...] + y, g_ref[...], b_ref[...])


def _proj_ln(a, w, x, g, b):
    k = a.shape[1]
    return pl.pallas_call(
        _proj_ln_kernel,
        grid=(T // TM,),
        in_specs=[_rows(TM, k), _full(w.shape), _rows(TM, D_MODEL), _full(g.shape), _full(b.shape)],
        out_specs=_rows(TM, D_MODEL),
        out_shape=jax.ShapeDtypeStruct((T, D_MODEL), F32),
        compiler_params=_params(),
        name="proj_ln",
    )(a, w, x, g, b)


FF_CHUNK = 256


def _ffn_kernel(x_ref, p_ref, wgu_ref, wd_ref, wg_ref, wp_ref, g_ref, b_ref, o_ref, act_ref):
    x = x_ref[...]
    xb = x.astype(BF16)
    for c in range(D_FF // FF_CHUNK):
        lo = c * FF_CHUNK
        gate = jnp.dot(xb, wgu_ref[:, lo:lo + FF_CHUNK], preferred_element_type=F32)
        up = jnp.dot(xb, wgu_ref[:, D_FF + lo:D_FF + lo + FF_CHUNK], preferred_element_type=F32)
        act_ref[:, lo:lo + FF_CHUNK] = (gate * _sigmoid(gate) * up).astype(BF16)
    f = jnp.dot(act_ref[...], wd_ref[...], preferred_element_type=F32)
    e = _sigmoid(jnp.dot(xb, wg_ref[...], preferred_element_type=F32)) * jnp.dot(
        p_ref[...].astype(BF16), wp_ref[...], preferred_element_type=F32)
    o_ref[...] = _layer_norm(ALPHA * x + f + e, g_ref[...], b_ref[...])


def _ffn(x, p, wgu, wd, wg, wp, g, b):
    return pl.pallas_call(
        _ffn_kernel,
        grid=(T // TM,),
        in_specs=[_rows(TM, D_MODEL), _rows(TM, PLE_DIM), _full(wgu.shape), _full(wd.shape),
                  _full(wg.shape), _full(wp.shape), _full(g.shape), _full(b.shape)],
        out_specs=_rows(TM, D_MODEL),
        out_shape=jax.ShapeDtypeStruct((T, D_MODEL), F32),
        scratch_shapes=[pltpu.VMEM((TM, D_FF), BF16)],
        compiler_params=_params(),
        name="ffn",
    )(x, p, wgu, wd, wg, wp, g, b)


DN_COLS = 512
HIST = SUBLANES


def _dn_in_kernel(x_ref, st_ref, wmix_ref, wz_ref, wab_ref, cw_ref, alog_ref, dtb_ref, *rest,
                  tm, steps_per_seq, n_alias):
    qn_ref, kn_ref, v_ref, z_ref, gates_ref, cst_ref, ext_ref = rest[n_alias:]
    i = pl.program_id(0)

    @pl.when(i % steps_per_seq == 0)
    def _():
        ext_ref[0:HIST, :] = st_ref[0]

    xb = x_ref[...].astype(BF16)
    for c in range(CONV_DIM // DN_COLS):
        lo = c * DN_COLS
        cols = slice(lo, lo + DN_COLS)
        ext_ref[HIST:HIST + tm, cols] = jnp.dot(xb, wmix_ref[:, cols], preferred_element_type=F32)
        cw = cw_ref[:, cols]
        conv = cw[3:4] * ext_ref[HIST:HIST + tm, cols]
        for j in range(1, CONV_WIDTH):
            conv = conv + cw[3 - j:4 - j] * ext_ref[HIST - j:HIST - j + tm, cols]
        y = conv * _sigmoid(conv)
        if lo < 2 * DN_KEY_DIM:
            is_q = lo < DN_KEY_DIM
            dst = qn_ref if is_q else kn_ref
            base = lo if is_q else lo - DN_KEY_DIM
            for t in range(DN_COLS // DN_DK):
                yt = y[:, t * DN_DK:(t + 1) * DN_DK]
                n = yt * lax.rsqrt(jnp.sum(yt * yt, axis=-1, keepdims=True) + 1e-6)
                if is_q:
                    n = n * (DN_DK ** -0.5)
                dst[:, base + t * DN_DK:base + (t + 1) * DN_DK] = n.astype(BF16)
        else:
            v_ref[:, lo - 2 * DN_KEY_DIM:lo - 2 * DN_KEY_DIM + DN_COLS] = y.astype(BF16)
    last = ext_ref[tm:tm + HIST, :]
    ext_ref[0:HIST, :] = last
    cst_ref[0] = last

    for c in range(DN_VAL_DIM // DN_COLS):
        cols = slice(c * DN_COLS, (c + 1) * DN_COLS)
        z_ref[:, cols] = jnp.dot(xb, wz_ref[:, cols], preferred_element_type=F32).astype(BF16)

    ab = jnp.dot(xb, wab_ref[...], preferred_element_type=F32)
    sp_in = ab + dtb_ref[...]
    softplus = jnp.maximum(sp_in, 0.0) + jnp.log1p(jnp.exp(-jnp.abs(sp_in)))
    gval = -jnp.exp(alog_ref[...]) * softplus
    lane = lax.broadcasted_iota(jnp.int32, ab.shape, 1)
    gates_ref[...] = jnp.where(lane < DN_V_HEADS, gval, _sigmoid(ab))


def _dn_in(x, state, wmix, wz, wab, cw, alog, dtb, *, tm, row_block0, n_seq, prev=None):
    steps_per_seq = (TP // tm) if n_seq == 1 else 1
    n_steps = steps_per_seq * n_seq
    n_alias = 0 if prev is None else len(prev)
    row = lambda width: pl.BlockSpec((tm, width), lambda i: (row_block0 + i, 0))
    state_spec = pl.BlockSpec((1, HIST, CONV_DIM), lambda i: (i // steps_per_seq, 0, 0))
    in_specs = [row(D_MODEL), state_spec, _full(wmix.shape), _full(wz.shape), _full(wab.shape),
                _full(cw.shape), _full(alog.shape), _full(dtb.shape)]
    in_specs += [pl.BlockSpec(memory_space=pl.ANY)] * n_alias
    out_shape = [jax.ShapeDtypeStruct((T, DN_KEY_DIM), BF16), jax.ShapeDtypeStruct((T, DN_KEY_DIM), BF16),
                 jax.ShapeDtypeStruct((T, DN_VAL_DIM), BF16), jax.ShapeDtypeStruct((T, DN_VAL_DIM), BF16),
                 jax.ShapeDtypeStruct((T, LANES), F32),
                 jax.ShapeDtypeStruct((n_seq, HIST, CONV_DIM), F32)]
    out_specs = [row(DN_KEY_DIM), row(DN_KEY_DIM), row(DN_VAL_DIM), row(DN_VAL_DIM), row(LANES), state_spec]
    args = (x, state, wmix, wz, wab, cw, alog, dtb) + (tuple(prev) if prev is not None else ())
    return pl.pallas_call(
        functools.partial(_dn_in_kernel, tm=tm, steps_per_seq=steps_per_seq, n_alias=n_alias),
        grid=(n_steps,),
        in_specs=in_specs,
        out_specs=out_specs,
        out_shape=out_shape,
        scratch_shapes=[pltpu.VMEM((tm + HIST, CONV_DIM), F32)],
        input_output_aliases={8 + k: k for k in range(n_alias)},
        compiler_params=_params(),
        name="dn_in_prompt" if n_seq == 1 else "dn_in_sample",
    )(*args)


def _split3(x):
    hi = x.astype(BF16)
    r = x - hi.astype(F32)
    mid = r.astype(BF16)
    lo = (r - mid.astype(F32)).astype(BF16)
    return hi, mid, lo


def _unit_lower_inverse(lmat, eye):
    c = lmat.shape[0]
    lb = lmat.astype(BF16)
    m = jnp.dot(lb, lb, preferred_element_type=F32)
    p = eye - lmat
    power = 2
    while 2 * power < c:
        pm = jnp.dot(jnp.concatenate([p, m], axis=0).astype(BF16), m.astype(BF16),
                     preferred_element_type=F32)
        p = p + pm[:c]
        m = pm[c:]
        power *= 2
    return p + jnp.dot(p.astype(BF16), m.astype(BF16), preferred_element_type=F32)


def _gdn_kernel(q_ref, k_ref, v_ref, z_ref, gates_ref, nw_ref, sinit_ref, o_ref, s_ref):
    g = pl.program_id(0)

    @pl.when((g == 0) | (g >= NPC))
    def _():
        s_ref[...] = sinit_ref[...]

    c = CHUNK
    ri = lax.broadcasted_iota(jnp.int32, (c, c), 0)
    ci = lax.broadcasted_iota(jnp.int32, (c, c), 1)
    causal = ri >= ci
    strict = ri > ci
    eye = jnp.where(ri == ci, 1.0, 0.0).astype(F32)
    tril = jnp.where(causal, 1.0, 0.0).astype(BF16)

    gates = gates_ref[...]
    cum = sum(jnp.dot(tril, part, preferred_element_type=F32) for part in _split3(gates))
    cum_t = jnp.concatenate([cum, jnp.zeros_like(cum)], axis=0).T
    nw = nw_ref[...]

    for j in range(DN_QK_HEADS):
        kb = k_ref[:, j * DN_DK:(j + 1) * DN_DK]
        qb = q_ref[:, j * DN_DK:(j + 1) * DN_DK]
        kq = lax.dot_general(jnp.concatenate([kb, qb], axis=0), kb, (((1,), (1,)), ((), ())),
                             preferred_element_type=F32)
        kk = kq[:c]
        qk = kq[c:]
        kf = kb.astype(F32)
        qf = qb.astype(F32)
        for h in range(2 * j, 2 * j + 2):
            gbc = cum[:, h:h + 1]
            gbr = cum_t[h:h + 1, :c]
            beta = gates[:, DN_V_HEADS + h:DN_V_HEADS + h + 1]
            glast = cum[c - 1:c, h:h + 1]
            decay = jnp.where(causal, jnp.exp(jnp.where(causal, gbc - gbr, 0.0)), 0.0)
            lmat = jnp.where(strict, kk * beta * decay, 0.0)
            tinv = _unit_lower_inverse(lmat, eye)
            egb = jnp.exp(gbc)
            vf = v_ref[:, h * DN_DV:(h + 1) * DN_DV].astype(F32)
            rhs = jnp.concatenate([vf * beta, kf * (beta * egb)], axis=1)
            sol = jnp.dot(tinv.astype(BF16), rhs.astype(BF16), preferred_element_type=F32)
            u = sol[:, :DN_DV]
            w = sol[:, DN_DV:]
            s = s_ref[0, h]
            wq = jnp.concatenate([w, qf * egb], axis=0).astype(BF16)
            wq_s = jnp.dot(wq, s.astype(BF16), preferred_element_type=F32)
            v_new = u - wq_s[:c]
            vnb = v_new.astype(BF16)
            o = wq_s[c:] + jnp.dot((qk * decay).astype(BF16), vnb, preferred_element_type=F32)
            kdec = (kf * jnp.exp(glast - gbc)).astype(BF16)
            s_ref[0, h] = s * jnp.exp(glast) + lax.dot_general(
                kdec, vnb, (((0,), (0,)), ((), ())), preferred_element_type=F32)
            o = o * lax.rsqrt(jnp.mean(o * o, axis=-1, keepdims=True) + RMS_EPS) * nw
            zf = z_ref[:, h * DN_DV:(h + 1) * DN_DV].astype(F32)
            o_ref[:, h * DN_DV:(h + 1) * DN_DV] = (o * (zf * _sigmoid(zf))).astype(BF16)


def _gdn(qn, kn, v, z, gates, nw, s_init):
    seq_map = lambda g: (jnp.where(g < NPC, 0, g - NPC + 1), 0, 0, 0)
    state_spec = pl.BlockSpec((1, DN_V_HEADS, DN_DK, DN_DV), seq_map)
    return pl.pallas_call(
        _gdn_kernel,
        grid=(NCH,),
        in_specs=[_rows(CHUNK, DN_KEY_DIM), _rows(CHUNK, DN_KEY_DIM), _rows(CHUNK, DN_VAL_DIM),
                  _rows(CHUNK, DN_VAL_DIM), _rows(CHUNK, LANES), _full(nw.shape), state_spec],
        out_specs=[_rows(CHUNK, DN_VAL_DIM), state_spec],
        out_shape=[jax.ShapeDtypeStruct((T, DN_VAL_DIM), BF16),
                   jax.ShapeDtypeStruct(s_init.shape, F32)],
        compiler_params=_params(),
        name="gdn_core",
    )(qn, kn, v, z, gates, nw, s_init)


def _lane_row(vec):
    return jnp.pad(vec.astype(F32), (0, LANES - vec.shape[0])).reshape(1, LANES)


def kernel(x_prompt, x_sample, cache_k, cache_v, state_conv, state_ssm, p_prompt, p_sample, w_qkv, b_qkv, attn_sinks, w_attn_out, w_dn_in, dn_conv_w, dn_a_log, dn_dt_bias, dn_norm_w, w_dn_out, w_ffn_gu, w_ffn_down, w_ple_gate, w_ple_proj, ln_g, ln_b):
    x = jnp.concatenate([x_prompt.reshape(TP, D_MODEL), x_sample.reshape(TS, D_MODEL)], axis=0)
    p_all = jnp.concatenate([p_prompt.reshape(DEPTH, TP, PLE_DIM), p_sample.reshape(DEPTH, TS, PLE_DIM)], axis=1)
    row = lambda v: v.reshape(1, -1)

    def channel(xin, i):
        return _ffn(xin, p_all[i], w_ffn_gu[i].astype(BF16), w_ffn_down[i].astype(BF16),
                    w_ple_gate[i].astype(BF16), w_ple_proj[i].astype(BF16), row(ln_g[i, 1]), row(ln_b[i, 1]))

    q, kv = _qkv_proj(x, w_qkv[0].astype(BF16), row(b_qkv[0]))
    kv_new_s = kv[TP:].reshape(DEC_BATCH, DEC_SEQ, KV_DIM)
    cache = jnp.concatenate([cache_k[0].reshape(DEC_BATCH, WINDOW, KV_DIM // 2),
                             cache_v[0].reshape(DEC_BATCH, WINDOW, KV_DIM // 2)], axis=-1)
    kvx = jnp.concatenate([kv[:TP], jnp.concatenate([cache, kv_new_s], axis=1).reshape(-1, KV_DIM)], axis=0)
    o = _attention(attn_sinks[0], q, kvx)
    x = _proj_ln(o, w_attn_out[0].astype(BF16), x, row(ln_g[0, 0]), row(ln_b[0, 0]))
    x = channel(x, 0)

    kv_shape = (N_KV_HEADS, HEAD_DIM)
    half = KV_DIM // 2
    new_k_prompt = kv[TP - WINDOW:TP, :half].reshape((1, 1, WINDOW) + kv_shape)
    new_v_prompt = kv[TP - WINDOW:TP, half:].reshape((1, 1, WINDOW) + kv_shape)
    new_k_sample = jnp.concatenate([cache_k[0][:, DEC_SEQ:], kv_new_s[..., :half].reshape((DEC_BATCH, DEC_SEQ) + kv_shape)], axis=1)[None]
    new_v_sample = jnp.concatenate([cache_v[0][:, DEC_SEQ:], kv_new_s[..., half:].reshape((DEC_BATCH, DEC_SEQ) + kv_shape)], axis=1)[None]

    w_in = w_dn_in[0]
    wmix = w_in[:, :CONV_DIM].astype(BF16)
    wz = w_in[:, CONV_DIM:CONV_DIM + DN_VAL_DIM].astype(BF16)
    wab = jnp.pad(w_in[:, CONV_DIM + DN_VAL_DIM:], ((0, 0), (0, LANES - 2 * DN_V_HEADS))).astype(BF16)
    alog = _lane_row(dn_a_log[0])
    dtb = _lane_row(dn_dt_bias[0])
    hist_pad = ((0, 0), (HIST - (CONV_WIDTH - 1), 0), (0, 0))
    st_prompt = jnp.zeros((1, HIST, CONV_DIM), F32)
    st_sample = jnp.pad(state_conv[0], hist_pad)
    dn_w = (wmix, wz, wab, dn_conv_w[0], alog, dtb)
    *tok_p, cst_p = _dn_in(x, st_prompt, *dn_w, tm=TM, row_block0=0, n_seq=1)
    *tok, cst_s = _dn_in(x, st_sample, *dn_w, tm=DEC_SEQ, row_block0=NPC, n_seq=DEC_BATCH, prev=tok_p)
    qn, kn, v, z, gates = tok
    s_init = jnp.concatenate([jnp.zeros((1,) + state_ssm.shape[2:], F32), state_ssm[0]], axis=0)
    o, s_out = _gdn(qn, kn, v, z, gates, row(dn_norm_w[0]), s_init)
    x = _proj_ln(o, w_dn_out[0].astype(BF16), x, row(ln_g[1, 0]), row(ln_b[1, 0]))
    x = channel(x, 1)

    tail = HIST - (CONV_WIDTH - 1)
    return (x[:TP].reshape(1, TP, D_MODEL), x[TP:].reshape(DEC_BATCH, DEC_SEQ, D_MODEL),
            new_k_prompt, new_v_prompt, new_k_sample, new_v_sample,
            cst_p[:, tail:][None], s_out[:1][None], cst_s[:, tail:][None], s_out[1:][None])
```

```python
import functools

import jax
import jax.numpy as jnp
from jax import lax
from jax.experimental import pallas as pl
from jax.experimental.pallas import tpu as pltpu

F32 = jnp.float32
BF16 = jnp.bfloat16

D_MODEL = 1024
SEQ = 16384
DEC_BATCH = 8
DEC_SEQ = 64
TP = SEQ
TS = DEC_BATCH * DEC_SEQ
T = TP + TS
CHUNK = 64
NPC = TP // CHUNK
NCH = T // CHUNK

N_HEADS = 16
N_KV_HEADS = 2
HEAD_DIM = 64
WINDOW = 128
Q_DIM = N_HEADS * HEAD_DIM
KVX_DIM = 2 * N_KV_HEADS * 2 * HEAD_DIM

DN_QK_HEADS = 8
DN_V_HEADS = 16
DN_DK = 128
DN_DV = 128
DN_KEY_DIM = DN_QK_HEADS * DN_DK
DN_VAL_DIM = DN_V_HEADS * DN_DV
CONV_DIM = 2 * DN_KEY_DIM + DN_VAL_DIM
CONV_WIDTH = 4

D_FF = 2816
PLE_DIM = 256
DEPTH = 2
ALPHA = (2 * DEPTH) ** 0.25
LN_EPS = 1e-5
RMS_EPS = 1e-6
NEG_INF = -1e30

LANES = 128
SUBLANES = 8
MXU_DIM = 256
TM = 512
WY_CHUNKS = 4
REC_CHUNKS = 4
VMEM_LIMIT = 56 * 1024 * 1024


def _params(vmem=VMEM_LIMIT):
    return pltpu.CompilerParams(dimension_semantics=("arbitrary",), vmem_limit_bytes=vmem)


def _full(shape):
    nd = len(shape)
    return pl.BlockSpec(shape, lambda i: (0,) * nd, pipeline_mode=pl.Buffered(1))


def _rows(tm, width):
    return pl.BlockSpec((tm, width), lambda i: (i, 0))


def _sigmoid(x):
    return 1.0 / (1.0 + jnp.exp(-x))


def _layer_norm(y, g, b):
    mu = jnp.mean(y, axis=-1, keepdims=True)
    d = y - mu
    var = jnp.mean(d * d, axis=-1, keepdims=True)
    return d * lax.rsqrt(var + LN_EPS) * g + b


def _iota(shape, axis):
    return lax.broadcasted_iota(jnp.int32, shape, axis)


def _qkv_kernel(x_ref, w_ref, b_ref, q_ref, kv_ref):
    xb = x_ref[...].astype(BF16)
    y = jnp.dot(xb, w_ref[...], preferred_element_type=F32) + b_ref[...]
    q_ref[...] = y[:, :Q_DIM].astype(BF16)
    kv_ref[...] = y[:, Q_DIM:]


def _qkv_proj(x, w, b):
    return pl.pallas_call(
        _qkv_kernel,
        grid=(T // TM,),
        in_specs=[_rows(TM, D_MODEL), _full(w.shape), _full(b.shape)],
        out_specs=[_rows(TM, Q_DIM), _rows(TM, KVX_DIM)],
        out_shape=[jax.ShapeDtypeStruct((T, Q_DIM), BF16), jax.ShapeDtypeStruct((T, KVX_DIM), F32)],
        compiler_params=_params(),
        name="qkv_proj",
    )(x, w, b)


def _attn_kernel(sink_ref, q_ref, kv0_ref, kv1_ref, kv2_ref, o_ref):
    g = pl.program_id(0)
    band = 3 * CHUNK
    keys = 2 * LANES
    start = jnp.where(g < NPC, jnp.maximum(2 - g, 0) * CHUNK, 0)
    key = _iota((CHUNK, 2 * keys), 1) & (keys - 1)
    valid = (key >= start) & (key < band)
    kvb = jnp.concatenate([kv0_ref[...], kv1_ref[...], kv2_ref[...],
                           jnp.zeros((keys - band, KVX_DIM), F32)], axis=0).astype(BF16)
    lo = _iota((keys, LANES), 1) < HEAD_DIM
    zero = jnp.zeros((), BF16)

    def halves(tile):
        return jnp.concatenate([jnp.where(lo, tile, zero), jnp.where(lo, zero, tile)], axis=0)

    k_ops = [halves(kvb[:, kh * LANES:(kh + 1) * LANES]) for kh in range(N_KV_HEADS)]
    v_ops = [halves(kvb[:, (N_KV_HEADS + kh) * LANES:(N_KV_HEADS + kh + 1) * LANES])
             for kh in range(N_KV_HEADS)]
    pairs = Q_DIM // LANES
    per_kv = pairs // N_KV_HEADS
    scores = [lax.dot_general(q_ref[:, i * LANES:(i + 1) * LANES], k_ops[i // per_kv],
                              (((1,), (1,)), ((), ())), preferred_element_type=F32) for i in range(pairs)]
    probs = []
    for i in range(pairs):
        s = jnp.where(valid, scores[i] * (HEAD_DIM ** -0.5), NEG_INF)
        ps = []
        for hh in range(2):
            sh = s[:, hh * keys:(hh + 1) * keys]
            sink = sink_ref[2 * i + hh]
            m = jnp.maximum(jnp.max(sh, axis=-1, keepdims=True), sink)
            p = jnp.exp(sh - m)
            denom = jnp.sum(p, axis=-1, keepdims=True) + jnp.exp(sink - m)
            ps.append(p * (1.0 / denom))
        probs.append(jnp.concatenate(ps, axis=1).astype(BF16))
    for i in range(pairs):
        o = jnp.dot(probs[i], v_ops[i // per_kv], preferred_element_type=F32)
        o_ref[:, i * LANES:(i + 1) * LANES] = o.astype(BF16)


def _kv_band_map(j):
    def index_map(g):
        prompt_blk = jnp.maximum(g - 2 + j, 0)
        sample_blk = NPC + 3 * (g - NPC) + j
        return (jnp.where(g < NPC, prompt_blk, sample_blk), 0)
    return index_map


def _attention(sinks, q, kvx):
    return pl.pallas_call(
        _attn_kernel,
        grid=(NCH,),
        in_specs=[pl.BlockSpec(memory_space=pltpu.SMEM),
                  _rows(CHUNK, Q_DIM),
                  pl.BlockSpec((CHUNK, KVX_DIM), _kv_band_map(0)),
                  pl.BlockSpec((CHUNK, KVX_DIM), _kv_band_map(1)),
                  pl.BlockSpec((CHUNK, KVX_DIM), _kv_band_map(2))],
        out_specs=_rows(CHUNK, Q_DIM),
        out_shape=jax.ShapeDtypeStruct((T, Q_DIM), BF16),
        compiler_params=_params(),
        name="swa_attn",
    )(sinks, q, kvx, kvx, kvx)


def _proj_ln_kernel(a_ref, w_ref, x_ref, g_ref, b_ref, o_ref):
    y = jnp.dot(a_ref[...], w_ref[...], preferred_element_type=F32)
    o_ref[...] = _layer_norm(ALPHA * x_ref[...] + y, g_ref[...], b_ref[...])


def _proj_ln(a, w, x, g, b):
    k = a.shape[1]
    return pl.pallas_call(
        _proj_ln_kernel,
        grid=(T // TM,),
        in_specs=[_rows(TM, k), _full(w.shape), _rows(TM, D_MODEL), _full(g.shape), _full(b.shape)],
        out_specs=_rows(TM, D_MODEL),
        out_shape=jax.ShapeDtypeStruct((T, D_MODEL), F32),
        compiler_params=_params(),
        name="proj_ln",
    )(a, w, x, g, b)


FF_CHUNK = 256


def _ffn_kernel(x_ref, p_ref, wgu_ref, wd_ref, wg_ref, wp_ref, g_ref, b_ref, o_ref, act_ref):
    x = x_ref[...]
    xb = x.astype(BF16)
    for c in range(D_FF // FF_CHUNK):
        lo = c * FF_CHUNK
        gate = jnp.dot(xb, wgu_ref[:, lo:lo + FF_CHUNK], preferred_element_type=F32)
        up = jnp.dot(xb, wgu_ref[:, D_FF + lo:D_FF + lo + FF_CHUNK], preferred_element_type=F32)
        act_ref[:, lo:lo + FF_CHUNK] = (gate * _sigmoid(gate) * up).astype(BF16)
    f = jnp.dot(act_ref[...], wd_ref[...], preferred_element_type=F32)
    e = _sigmoid(jnp.dot(xb, wg_ref[...], preferred_element_type=F32)) * jnp.dot(
        p_ref[...].astype(BF16), wp_ref[...], preferred_element_type=F32)
    o_ref[...] = _layer_norm(ALPHA * x + f + e, g_ref[...], b_ref[...])


def _ffn(x, p, wgu, wd, wg, wp, g, b):
    return pl.pallas_call(
        _ffn_kernel,
        grid=(T // TM,),
        in_specs=[_rows(TM, D_MODEL), _rows(TM, PLE_DIM), _full(wgu.shape), _full(wd.shape),
                  _full(wg.shape), _full(wp.shape), _full(g.shape), _full(b.shape)],
        out_specs=_rows(TM, D_MODEL),
        out_shape=jax.ShapeDtypeStruct((T, D_MODEL), F32),
        scratch_shapes=[pltpu.VMEM((TM, D_FF), BF16)],
        compiler_params=_params(),
        name="ffn",
    )(x, p, wgu, wd, wg, wp, g, b)


DN_COLS = 512
HIST = SUBLANES


def _dn_in_kernel(x_ref, st_ref, wmix_ref, wz_ref, wab_ref, cw_ref, alog_ref, dtb_ref, *rest,
                  tm, steps_per_seq, n_alias):
    qn_ref, kn_ref, v_ref, z_ref, gates_ref, cst_ref, ext_ref = rest[n_alias:]
    i = pl.program_id(0)

    @pl.when(i % steps_per_seq == 0)
    def _():
        ext_ref[0:HIST, :] = st_ref[0]

    xb = x_ref[...].astype(BF16)
    for c in range(CONV_DIM // DN_COLS):
        lo = c * DN_COLS
        cols = slice(lo, lo + DN_COLS)
        ext_ref[HIST:HIST + tm, cols] = jnp.dot(xb, wmix_ref[:, cols], preferred_element_type=F32)
        cw = cw_ref[:, cols]
        conv = cw[3:4] * ext_ref[HIST:HIST + tm, cols]
        for j in range(1, CONV_WIDTH):
            conv = conv + cw[3 - j:4 - j] * ext_ref[HIST - j:HIST - j + tm, cols]
        y = conv * _sigmoid(conv)
        if lo < 2 * DN_KEY_DIM:
            is_q = lo < DN_KEY_DIM
            dst = qn_ref if is_q else kn_ref
            base = lo if is_q else lo - DN_KEY_DIM
            for t in range(DN_COLS // DN_DK):
                yt = y[:, t * DN_DK:(t + 1) * DN_DK]
                n = yt * lax.rsqrt(jnp.sum(yt * yt, axis=-1, keepdims=True) + 1e-6)
                if is_q:
                    n = n * (DN_DK ** -0.5)
                dst[:, base + t * DN_DK:base + (t + 1) * DN_DK] = n.astype(BF16)
        else:
            v_ref[:, lo - 2 * DN_KEY_DIM:lo - 2 * DN_KEY_DIM + DN_COLS] = y.astype(BF16)
    last = ext_ref[tm:tm + HIST, :]
    ext_ref[0:HIST, :] = last
    cst_ref[0] = last

    for c in range(DN_VAL_DIM // DN_COLS):
        cols = slice(c * DN_COLS, (c + 1) * DN_COLS)
        z_ref[:, cols] = jnp.dot(xb, wz_ref[:, cols], preferred_element_type=F32).astype(BF16)

    ab = jnp.dot(xb, wab_ref[...], preferred_element_type=F32)
    sp_in = ab + dtb_ref[...]
    softplus = jnp.maximum(sp_in, 0.0) + jnp.log1p(jnp.exp(-jnp.abs(sp_in)))
    gval = -jnp.exp(alog_ref[...]) * softplus
    gates_ref[...] = jnp.where(_iota(ab.shape, 1) < DN_V_HEADS, gval, _sigmoid(ab))


def _dn_in(x, state, wmix, wz, wab, cw, alog, dtb, *, tm, row_block0, n_seq, prev=None):
    steps_per_seq = (TP // tm) if n_seq == 1 else 1
    n_steps = steps_per_seq * n_seq
    n_alias = 0 if prev is None else len(prev)
    row = lambda width: pl.BlockSpec((tm, width), lambda i: (row_block0 + i, 0))
    state_spec = pl.BlockSpec((1, HIST, CONV_DIM), lambda i: (i // steps_per_seq, 0, 0))
    in_specs = [row(D_MODEL), state_spec, _full(wmix.shape), _full(wz.shape), _full(wab.shape),
                _full(cw.shape), _full(alog.shape), _full(dtb.shape)]
    in_specs += [pl.BlockSpec(memory_space=pl.ANY)] * n_alias
    out_shape = [jax.ShapeDtypeStruct((T, DN_KEY_DIM), BF16), jax.ShapeDtypeStruct((T, DN_KEY_DIM), BF16),
                 jax.ShapeDtypeStruct((T, DN_VAL_DIM), BF16), jax.ShapeDtypeStruct((T, DN_VAL_DIM), BF16),
                 jax.ShapeDtypeStruct((T, LANES), F32),
                 jax.ShapeDtypeStruct((n_seq, HIST, CONV_DIM), F32)]
    out_specs = [row(DN_KEY_DIM), row(DN_KEY_DIM), row(DN_VAL_DIM), row(DN_VAL_DIM), row(LANES), state_spec]
    args = (x, state, wmix, wz, wab, cw, alog, dtb) + (tuple(prev) if prev is not None else ())
    return pl.pallas_call(
        functools.partial(_dn_in_kernel, tm=tm, steps_per_seq=steps_per_seq, n_alias=n_alias),
        grid=(n_steps,),
        in_specs=in_specs,
        out_specs=out_specs,
        out_shape=out_shape,
        scratch_shapes=[pltpu.VMEM((tm + HIST, CONV_DIM), F32)],
        input_output_aliases={8 + k: k for k in range(n_alias)},
        compiler_params=_params(),
        name="dn_in_prompt" if n_seq == 1 else "dn_in_sample",
    )(*args)


GROUP_HEADS = MXU_DIM // CHUNK


def _split3(x):
    hi = x.astype(BF16)
    r = x - hi.astype(F32)
    mid = r.astype(BF16)
    lo = (r - mid.astype(F32)).astype(BF16)
    return hi, mid, lo


def _unit_lower_inverses(lcats, eye_cat, bd01):
    c = lcats[0].shape[0]

    def mm(a, b):
        b_bd = jnp.concatenate([b.astype(BF16)] * GROUP_HEADS, axis=0) * bd01
        return jnp.dot(a.astype(BF16), b_bd, preferred_element_type=F32)

    ms = [mm(l, l) for l in lcats]
    ps = [eye_cat - l for l in lcats]
    power = 2
    while 2 * power < c:
        pms = [mm(jnp.concatenate([p, m], axis=0), m) for p, m in zip(ps, ms)]
        ps = [p + pm[:c] for p, pm in zip(ps, pms)]
        ms = [pm[c:] for pm in pms]
        power *= 2
    return [p + mm(p, m) for p, m in zip(ps, ms)]


def _gdn_wy_kernel(q_ref, k_ref, v_ref, gates_ref, u_ref, w_ref, qd_ref, kdt_ref, qkd_ref, egl_ref, *, cb):
    c = CHUNK
    grp = GROUP_HEADS * c
    row = _iota((c, LANES), 0)
    lane = _iota((c, LANES), 1)
    col = lane & (c - 1)
    lo = lane < c
    causal2 = row >= col
    strict2 = row > col
    tril = jnp.where(_iota((c, c), 0) >= _iota((c, c), 1), 1.0, 0.0).astype(BF16)
    eye_cat = jnp.where(_iota((c, grp), 0) == (_iota((c, grp), 1) & (c - 1)), 1.0, 0.0).astype(F32)
    shift = c.bit_length() - 1
    bd01 = jnp.where((_iota((grp, grp), 0) >> shift) == (_iota((grp, grp), 1) >> shift), 1.0, 0.0).astype(BF16)
    nt = (((1,), (1,)), ((), ()))

    lcats, rhs = [], []
    for cc in range(cb):
        rows = slice(cc * c, (cc + 1) * c)
        gates = gates_ref[rows, :]
        cum = sum(jnp.dot(tril, part, preferred_element_type=F32) for part in _split3(gates))
        cum_t = jnp.concatenate([cum, cum], axis=0).T
        egl_ref[cc * DN_V_HEADS:(cc + 1) * DN_V_HEADS, :] = jnp.broadcast_to(
            jnp.exp(cum_t[0:DN_V_HEADS, c - 1:c]), (DN_V_HEADS, LANES))
        lmats = []
        for j in range(DN_QK_HEADS):
            a, b = 2 * j, 2 * j + 1
            tile = slice(j * LANES, (j + 1) * LANES)
            kb = k_ref[rows, tile]
            qb = q_ref[rows, tile]
            kq = lax.dot_general(jnp.concatenate([kb, qb], axis=0), jnp.concatenate([kb, kb], axis=0), nt,
                                 preferred_element_type=F32)
            gbc = jnp.where(lo, cum[:, a:a + 1], cum[:, b:b + 1])
            gbr = jnp.where(lo, cum_t[a:a + 1, :], cum_t[b:b + 1, :])
            beta2 = jnp.where(lo, gates[:, DN_V_HEADS + a:DN_V_HEADS + a + 1],
                              gates[:, DN_V_HEADS + b:DN_V_HEADS + b + 1])
            decay = jnp.where(causal2, jnp.exp(jnp.where(causal2, gbc - gbr, 0.0)), 0.0)
            lmats.append(jnp.where(strict2, kq[:c] * beta2 * decay, 0.0))
            qkd_ref[rows, tile] = (kq[c:] * decay).astype(BF16)
            kf = kb.astype(F32)
            qf = qb.astype(F32)
            kdec = []
            for h in (a, b):
                head = slice(h * DN_DV, (h + 1) * DN_DV)
                gb = cum[:, h:h + 1]
                beta = gates[:, DN_V_HEADS + h:DN_V_HEADS + h + 1]
                egb = jnp.exp(gb)
                vf = v_ref[rows, head].astype(F32)
                rhs.append(jnp.concatenate([vf * beta, kf * (beta * egb)], axis=1).astype(BF16))
                qd_ref[rows, head] = (qf * egb).astype(BF16)
                kdec.append(kf * jnp.exp(cum[c - 1:c, h:h + 1] - gb))
            kdt_ref[cc * DN_DK:(cc + 1) * DN_DK, tile] = jnp.concatenate(kdec, axis=0).T.astype(BF16)
        lcats += [jnp.concatenate(lmats[2 * m:2 * m + 2], axis=1) for m in range(DN_V_HEADS // GROUP_HEADS)]

    tinvs = _unit_lower_inverses(lcats, eye_cat, bd01)
    zeros = jnp.zeros((c, 2 * DN_DV), BF16)
    for cc in range(cb):
        rows = slice(cc * c, (cc + 1) * c)
        for j in range(DN_QK_HEADS):
            a = 2 * j
            h0 = cc * DN_V_HEADS + a
            pair = slice(a * DN_DV, (a + 2) * DN_DV)
            tinv = tinvs[cc * (DN_V_HEADS // GROUP_HEADS) + j // 2][:, (j % 2) * LANES:(j % 2 + 1) * LANES]
            rhs_bd = jnp.concatenate([jnp.concatenate([rhs[h0], zeros], axis=1),
                                      jnp.concatenate([zeros, rhs[h0 + 1]], axis=1)], axis=0)
            sol = jnp.dot(tinv.astype(BF16), rhs_bd, preferred_element_type=F32)
            u_ref[rows, pair] = jnp.concatenate(
                [sol[:, 0:DN_DV], sol[:, 2 * DN_DV:3 * DN_DV]], axis=1).astype(BF16)
            w_ref[rows, pair] = jnp.concatenate(
                [sol[:, DN_DV:2 * DN_DV], sol[:, 3 * DN_DV:4 * DN_DV]], axis=1).astype(BF16)


def _gdn_wy(qn, kn, v, gates):
    cb = WY_CHUNKS
    tm = cb * CHUNK
    tok = lambda width: jax.ShapeDtypeStruct((T, width), BF16)
    return pl.pallas_call(
        functools.partial(_gdn_wy_kernel, cb=cb),
        grid=(NCH // cb,),
        in_specs=[_rows(tm, DN_KEY_DIM), _rows(tm, DN_KEY_DIM), _rows(tm, DN_VAL_DIM), _rows(tm, LANES)],
        out_specs=[_rows(tm, DN_VAL_DIM), _rows(tm, DN_VAL_DIM), _rows(tm, DN_VAL_DIM),
                   _rows(cb * DN_DK, DN_V_HEADS * CHUNK), _rows(tm, DN_V_HEADS * CHUNK),
                   _rows(cb * DN_V_HEADS, LANES)],
        out_shape=[tok(DN_VAL_DIM), tok(DN_VAL_DIM), tok(DN_VAL_DIM),
                   jax.ShapeDtypeStruct((NCH * DN_DK, DN_V_HEADS * CHUNK), BF16), tok(DN_V_HEADS * CHUNK),
                   jax.ShapeDtypeStruct((NCH * DN_V_HEADS, LANES), F32)],
        compiler_params=_params(),
        name="gdn_wy",
    )(qn, kn, v, gates)


def _gdn_rec_kernel(u_ref, w_ref, qd_ref, kdt_ref, qkd_ref, egl_ref, z_ref, nw_ref, sinit_ref, *rest,
                    cb, steps_per_seq, n_alias):
    o_ref, s_ref = rest[n_alias:]
    c = CHUNK

    @pl.when(pl.program_id(0) % steps_per_seq == 0)
    def _():
        s_ref[...] = sinit_ref[...]

    nw = nw_ref[...]
    zero_s = jnp.zeros((DN_DK, DN_DV), BF16)
    zero_v = jnp.zeros((c, DN_DV), BF16)
    for cc in range(cb):
        rows = slice(cc * c, (cc + 1) * c)
        egl = egl_ref[cc * DN_V_HEADS:(cc + 1) * DN_V_HEADS, :]
        v_new, o_inter = [], []
        for m in range(DN_V_HEADS // 2):
            a = 2 * m
            pair = slice(a * DN_DV, (a + 2) * DN_DV)
            s_bd = jnp.concatenate(
                [jnp.concatenate([s_ref[0, a].astype(BF16), zero_s], axis=1),
                 jnp.concatenate([zero_s, s_ref[0, a + 1].astype(BF16)], axis=1)], axis=0)
            r = jnp.dot(jnp.concatenate([w_ref[rows, pair], qd_ref[rows, pair]], axis=0), s_bd,
                        preferred_element_type=F32)
            vn = u_ref[rows, pair].astype(F32) - r[:c]
            v_new += [vn[:, :DN_DV].astype(BF16), vn[:, DN_DV:].astype(BF16)]
            o_inter += [r[c:, :DN_DV], r[c:, DN_DV:]]
        for n in range(DN_V_HEADS // GROUP_HEADS):
            heads = range(GROUP_HEADS * n, GROUP_HEADS * (n + 1))
            v_bd = jnp.concatenate(
                [jnp.concatenate([v_new[h] if t == i else zero_v for t in range(GROUP_HEADS)], axis=1)
                 for i, h in enumerate(heads)], axis=0)
            grp = slice(n * GROUP_HEADS * c, (n + 1) * GROUP_HEADS * c)
            r2 = jnp.dot(jnp.concatenate([qkd_ref[rows, grp], kdt_ref[cc * DN_DK:(cc + 1) * DN_DK, grp]], axis=0),
                         v_bd, preferred_element_type=F32)
            for i, h in enumerate(heads):
                head = slice(h * DN_DV, (h + 1) * DN_DV)
                blk = slice(i * DN_DV, (i + 1) * DN_DV)
                s_ref[0, h] = s_ref[0, h] * egl[h:h + 1, :] + r2[c:, blk]
                o = o_inter[h] + r2[:c, blk]
                o = o * lax.rsqrt(jnp.mean(o * o, axis=-1, keepdims=True) + RMS_EPS) * nw
                zf = z_ref[rows, head].astype(F32)
                o_ref[rows, head] = (o * (zf * _sigmoid(zf))).astype(BF16)


def _gdn_rec(u, w, qd, kdt, qkd, egl, z, nw, s_init, *, cb, chunk0, n_seq, prev=None):
    steps_per_seq = (NPC // cb) if n_seq == 1 else 1
    blk0 = chunk0 // cb
    n_alias = 0 if prev is None else 1
    rows = lambda r, width: pl.BlockSpec((cb * r, width), lambda i: (blk0 + i, 0))
    state_spec = pl.BlockSpec((1, DN_V_HEADS, DN_DK, DN_DV), lambda i: (i // steps_per_seq, 0, 0, 0))
    in_specs = [rows(CHUNK, DN_VAL_DIM), rows(CHUNK, DN_VAL_DIM), rows(CHUNK, DN_VAL_DIM),
                rows(DN_DK, DN_V_HEADS * CHUNK), rows(CHUNK, DN_V_HEADS * CHUNK), rows(DN_V_HEADS, LANES),
                rows(CHUNK, DN_VAL_DIM), _full(nw.shape), state_spec]
    in_specs += [pl.BlockSpec(memory_space=pl.ANY)] * n_alias
    args = (u, w, qd, kdt, qkd, egl, z, nw, s_init) + ((prev,) if prev is not None else ())
    return pl.pallas_call(
        functools.partial(_gdn_rec_kernel, cb=cb, steps_per_seq=steps_per_seq, n_alias=n_alias),
        grid=(steps_per_seq * n_seq,),
        in_specs=in_specs,
        out_specs=[rows(CHUNK, DN_VAL_DIM), state_spec],
        out_shape=[jax.ShapeDtypeStruct((T, DN_VAL_DIM), BF16), jax.ShapeDtypeStruct(s_init.shape, F32)],
        input_output_aliases={9: 0} if n_alias else {},
        compiler_params=_params(),
        name="gdn_rec_prompt" if n_seq == 1 else "gdn_rec_sample",
    )(*args)


def _lane_row(vec):
    return jnp.pad(vec.astype(F32), (0, LANES - vec.shape[0])).reshape(1, LANES)


def _kvx_columns(k, v):
    parts = [t[..., h, :] for t in (k, v) for h in range(N_KV_HEADS) for _ in range(2)]
    return jnp.concatenate(parts, axis=-1)


def kernel(x_prompt, x_sample, cache_k, cache_v, state_conv, state_ssm, p_prompt, p_sample, w_qkv, b_qkv, attn_sinks, w_attn_out, w_dn_in, dn_conv_w, dn_a_log, dn_dt_bias, dn_norm_w, w_dn_out, w_ffn_gu, w_ffn_down, w_ple_gate, w_ple_proj, ln_g, ln_b):
    x = jnp.concatenate([x_prompt.reshape(TP, D_MODEL), x_sample.reshape(TS, D_MODEL)], axis=0)
    p_all = jnp.concatenate([p_prompt.reshape(DEPTH, TP, PLE_DIM), p_sample.reshape(DEPTH, TS, PLE_DIM)], axis=1)
    row = lambda v: v.reshape(1, -1)

    def channel(xin, i):
        return _ffn(xin, p_all[i], w_ffn_gu[i].astype(BF16), w_ffn_down[i].astype(BF16),
                    w_ple_gate[i].astype(BF16), w_ple_proj[i].astype(BF16), row(ln_g[i, 1]), row(ln_b[i, 1]))

    kv_heads = (N_KV_HEADS, HEAD_DIM)
    w_q, w_k, w_v = jnp.split(w_qkv[0], [Q_DIM, Q_DIM + N_KV_HEADS * HEAD_DIM], axis=1)
    b_q, b_k, b_v = jnp.split(b_qkv[0], [Q_DIM, Q_DIM + N_KV_HEADS * HEAD_DIM])
    w_x = jnp.concatenate([w_q, _kvx_columns(w_k.reshape((D_MODEL,) + kv_heads),
                                             w_v.reshape((D_MODEL,) + kv_heads))], axis=1).astype(BF16)
    b_x = jnp.concatenate([b_q, _kvx_columns(b_k.reshape(kv_heads), b_v.reshape(kv_heads))])
    q, kv = _qkv_proj(x, w_x, row(b_x))
    kv_new_s = kv[TP:].reshape(DEC_BATCH, DEC_SEQ, KVX_DIM)
    kvx = jnp.concatenate(
        [kv[:TP], jnp.concatenate([_kvx_columns(cache_k[0], cache_v[0]), kv_new_s], axis=1).reshape(-1, KVX_DIM)],
        axis=0)
    o = _attention(attn_sinks[0], q, kvx)
    x = _proj_ln(o, w_attn_out[0].astype(BF16), x, row(ln_g[0, 0]), row(ln_b[0, 0]))
    x = channel(x, 0)

    def heads_of(rows_, first_tile):
        cols = [rows_[..., (first_tile + h) * LANES:(first_tile + h) * LANES + HEAD_DIM] for h in range(N_KV_HEADS)]
        return jnp.stack(cols, axis=-2)

    new_k_prompt = heads_of(kv[TP - WINDOW:TP], 0)[None, None]
    new_v_prompt = heads_of(kv[TP - WINDOW:TP], N_KV_HEADS)[None, None]
    new_k_sample = jnp.concatenate([cache_k[0][:, DEC_SEQ:], heads_of(kv_new_s, 0)], axis=1)[None]
    new_v_sample = jnp.concatenate([cache_v[0][:, DEC_SEQ:], heads_of(kv_new_s, N_KV_HEADS)], axis=1)[None]

    w_in = w_dn_in[0]
    wmix = w_in[:, :CONV_DIM].astype(BF16)
    wz = w_in[:, CONV_DIM:CONV_DIM + DN_VAL_DIM].astype(BF16)
    wab = jnp.pad(w_in[:, CONV_DIM + DN_VAL_DIM:], ((0, 0), (0, LANES - 2 * DN_V_HEADS))).astype(BF16)
    alog = _lane_row(dn_a_log[0])
    dtb = _lane_row(dn_dt_bias[0])
    hist_pad = ((0, 0), (HIST - (CONV_WIDTH - 1), 0), (0, 0))
    st_prompt = jnp.zeros((1, HIST, CONV_DIM), F32)
    st_sample = jnp.pad(state_conv[0], hist_pad)
    dn_w = (wmix, wz, wab, dn_conv_w[0], alog, dtb)
    *tok_p, cst_p = _dn_in(x, st_prompt, *dn_w, tm=TM, row_block0=0, n_seq=1)
    *tok, cst_s = _dn_in(x, st_sample, *dn_w, tm=DEC_SEQ, row_block0=NPC, n_seq=DEC_BATCH, prev=tok_p)
    qn, kn, v, z, gates = tok
    wy = _gdn_wy(qn, kn, v, gates)
    nw = row(dn_norm_w[0])
    s0_prompt = jnp.zeros((1,) + state_ssm.shape[2:], F32)
    o_p, s_p = _gdn_rec(*wy, z, nw, s0_prompt, cb=REC_CHUNKS, chunk0=0, n_seq=1)
    o, s_s = _gdn_rec(*wy, z, nw, state_ssm[0], cb=1, chunk0=NPC, n_seq=DEC_BATCH, prev=o_p)
    x = _proj_ln(o, w_dn_out[0].astype(BF16), x, row(ln_g[1, 0]), row(ln_b[1, 0]))
    x = channel(x, 1)

    tail = HIST - (CONV_WIDTH - 1)
    return (x[:TP].reshape(1, TP, D_MODEL), x[TP:].reshape(DEC_BATCH, DEC_SEQ, D_MODEL),
            new_k_prompt, new_v_prompt, new_k_sample, new_v_sample,
            cst_p[:, tail:][None], s_p[None], cst_s[:, tail:][None], s_s[None])
```

```python
import functools

import jax
import jax.numpy as jnp
from jax import lax
from jax.experimental import pallas as pl
from jax.experimental.pallas import tpu as pltpu

F32 = jnp.float32
BF16 = jnp.bfloat16

D_MODEL = 1024
SEQ = 16384
DEC_BATCH = 8
DEC_SEQ = 64
TP = SEQ
TS = DEC_BATCH * DEC_SEQ
T = TP + TS
CHUNK = 64
NPC = TP // CHUNK
NCH = T // CHUNK

N_HEADS = 16
N_KV_HEADS = 2
HEAD_DIM = 64
WINDOW = 128
Q_DIM = N_HEADS * HEAD_DIM
KVX_DIM = 2 * N_KV_HEADS * 2 * HEAD_DIM

DN_QK_HEADS = 8
DN_V_HEADS = 16
DN_DK = 128
DN_DV = 128
DN_KEY_DIM = DN_QK_HEADS * DN_DK
DN_VAL_DIM = DN_V_HEADS * DN_DV
CONV_DIM = 2 * DN_KEY_DIM + DN_VAL_DIM
CONV_WIDTH = 4

D_FF = 2816
PLE_DIM = 256
DEPTH = 2
ALPHA = (2 * DEPTH) ** 0.25
LN_EPS = 1e-5
RMS_EPS = 1e-6
NEG_INF = -1e30

LANES = 128
SUBLANES = 8
MXU_DIM = 256
TM = 512
WY_CHUNKS = 4
REC_CHUNKS = 4
VMEM_LIMIT = 56 * 1024 * 1024


def _params(vmem=VMEM_LIMIT):
    return pltpu.CompilerParams(dimension_semantics=("arbitrary",), vmem_limit_bytes=vmem)


def _full(shape):
    nd = len(shape)
    return pl.BlockSpec(shape, lambda i: (0,) * nd, pipeline_mode=pl.Buffered(1))


def _rows(tm, width):
    return pl.BlockSpec((tm, width), lambda i: (i, 0))


def _sigmoid(x):
    return 1.0 / (1.0 + jnp.exp(-x))


def _layer_norm(y, g, b):
    mu = jnp.mean(y, axis=-1, keepdims=True)
    d = y - mu
    var = jnp.mean(d * d, axis=-1, keepdims=True)
    return d * lax.rsqrt(var + LN_EPS) * g + b


def _iota(shape, axis):
    return lax.broadcasted_iota(jnp.int32, shape, axis)


def _main_tail(block, n_main, tail_block):
    main = pl.BlockSpec(block, lambda i: (jnp.minimum(i, n_main - 1), 0))
    tail = pl.BlockSpec(block, lambda i: (tail_block, 0))
    return main, tail


def _pick(main_ref, tail_ref, n_main):
    return jnp.where(pl.program_id(0) < n_main, main_ref[...], tail_ref[...])


def _qkv_kernel(xm_ref, xt_ref, w_ref, b_ref, q_ref, kv_ref, *, n_main):
    xb = _pick(xm_ref, xt_ref, n_main).astype(BF16)
    y = jnp.dot(xb, w_ref[...], preferred_element_type=F32) + b_ref[...]
    q_ref[...] = y[:, :Q_DIM].astype(BF16)
    kv_ref[...] = y[:, Q_DIM:]


def _qkv_proj(x_main, x_tail, w, b):
    n_main = TP // TM
    return pl.pallas_call(
        functools.partial(_qkv_kernel, n_main=n_main),
        grid=(T // TM,),
        in_specs=[*_main_tail((TM, D_MODEL), n_main, 0), _full(w.shape), _full(b.shape)],
        out_specs=[_rows(TM, Q_DIM), _rows(TM, KVX_DIM)],
        out_shape=[jax.ShapeDtypeStruct((T, Q_DIM), BF16), jax.ShapeDtypeStruct((T, KVX_DIM), F32)],
        compiler_params=_params(),
        name="qkv_proj",
    )(x_main, x_tail, w, b)


def _attn_kernel(sink_ref, q_ref, kv0_ref, kv1_ref, kv2_ref, c0_ref, c1_ref, o_ref):
    g = pl.program_id(0)
    band = 3 * CHUNK
    keys = 2 * LANES
    start = jnp.where(g < NPC, jnp.maximum(2 - g, 0) * CHUNK, 0)
    key = _iota((CHUNK, 2 * keys), 1) & (keys - 1)
    valid = (key >= start) & (key < band)
    is_prompt = g < NPC
    kvb = jnp.concatenate([jnp.where(is_prompt, kv0_ref[...], c0_ref[...]),
                           jnp.where(is_prompt, kv1_ref[...], c1_ref[...]), kv2_ref[...],
                           jnp.zeros((keys - band, KVX_DIM), F32)], axis=0).astype(BF16)
    lo = _iota((keys, LANES), 1) < HEAD_DIM
    zero = jnp.zeros((), BF16)

    def halves(tile):
        return jnp.concatenate([jnp.where(lo, tile, zero), jnp.where(lo, zero, tile)], axis=0)

    k_ops = [halves(kvb[:, kh * LANES:(kh + 1) * LANES]) for kh in range(N_KV_HEADS)]
    v_ops = [halves(kvb[:, (N_KV_HEADS + kh) * LANES:(N_KV_HEADS + kh + 1) * LANES])
             for kh in range(N_KV_HEADS)]
    pairs = Q_DIM // LANES
    per_kv = pairs // N_KV_HEADS
    scores = [lax.dot_general(q_ref[:, i * LANES:(i + 1) * LANES], k_ops[i // per_kv],
                              (((1,), (1,)), ((), ())), preferred_element_type=F32) for i in range(pairs)]
    probs = []
    for i in range(pairs):
        s = jnp.where(valid, scores[i] * (HEAD_DIM ** -0.5), NEG_INF)
        ps = []
        for hh in range(2):
            sh = s[:, hh * keys:(hh + 1) * keys]
            sink = sink_ref[2 * i + hh]
            m = jnp.maximum(jnp.max(sh, axis=-1, keepdims=True), sink)
            p = jnp.exp(sh - m)
            denom = jnp.sum(p, axis=-1, keepdims=True) + jnp.exp(sink - m)
            ps.append(p * (1.0 / denom))
        probs.append(jnp.concatenate(ps, axis=1).astype(BF16))
    for i in range(pairs):
        o = jnp.dot(probs[i], v_ops[i // per_kv], preferred_element_type=F32)
        o_ref[:, i * LANES:(i + 1) * LANES] = o.astype(BF16)


def _attention(sinks, q, kv, cache):
    def older(j):
        return lambda g: (jnp.where(g < NPC, jnp.maximum(g - 2 + j, 0), NPC - 1), 0)

    def cached(j):
        return lambda g: (jnp.where(g < NPC, j, 2 * (g - NPC) + j), 0)

    kv_block = lambda index_map: pl.BlockSpec((CHUNK, KVX_DIM), index_map)
    return pl.pallas_call(
        _attn_kernel,
        grid=(NCH,),
        in_specs=[pl.BlockSpec(memory_space=pltpu.SMEM), _rows(CHUNK, Q_DIM),
                  kv_block(older(0)), kv_block(older(1)), _rows(CHUNK, KVX_DIM),
                  kv_block(cached(0)), kv_block(cached(1))],
        out_specs=_rows(CHUNK, Q_DIM),
        out_shape=jax.ShapeDtypeStruct((T, Q_DIM), BF16),
        compiler_params=_params(),
        name="swa_attn",
    )(sinks, q, kv, kv, kv, cache, cache)


FF_CHUNK = 256


def _mix_ffn_kernel(a_ref, wo_ref, xm_ref, xt_ref, pm_ref, pt_ref, g0_ref, b0_ref,
                    wgu_ref, wd_ref, wg_ref, wp_ref, g1_ref, b1_ref, *rest, n_main, split_out):
    act_ref = rest[-1]
    y = jnp.dot(a_ref[...], wo_ref[...], preferred_element_type=F32)
    x = _layer_norm(ALPHA * _pick(xm_ref, xt_ref, n_main) + y, g0_ref[...], b0_ref[...])
    xb = x.astype(BF16)
    for c in range(D_FF // FF_CHUNK):
        lo = c * FF_CHUNK
        gate = jnp.dot(xb, wgu_ref[:, lo:lo + FF_CHUNK], preferred_element_type=F32)
        up = jnp.dot(xb, wgu_ref[:, D_FF + lo:D_FF + lo + FF_CHUNK], preferred_element_type=F32)
        act_ref[:, lo:lo + FF_CHUNK] = (gate * _sigmoid(gate) * up).astype(BF16)
    f = jnp.dot(act_ref[...], wd_ref[...], preferred_element_type=F32)
    pb = _pick(pm_ref, pt_ref, n_main).astype(BF16)
    e = _sigmoid(jnp.dot(xb, wg_ref[...], preferred_element_type=F32)) * jnp.dot(
        pb, wp_ref[...], preferred_element_type=F32)
    out = _layer_norm(ALPHA * x + f + e, g1_ref[...], b1_ref[...])
    if split_out:
        om_ref, ot_ref = rest[:2]
        is_main = pl.program_id(0) < n_main

        @pl.when(is_main)
        def _():
            om_ref[...] = out

        @pl.when(jnp.logical_not(is_main))
        def _():
            ot_ref[...] = out
    else:
        rest[0][...] = out


def _mix_ffn(a, wo, x_main, x_tail, x_tail_block, p_main, p_tail, layer, ln0, weights, ln1, *, split_out):
    n_main = TP // TM
    small = [ln0[0], ln0[1], *weights, ln1[0], ln1[1]]
    in_specs = [_rows(TM, a.shape[1]), _full(wo.shape),
                *_main_tail((TM, D_MODEL), n_main, x_tail_block),
                pl.BlockSpec((None, None, TM, PLE_DIM), lambda i: (layer, 0, jnp.minimum(i, n_main - 1), 0)),
                pl.BlockSpec((None, TM, PLE_DIM), lambda i: (layer, 0, 0)),
                *[_full(s.shape) for s in small]]
    if split_out:
        out_specs = list(_main_tail((TM, D_MODEL), n_main, 0))
        out_shape = [jax.ShapeDtypeStruct((TP, D_MODEL), F32), jax.ShapeDtypeStruct((TS, D_MODEL), F32)]
    else:
        out_specs = _rows(TM, D_MODEL)
        out_shape = jax.ShapeDtypeStruct((T, D_MODEL), F32)
    return pl.pallas_call(
        functools.partial(_mix_ffn_kernel, n_main=n_main, split_out=split_out),
        grid=(T // TM,),
        in_specs=in_specs,
        out_specs=out_specs,
        out_shape=out_shape,
        scratch_shapes=[pltpu.VMEM((TM, D_FF), BF16)],
        compiler_params=_params(),
        name="mix_ffn",
    )(a, wo, x_main, x_tail, p_main, p_tail, *small)


DN_COLS = 512
HIST = SUBLANES
TAPS_BACK = CONV_WIDTH - 1


def _dn_in_kernel(x_ref, st_ref, wmix_ref, wz_ref, wab_ref, cw_ref, alog_ref, dtb_ref, *rest,
                  tm, steps_per_seq, n_alias):
    qn_ref, kn_ref, v_ref, z_ref, gates_ref, cst_ref, hist_ref, tmp_ref = rest[n_alias:]
    i = pl.program_id(0)

    @pl.when(i % steps_per_seq == 0)
    def _():
        hist_ref[...] = st_ref[0]

    xb = x_ref[...].astype(BF16)
    for c in range(CONV_DIM // DN_COLS):
        lo = c * DN_COLS
        cols = slice(lo, lo + DN_COLS)
        buf = tmp_ref.at[c % 2]
        mm = jnp.dot(xb, wmix_ref[:, cols], preferred_element_type=F32)
        buf[0, HIST:HIST + tm, :] = mm
        for j in range(1, CONV_WIDTH):
            buf[j, HIST:2 * HIST, :] = hist_ref[j - 1, :, cols]
            buf[j, HIST + j:HIST + j + tm, :] = mm
            hist_ref[j - 1, :, cols] = buf[j, HIST + tm:2 * HIST + tm, :]
        cw = cw_ref[:, cols]
        conv = cw[3:4] * buf[0, HIST:HIST + tm, :]
        for j in range(1, CONV_WIDTH):
            conv = conv + cw[3 - j:4 - j] * buf[j, HIST:HIST + tm, :]
        y = conv * _sigmoid(conv)
        if lo < 2 * DN_KEY_DIM:
            is_q = lo < DN_KEY_DIM
            dst = qn_ref if is_q else kn_ref
            base = lo if is_q else lo - DN_KEY_DIM
            for t in range(DN_COLS // DN_DK):
                yt = y[:, t * DN_DK:(t + 1) * DN_DK]
                n = yt * lax.rsqrt(jnp.sum(yt * yt, axis=-1, keepdims=True) + 1e-6)
                if is_q:
                    n = n * (DN_DK ** -0.5)
                dst[:, base + t * DN_DK:base + (t + 1) * DN_DK] = n.astype(BF16)
        else:
            v_ref[:, lo - 2 * DN_KEY_DIM:lo - 2 * DN_KEY_DIM + DN_COLS] = y.astype(BF16)
    cst_ref[0] = hist_ref[TAPS_BACK - 1]

    for c in range(DN_VAL_DIM // DN_COLS):
        cols = slice(c * DN_COLS, (c + 1) * DN_COLS)
        z_ref[:, cols] = jnp.dot(xb, wz_ref[:, cols], preferred_element_type=F32).astype(BF16)

    ab = jnp.dot(xb, wab_ref[...], preferred_element_type=F32)
    sp_in = ab + dtb_ref[...]
    softplus = jnp.maximum(sp_in, 0.0) + jnp.log1p(jnp.exp(-jnp.abs(sp_in)))
    gval = -jnp.exp(alog_ref[...]) * softplus
    gates_ref[...] = jnp.where(_iota(ab.shape, 1) < DN_V_HEADS, gval, _sigmoid(ab))


def _dn_in(x, state, wmix, wz, wab, cw, alog, dtb, *, tm, row_block0, n_seq, prev=None):
    steps_per_seq = (TP // tm) if n_seq == 1 else 1
    n_steps = steps_per_seq * n_seq
    n_alias = 0 if prev is None else len(prev)
    row = lambda width: pl.BlockSpec((tm, width), lambda i: (row_block0 + i, 0))
    state_spec = pl.BlockSpec((1, TAPS_BACK, HIST, CONV_DIM), lambda i: (i // steps_per_seq, 0, 0, 0))
    cst_spec = pl.BlockSpec((1, HIST, CONV_DIM), lambda i: (i // steps_per_seq, 0, 0))
    in_specs = [row(D_MODEL), state_spec, _full(wmix.shape), _full(wz.shape), _full(wab.shape),
                _full(cw.shape), _full(alog.shape), _full(dtb.shape)]
    in_specs += [pl.BlockSpec(memory_space=pl.ANY)] * n_alias
    out_shape = [jax.ShapeDtypeStruct((T, DN_KEY_DIM), BF16), jax.ShapeDtypeStruct((T, DN_KEY_DIM), BF16),
                 jax.ShapeDtypeStruct((T, DN_VAL_DIM), BF16), jax.ShapeDtypeStruct((T, DN_VAL_DIM), BF16),
                 jax.ShapeDtypeStruct((T, LANES), F32),
                 jax.ShapeDtypeStruct((n_seq, HIST, CONV_DIM), F32)]
    out_specs = [row(DN_KEY_DIM), row(DN_KEY_DIM), row(DN_VAL_DIM), row(DN_VAL_DIM), row(LANES), cst_spec]
    args = (x, state, wmix, wz, wab, cw, alog, dtb) + (tuple(prev) if prev is not None else ())
    return pl.pallas_call(
        functools.partial(_dn_in_kernel, tm=tm, steps_per_seq=steps_per_seq, n_alias=n_alias),
        grid=(n_steps,),
        in_specs=in_specs,
        out_specs=out_specs,
        out_shape=out_shape,
        scratch_shapes=[pltpu.VMEM((TAPS_BACK, HIST, CONV_DIM), F32),
                        pltpu.VMEM((2, CONV_WIDTH, tm + 2 * HIST, DN_COLS), F32)],
        input_output_aliases={8 + k: k for k in range(n_alias)},
        compiler_params=_params(),
        name="dn_in_prompt" if n_seq == 1 else "dn_in_sample",
    )(*args)


GROUP_HEADS = MXU_DIM // CHUNK


def _split3(x):
    hi = x.astype(BF16)
    r = x - hi.astype(F32)
    mid = r.astype(BF16)
    lo = (r - mid.astype(F32)).astype(BF16)
    return hi, mid, lo


def _unit_lower_inverses(lcats, eye_cat, bd01):
    c = lcats[0].shape[0]

    def mm(a, b):
        b_bd = jnp.concatenate([b.astype(BF16)] * GROUP_HEADS, axis=0) * bd01
        return jnp.dot(a.astype(BF16), b_bd, preferred_element_type=F32)

    ms = [mm(l, l) for l in lcats]
    ps = [eye_cat - l for l in lcats]
    power = 2
    while 2 * power < c:
        pms = [mm(jnp.concatenate([p, m], axis=0), m) for p, m in zip(ps, ms)]
        ps = [p + pm[:c] for p, pm in zip(ps, pms)]
        ms = [pm[c:] for pm in pms]
        power *= 2
    return [p + mm(p, m) for p, m in zip(ps, ms)]


def _gdn_wy_kernel(q_ref, k_ref, v_ref, gates_ref, u_ref, w_ref, qd_ref, kdt_ref, qkd_ref, egl_ref, *, cb):
    c = CHUNK
    grp = GROUP_HEADS * c
    row = _iota((c, LANES), 0)
    lane = _iota((c, LANES), 1)
    col = lane & (c - 1)
    lo = lane < c
    causal2 = row >= col
    strict2 = row > col
    tril = jnp.where(_iota((c, c), 0) >= _iota((c, c), 1), 1.0, 0.0).astype(BF16)
    eye_cat = jnp.where(_iota((c, grp), 0) == (_iota((c, grp), 1) & (c - 1)), 1.0, 0.0).astype(F32)
    shift = c.bit_length() - 1
    bd01 = jnp.where((_iota((grp, grp), 0) >> shift) == (_iota((grp, grp), 1) >> shift), 1.0, 0.0).astype(BF16)
    nt = (((1,), (1,)), ((), ()))

    lcats, rhs = [], []
    for cc in range(cb):
        rows = slice(cc * c, (cc + 1) * c)
        gates = gates_ref[rows, :]
        cum = sum(jnp.dot(tril, part, preferred_element_type=F32) for part in _split3(gates))
        cum_t = jnp.concatenate([cum, cum], axis=0).T
        egl_ref[cc * DN_V_HEADS:(cc + 1) * DN_V_HEADS, :] = jnp.broadcast_to(
            jnp.exp(cum_t[0:DN_V_HEADS, c - 1:c]), (DN_V_HEADS, LANES))
        lmats = []
        for j in range(DN_QK_HEADS):
            a, b = 2 * j, 2 * j + 1
            tile = slice(j * LANES, (j + 1) * LANES)
            kb = k_ref[rows, tile]
            qb = q_ref[rows, tile]
            kq = lax.dot_general(jnp.concatenate([kb, qb], axis=0), jnp.concatenate([kb, kb], axis=0), nt,
                                 preferred_element_type=F32)
            gbc = jnp.where(lo, cum[:, a:a + 1], cum[:, b:b + 1])
            gbr = jnp.where(lo, cum_t[a:a + 1, :], cum_t[b:b + 1, :])
            beta2 = jnp.where(lo, gates[:, DN_V_HEADS + a:DN_V_HEADS + a + 1],
                              gates[:, DN_V_HEADS + b:DN_V_HEADS + b + 1])
            decay = jnp.where(causal2, jnp.exp(jnp.where(causal2, gbc - gbr, 0.0)), 0.0)
            lmats.append(jnp.where(strict2, kq[:c] * beta2 * decay, 0.0))
            qkd_ref[rows, tile] = (kq[c:] * decay).astype(BF16)
            kf = kb.astype(F32)
            qf = qb.astype(F32)
            kdec = []
            for h in (a, b):
                head = slice(h * DN_DV, (h + 1) * DN_DV)
                gb = cum[:, h:h + 1]
                beta = gates[:, DN_V_HEADS + h:DN_V_HEADS + h + 1]
                egb = jnp.exp(gb)
                vf = v_ref[rows, head].astype(F32)
                rhs.append(jnp.concatenate([vf * beta, kf * (beta * egb)], axis=1).astype(BF16))
                qd_ref[rows, head] = (qf * egb).astype(BF16)
                kdec.append(kf * jnp.exp(cum[c - 1:c, h:h + 1] - gb))
            kdt_ref[cc * DN_DK:(cc + 1) * DN_DK, tile] = jnp.concatenate(kdec, axis=0).T.astype(BF16)
        lcats += [jnp.concatenate(lmats[2 * m:2 * m + 2], axis=1) for m in range(DN_V_HEADS // GROUP_HEADS)]

    tinvs = _unit_lower_inverses(lcats, eye_cat, bd01)
    zeros = jnp.zeros((c, 2 * DN_DV), BF16)
    for cc in range(cb):
        rows = slice(cc * c, (cc + 1) * c)
        for j in range(DN_QK_HEADS):
            a = 2 * j
            h0 = cc * DN_V_HEADS + a
            pair = slice(a * DN_DV, (a + 2) * DN_DV)
            tinv = tinvs[cc * (DN_V_HEADS // GROUP_HEADS) + j // 2][:, (j % 2) * LANES:(j % 2 + 1) * LANES]
            rhs_bd = jnp.concatenate([jnp.concatenate([rhs[h0], zeros], axis=1),
                                      jnp.concatenate([zeros, rhs[h0 + 1]], axis=1)], axis=0)
            sol = jnp.dot(tinv.astype(BF16), rhs_bd, preferred_element_type=F32)
            u_ref[rows, pair] = jnp.concatenate(
                [sol[:, 0:DN_DV], sol[:, 2 * DN_DV:3 * DN_DV]], axis=1).astype(BF16)
            w_ref[rows, pair] = jnp.concatenate(
                [sol[:, DN_DV:2 * DN_DV], sol[:, 3 * DN_DV:4 * DN_DV]], axis=1).astype(BF16)


def _gdn_wy(qn, kn, v, gates):
    cb = WY_CHUNKS
    tm = cb * CHUNK
    tok = lambda width: jax.ShapeDtypeStruct((T, width), BF16)
    return pl.pallas_call(
        functools.partial(_gdn_wy_kernel, cb=cb),
        grid=(NCH // cb,),
        in_specs=[_rows(tm, DN_KEY_DIM), _rows(tm, DN_KEY_DIM), _rows(tm, DN_VAL_DIM), _rows(tm, LANES)],
        out_specs=[_rows(tm, DN_VAL_DIM), _rows(tm, DN_VAL_DIM), _rows(tm, DN_VAL_DIM),
                   _rows(cb * DN_DK, DN_V_HEADS * CHUNK), _rows(tm, DN_V_HEADS * CHUNK),
                   _rows(cb * DN_V_HEADS, LANES)],
        out_shape=[tok(DN_VAL_DIM), tok(DN_VAL_DIM), tok(DN_VAL_DIM),
                   jax.ShapeDtypeStruct((NCH * DN_DK, DN_V_HEADS * CHUNK), BF16), tok(DN_V_HEADS * CHUNK),
                   jax.ShapeDtypeStruct((NCH * DN_V_HEADS, LANES), F32)],
        compiler_params=_params(),
        name="gdn_wy",
    )(qn, kn, v, gates)


def _gdn_rec_kernel(u_ref, w_ref, qd_ref, kdt_ref, qkd_ref, egl_ref, z_ref, nw_ref, sinit_ref, *rest,
                    cb, steps_per_seq, n_alias):
    o_ref, s_ref = rest[n_alias:]
    c = CHUNK

    @pl.when(pl.program_id(0) % steps_per_seq == 0)
    def _():
        s_ref[...] = sinit_ref[...]

    nw = nw_ref[...]
    zero_s = jnp.zeros((DN_DK, DN_DV), BF16)
    zero_v = jnp.zeros((c, DN_DV), BF16)
    for cc in range(cb):
        rows = slice(cc * c, (cc + 1) * c)
        egl = egl_ref[cc * DN_V_HEADS:(cc + 1) * DN_V_HEADS, :]
        v_new, o_inter = [], []
        for m in range(DN_V_HEADS // 2):
            a = 2 * m
            pair = slice(a * DN_DV, (a + 2) * DN_DV)
            s_bd = jnp.concatenate(
                [jnp.concatenate([s_ref[0, a].astype(BF16), zero_s], axis=1),
                 jnp.concatenate([zero_s, s_ref[0, a + 1].astype(BF16)], axis=1)], axis=0)
            r = jnp.dot(jnp.concatenate([w_ref[rows, pair], qd_ref[rows, pair]], axis=0), s_bd,
                        preferred_element_type=F32)
            vn = u_ref[rows, pair].astype(F32) - r[:c]
            v_new += [vn[:, :DN_DV].astype(BF16), vn[:, DN_DV:].astype(BF16)]
            o_inter += [r[c:, :DN_DV], r[c:, DN_DV:]]
        for n in range(DN_V_HEADS // GROUP_HEADS):
            heads = range(GROUP_HEADS * n, GROUP_HEADS * (n + 1))
            v_bd = jnp.concatenate(
                [jnp.concatenate([v_new[h] if t == i else zero_v for t in range(GROUP_HEADS)], axis=1)
                 for i, h in enumerate(heads)], axis=0)
            grp = slice(n * GROUP_HEADS * c, (n + 1) * GROUP_HEADS * c)
            r2 = jnp.dot(jnp.concatenate([qkd_ref[rows, grp], kdt_ref[cc * DN_DK:(cc + 1) * DN_DK, grp]], axis=0),
                         v_bd, preferred_element_type=F32)
            for i, h in enumerate(heads):
                head = slice(h * DN_DV, (h + 1) * DN_DV)
                blk = slice(i * DN_DV, (i + 1) * DN_DV)
                s_ref[0, h] = s_ref[0, h] * egl[h:h + 1, :] + r2[c:, blk]
                o = o_inter[h] + r2[:c, blk]
                o = o * lax.rsqrt(jnp.mean(o * o, axis=-1, keepdims=True) + RMS_EPS) * nw
                zf = z_ref[rows, head].astype(F32)
                o_ref[rows, head] = (o * (zf * _sigmoid(zf))).astype(BF16)


def _gdn_rec(u, w, qd, kdt, qkd, egl, z, nw, s_init, *, cb, chunk0, n_seq, prev=None):
    steps_per_seq = (NPC // cb) if n_seq == 1 else 1
    blk0 = chunk0 // cb
    n_alias = 0 if prev is None else 1
    rows = lambda r, width: pl.BlockSpec((cb * r, width), lambda i: (blk0 + i, 0))
    state_spec = pl.BlockSpec((1, DN_V_HEADS, DN_DK, DN_DV), lambda i: (i // steps_per_seq, 0, 0, 0))
    in_specs = [rows(CHUNK, DN_VAL_DIM), rows(CHUNK, DN_VAL_DIM), rows(CHUNK, DN_VAL_DIM),
                rows(DN_DK, DN_V_HEADS * CHUNK), rows(CHUNK, DN_V_HEADS * CHUNK), rows(DN_V_HEADS, LANES),
                rows(CHUNK, DN_VAL_DIM), _full(nw.shape), state_spec]
    in_specs += [pl.BlockSpec(memory_space=pl.ANY)] * n_alias
    args = (u, w, qd, kdt, qkd, egl, z, nw, s_init) + ((prev,) if prev is not None else ())
    return pl.pallas_call(
        functools.partial(_gdn_rec_kernel, cb=cb, steps_per_seq=steps_per_seq, n_alias=n_alias),
        grid=(steps_per_seq * n_seq,),
        in_specs=in_specs,
        out_specs=[rows(CHUNK, DN_VAL_DIM), state_spec],
        out_shape=[jax.ShapeDtypeStruct((T, DN_VAL_DIM), BF16), jax.ShapeDtypeStruct(s_init.shape, F32)],
        input_output_aliases={9: 0} if n_alias else {},
        compiler_params=_params(),
        name="gdn_rec_prompt" if n_seq == 1 else "gdn_rec_sample",
    )(*args)


def _lane_row(vec):
    return jnp.pad(vec.astype(F32), (0, LANES - vec.shape[0])).reshape(1, LANES)


def _history_tiles(state):
    tiles = [jnp.pad(state[:, TAPS_BACK - j:], ((0, 0), (0, HIST - j), (0, 0))) for j in range(1, CONV_WIDTH)]
    return jnp.stack(tiles, axis=1)


def _kvx_columns(k, v):
    parts = [t[..., h, :] for t in (k, v) for h in range(N_KV_HEADS) for _ in range(2)]
    return jnp.concatenate(parts, axis=-1)


def kernel(x_prompt, x_sample, cache_k, cache_v, state_conv, state_ssm, p_prompt, p_sample, w_qkv, b_qkv, attn_sinks, w_attn_out, w_dn_in, dn_conv_w, dn_a_log, dn_dt_bias, dn_norm_w, w_dn_out, w_ffn_gu, w_ffn_down, w_ple_gate, w_ple_proj, ln_g, ln_b):
    assert TS == TM, "the sample tokens must form exactly one row block"
    xp = x_prompt.reshape(TP, D_MODEL)
    xs = x_sample.reshape(TS, D_MODEL)
    ps = p_sample.reshape(DEPTH, TS, PLE_DIM)
    row = lambda v: v.reshape(1, -1)

    def channel_weights(i):
        return (w_ffn_gu[i].astype(BF16), w_ffn_down[i].astype(BF16),
                w_ple_gate[i].astype(BF16), w_ple_proj[i].astype(BF16))

    def ln(i, k):
        return row(ln_g[i, k]), row(ln_b[i, k])

    kv_heads = (N_KV_HEADS, HEAD_DIM)
    w_q, w_k, w_v = jnp.split(w_qkv[0], [Q_DIM, Q_DIM + N_KV_HEADS * HEAD_DIM], axis=1)
    b_q, b_k, b_v = jnp.split(b_qkv[0], [Q_DIM, Q_DIM + N_KV_HEADS * HEAD_DIM])
    w_x = jnp.concatenate([w_q, _kvx_columns(w_k.reshape((D_MODEL,) + kv_heads),
                                             w_v.reshape((D_MODEL,) + kv_heads))], axis=1).astype(BF16)
    b_x = jnp.concatenate([b_q, _kvx_columns(b_k.reshape(kv_heads), b_v.reshape(kv_heads))])
    q, kv = _qkv_proj(xp, xs, w_x, row(b_x))
    cache = _kvx_columns(cache_k[0], cache_v[0]).reshape(DEC_BATCH * WINDOW, KVX_DIM)
    o = _attention(attn_sinks[0], q, kv, cache)
    x = _mix_ffn(o, w_attn_out[0].astype(BF16), xp, xs, 0, p_prompt, ps, 0, ln(0, 0), channel_weights(0),
                 ln(0, 1), split_out=False)

    def heads_of(rows_, first_tile):
        cols = [rows_[..., (first_tile + h) * LANES:(first_tile + h) * LANES + HEAD_DIM] for h in range(N_KV_HEADS)]
        return jnp.stack(cols, axis=-2)

    kv_new_s = kv[TP:].reshape(DEC_BATCH, DEC_SEQ, KVX_DIM)
    new_k_prompt = heads_of(kv[TP - WINDOW:TP], 0)[None, None]
    new_v_prompt = heads_of(kv[TP - WINDOW:TP], N_KV_HEADS)[None, None]
    new_k_sample = jnp.concatenate([cache_k[0][:, DEC_SEQ:], heads_of(kv_new_s, 0)], axis=1)[None]
    new_v_sample = jnp.concatenate([cache_v[0][:, DEC_SEQ:], heads_of(kv_new_s, N_KV_HEADS)], axis=1)[None]

    w_in = w_dn_in[0]
    wmix = w_in[:, :CONV_DIM].astype(BF16)
    wz = w_in[:, CONV_DIM:CONV_DIM + DN_VAL_DIM].astype(BF16)
    wab = jnp.pad(w_in[:, CONV_DIM + DN_VAL_DIM:], ((0, 0), (0, LANES - 2 * DN_V_HEADS))).astype(BF16)
    alog = _lane_row(dn_a_log[0])
    dtb = _lane_row(dn_dt_bias[0])
    st_prompt = jnp.zeros((1, TAPS_BACK, HIST, CONV_DIM), F32)
    st_sample = _history_tiles(state_conv[0])
    dn_w = (wmix, wz, wab, dn_conv_w[0], alog, dtb)
    *tok_p, cst_p = _dn_in(x, st_prompt, *dn_w, tm=TM, row_block0=0, n_seq=1)
    *tok, cst_s = _dn_in(x, st_sample, *dn_w, tm=DEC_SEQ, row_block0=NPC, n_seq=DEC_BATCH, prev=tok_p)
    qn, kn, v, z, gates = tok
    wy = _gdn_wy(qn, kn, v, gates)
    nw = row(dn_norm_w[0])
    s0_prompt = jnp.zeros((1,) + state_ssm.shape[2:], F32)
    o_p, s_p = _gdn_rec(*wy, z, nw, s0_prompt, cb=REC_CHUNKS, chunk0=0, n_seq=1)
    o, s_s = _gdn_rec(*wy, z, nw, state_ssm[0], cb=1, chunk0=NPC, n_seq=DEC_BATCH, prev=o_p)
    y_prompt, y_sample = _mix_ffn(o, w_dn_out[0].astype(BF16), x, x, TP // TM, p_prompt, ps, 1, ln(1, 0),
                                  channel_weights(1), ln(1, 1), split_out=True)

    return (y_prompt.reshape(1, TP, D_MODEL), y_sample.reshape(DEC_BATCH, DEC_SEQ, D_MODEL),
            new_k_prompt, new_v_prompt, new_k_sample, new_v_sample,
            cst_p[:, :TAPS_BACK][None], s_p[None], cst_s[:, :TAPS_BACK][None], s_s[None])
```

```python
import functools

import jax
import jax.numpy as jnp
from jax import lax
from jax.experimental import pallas as pl
from jax.experimental.pallas import tpu as pltpu

F32 = jnp.float32
BF16 = jnp.bfloat16

D_MODEL = 1024
SEQ = 16384
DEC_BATCH = 8
DEC_SEQ = 64
TP = SEQ
TS = DEC_BATCH * DEC_SEQ
T = TP + TS
CHUNK = 64
NPC = TP // CHUNK
NCH = T // CHUNK

N_HEADS = 16
N_KV_HEADS = 2
HEAD_DIM = 64
WINDOW = 128
Q_DIM = N_HEADS * HEAD_DIM
KVX_DIM = 2 * N_KV_HEADS * 2 * HEAD_DIM

DN_QK_HEADS = 8
DN_V_HEADS = 16
DN_DK = 128
DN_DV = 128
DN_KEY_DIM = DN_QK_HEADS * DN_DK
DN_VAL_DIM = DN_V_HEADS * DN_DV
CONV_DIM = 2 * DN_KEY_DIM + DN_VAL_DIM
CONV_WIDTH = 4

D_FF = 2816
PLE_DIM = 256
DEPTH = 2
ALPHA = (2 * DEPTH) ** 0.25
LN_EPS = 1e-5
RMS_EPS = 1e-6
NEG_INF = -1e30

LANES = 128
SUBLANES = 8
MXU_DIM = 256
TM = 512
WY_CHUNKS = 4
VMEM_LIMIT = 56 * 1024 * 1024


def _params(vmem=VMEM_LIMIT):
    return pltpu.CompilerParams(dimension_semantics=("arbitrary",), vmem_limit_bytes=vmem)


def _full(shape):
    nd = len(shape)
    return pl.BlockSpec(shape, lambda i: (0,) * nd, pipeline_mode=pl.Buffered(1))


def _rows(tm, width):
    return pl.BlockSpec((tm, width), lambda i: (i, 0))


def _sigmoid(x):
    return 1.0 / (1.0 + jnp.exp(-x))


def _layer_norm(y, g, b):
    mu = jnp.mean(y, axis=-1, keepdims=True)
    d = y - mu
    var = jnp.mean(d * d, axis=-1, keepdims=True)
    return d * lax.rsqrt(var + LN_EPS) * g + b


def _iota(shape, axis):
    return lax.broadcasted_iota(jnp.int32, shape, axis)


def _main_tail(block, n_main, tail_block):
    main = pl.BlockSpec(block, lambda i: (jnp.minimum(i, n_main - 1), 0))
    tail = pl.BlockSpec(block, lambda i: (tail_block, 0))
    return main, tail


def _pick(main_ref, tail_ref, n_main):
    return jnp.where(pl.program_id(0) < n_main, main_ref[...], tail_ref[...])


def _qkv_kernel(xm_ref, xt_ref, w_ref, b_ref, q_ref, kv_ref, *, n_main):
    xb = _pick(xm_ref, xt_ref, n_main).astype(BF16)
    y = jnp.dot(xb, w_ref[...], preferred_element_type=F32) + b_ref[...]
    q_ref[...] = y[:, :Q_DIM].astype(BF16)
    kv_ref[...] = y[:, Q_DIM:]


def _qkv_proj(x_main, x_tail, w, b):
    n_main = TP // TM
    return pl.pallas_call(
        functools.partial(_qkv_kernel, n_main=n_main),
        grid=(T // TM,),
        in_specs=[*_main_tail((TM, D_MODEL), n_main, 0), _full(w.shape), _full(b.shape)],
        out_specs=[_rows(TM, Q_DIM), _rows(TM, KVX_DIM)],
        out_shape=[jax.ShapeDtypeStruct((T, Q_DIM), BF16), jax.ShapeDtypeStruct((T, KVX_DIM), F32)],
        compiler_params=_params(),
        name="qkv_proj",
    )(x_main, x_tail, w, b)


def _attn_kernel(sink_ref, q_ref, kv0_ref, kv1_ref, kv2_ref, c0_ref, c1_ref, o_ref):
    g = pl.program_id(0)
    band = 3 * CHUNK
    keys = 2 * LANES
    start = jnp.where(g < NPC, jnp.maximum(2 - g, 0) * CHUNK, 0)
    pairs = Q_DIM // LANES
    per_kv = pairs // N_KV_HEADS
    key = _iota((CHUNK, 2 * keys), 1) & (keys - 1)
    valid = (key >= start) & (key < band)
    is_prompt = g < NPC
    kvb = jnp.concatenate([jnp.where(is_prompt, kv0_ref[...], c0_ref[...]),
                           jnp.where(is_prompt, kv1_ref[...], c1_ref[...]), kv2_ref[...],
                           jnp.zeros((keys - band, KVX_DIM), F32)], axis=0).astype(BF16)
    lo = _iota((keys, LANES), 1) < HEAD_DIM
    zero = jnp.zeros((), BF16)

    def halves(tile):
        return jnp.concatenate([jnp.where(lo, tile, zero), jnp.where(lo, zero, tile)], axis=0)

    k_ops = [halves(kvb[:, kh * LANES:(kh + 1) * LANES]) for kh in range(N_KV_HEADS)]
    v_ops = [halves(kvb[:, (N_KV_HEADS + kh) * LANES:(N_KV_HEADS + kh + 1) * LANES])
             for kh in range(N_KV_HEADS)]
    scores = [lax.dot_general(q_ref[:, i * LANES:(i + 1) * LANES], k_ops[i // per_kv],
                              (((1,), (1,)), ((), ())), preferred_element_type=F32) for i in range(pairs)]
    probs = []
    for i in range(pairs):
        s = jnp.where(valid, scores[i] * (HEAD_DIM ** -0.5), NEG_INF)
        ps = []
        for hh in range(2):
            sh = s[:, hh * keys:(hh + 1) * keys]
            sink = sink_ref[2 * i + hh]
            m = jnp.maximum(jnp.max(sh, axis=-1, keepdims=True), sink)
            p = jnp.exp(sh - m)
            denom = jnp.sum(p, axis=-1, keepdims=True) + jnp.exp(sink - m)
            ps.append(p * (1.0 / denom))
        probs.append(jnp.concatenate(ps, axis=1).astype(BF16))
    for i in range(pairs):
        o = jnp.dot(probs[i], v_ops[i // per_kv], preferred_element_type=F32)
        o_ref[:, i * LANES:(i + 1) * LANES] = o.astype(BF16)


def _attention(sinks, q, kv, cache):
    def older(j):
        return lambda g: (jnp.where(g < NPC, jnp.maximum(g - 2 + j, 0), NPC - 1), 0)

    def cached(j):
        return lambda g: (jnp.where(g < NPC, j, 2 * (g - NPC) + j), 0)

    kv_block = lambda index_map: pl.BlockSpec((CHUNK, KVX_DIM), index_map)
    return pl.pallas_call(
        _attn_kernel,
        grid=(NCH,),
        in_specs=[pl.BlockSpec(memory_space=pltpu.SMEM), _rows(CHUNK, Q_DIM),
                  kv_block(older(0)), kv_block(older(1)), _rows(CHUNK, KVX_DIM),
                  kv_block(cached(0)), kv_block(cached(1))],
        out_specs=_rows(CHUNK, Q_DIM),
        out_shape=jax.ShapeDtypeStruct((T, Q_DIM), BF16),
        compiler_params=_params(),
        name="swa_attn",
    )(sinks, q, kv, kv, kv, cache, cache)


FF_CHUNK = 256
FF_ROWS = 256


def _mix_ffn_kernel(a_ref, wo_ref, xm_ref, xt_ref, pm_ref, pt_ref, g0_ref, b0_ref,
                    wgu_ref, wd_ref, wg_ref, wp_ref, g1_ref, b1_ref, *rest, n_main, split_out):
    act_ref = rest[-1]
    is_main = pl.program_id(0) < n_main
    outs = []
    for r0 in range(0, TM, FF_ROWS):
        rows = slice(r0, r0 + FF_ROWS)
        y = jnp.dot(a_ref[rows, :], wo_ref[...], preferred_element_type=F32)
        x_res = jnp.where(is_main, xm_ref[rows, :], xt_ref[rows, :])
        x = _layer_norm(ALPHA * x_res + y, g0_ref[...], b0_ref[...])
        xb = x.astype(BF16)
        for c in range(D_FF // FF_CHUNK):
            lo = c * FF_CHUNK
            gate = jnp.dot(xb, wgu_ref[:, lo:lo + FF_CHUNK], preferred_element_type=F32)
            up = jnp.dot(xb, wgu_ref[:, D_FF + lo:D_FF + lo + FF_CHUNK], preferred_element_type=F32)
            act_ref[rows, lo:lo + FF_CHUNK] = (gate * _sigmoid(gate) * up).astype(BF16)
        f = jnp.dot(act_ref[rows, :], wd_ref[...], preferred_element_type=F32)
        pb = jnp.where(is_main, pm_ref[rows, :], pt_ref[rows, :]).astype(BF16)
        e = _sigmoid(jnp.dot(xb, wg_ref[...], preferred_element_type=F32)) * jnp.dot(
            pb, wp_ref[...], preferred_element_type=F32)
        out = _layer_norm(ALPHA * x + f + e, g1_ref[...], b1_ref[...])
        if split_out:
            outs.append(out)
        else:
            rest[0][rows, :] = out
    if split_out:
        om_ref, ot_ref = rest[:2]

        @pl.when(is_main)
        def _():
            for k, out in enumerate(outs):
                om_ref[k * FF_ROWS:(k + 1) * FF_ROWS, :] = out

        @pl.when(jnp.logical_not(is_main))
        def _():
            for k, out in enumerate(outs):
                ot_ref[k * FF_ROWS:(k + 1) * FF_ROWS, :] = out


def _layer_of(stacked, layer):
    nd = stacked.ndim - 1
    return pl.BlockSpec((None,) + stacked.shape[1:], lambda i: (layer,) + (0,) * nd, pipeline_mode=pl.Buffered(1))


def _mix_ffn(a, wo, x_main, x_tail, x_tail_block, p_main, p_tail, layer, ln0, weights, ln1, *, split_out):
    n_main = TP // TM
    in_specs = [_rows(TM, a.shape[1]), _full(wo.shape),
                *_main_tail((TM, D_MODEL), n_main, x_tail_block),
                pl.BlockSpec((None, None, TM, PLE_DIM), lambda i: (layer, 0, jnp.minimum(i, n_main - 1), 0)),
                pl.BlockSpec((None, TM, PLE_DIM), lambda i: (layer, 0, 0)),
                _full(ln0[0].shape), _full(ln0[1].shape), *[_layer_of(w, layer) for w in weights],
                _full(ln1[0].shape), _full(ln1[1].shape)]
    if split_out:
        out_specs = list(_main_tail((TM, D_MODEL), n_main, 0))
        out_shape = [jax.ShapeDtypeStruct((TP, D_MODEL), F32), jax.ShapeDtypeStruct((TS, D_MODEL), F32)]
    else:
        out_specs = _rows(TM, D_MODEL)
        out_shape = jax.ShapeDtypeStruct((T, D_MODEL), F32)
    return pl.pallas_call(
        functools.partial(_mix_ffn_kernel, n_main=n_main, split_out=split_out),
        grid=(T // TM,),
        in_specs=in_specs,
        out_specs=out_specs,
        out_shape=out_shape,
        scratch_shapes=[pltpu.VMEM((TM, D_FF), BF16)],
        compiler_params=_params(),
        name="mix_ffn",
    )(a, wo, x_main, x_tail, p_main, p_tail, *ln0, *weights, *ln1)


DN_COLS = 512
HIST = SUBLANES
TAPS_BACK = CONV_WIDTH - 1


def _dn_in_kernel(x_ref, st_ref, win_ref, cw_ref, alog_ref, dtb_ref, *rest, tm, steps_per_seq, n_alias):
    qn_ref, kn_ref, v_ref, z_ref, gates_ref, cst_ref, hist_ref, tmp_ref = rest[n_alias:]
    i = pl.program_id(0)

    @pl.when(i % steps_per_seq == 0)
    def _():
        hist_ref[...] = st_ref[0]

    xb = x_ref[...].astype(BF16)
    for c in range(CONV_DIM // DN_COLS):
        lo = c * DN_COLS
        cols = slice(lo, lo + DN_COLS)
        buf = tmp_ref.at[c % 2]
        mm = jnp.dot(xb, win_ref[:, cols], preferred_element_type=F32)
        buf[0, HIST:HIST + tm, :] = mm
        for j in range(1, CONV_WIDTH):
            buf[j, HIST:2 * HIST, :] = hist_ref[j - 1, :, cols]
            buf[j, HIST + j:HIST + j + tm, :] = mm
            hist_ref[j - 1, :, cols] = buf[j, HIST + tm:2 * HIST + tm, :]
        cw = cw_ref[:, cols]
        conv = cw[3:4] * buf[0, HIST:HIST + tm, :]
        for j in range(1, CONV_WIDTH):
            conv = conv + cw[3 - j:4 - j] * buf[j, HIST:HIST + tm, :]
        y = conv * _sigmoid(conv)
        if lo < 2 * DN_KEY_DIM:
            is_q = lo < DN_KEY_DIM
            dst = qn_ref if is_q else kn_ref
            base = lo if is_q else lo - DN_KEY_DIM
            for t in range(DN_COLS // DN_DK):
                yt = y[:, t * DN_DK:(t + 1) * DN_DK]
                n = yt * lax.rsqrt(jnp.sum(yt * yt, axis=-1, keepdims=True) + 1e-6)
                if is_q:
                    n = n * (DN_DK ** -0.5)
                dst[:, base + t * DN_DK:base + (t + 1) * DN_DK] = n.astype(BF16)
        else:
            v_ref[:, lo - 2 * DN_KEY_DIM:lo - 2 * DN_KEY_DIM + DN_COLS] = y.astype(BF16)
    cst_ref[0] = hist_ref[TAPS_BACK - 1]

    for c in range(DN_VAL_DIM // DN_COLS):
        lo = c * DN_COLS
        z_ref[:, lo:lo + DN_COLS] = jnp.dot(xb, win_ref[:, CONV_DIM + lo:CONV_DIM + lo + DN_COLS],
                                            preferred_element_type=F32).astype(BF16)

    n_gate = 2 * DN_V_HEADS
    w_ab = jnp.concatenate([win_ref[:, CONV_DIM + DN_VAL_DIM:CONV_DIM + DN_VAL_DIM + n_gate],
                            jnp.zeros((D_MODEL, LANES - n_gate), BF16)], axis=1)
    ab = jnp.dot(xb, w_ab, preferred_element_type=F32)
    sp_in = ab + dtb_ref[...]
    softplus = jnp.maximum(sp_in, 0.0) + jnp.log1p(jnp.exp(-jnp.abs(sp_in)))
    gval = -jnp.exp(alog_ref[...]) * softplus
    gates_ref[...] = jnp.where(_iota(ab.shape, 1) < DN_V_HEADS, gval, _sigmoid(ab))


def _dn_in(x, state, w_in, cw, alog, dtb, *, tm, row_block0, n_seq, prev=None):
    steps_per_seq = (TP // tm) if n_seq == 1 else 1
    n_steps = steps_per_seq * n_seq
    n_alias = 0 if prev is None else len(prev)
    row = lambda width: pl.BlockSpec((tm, width), lambda i: (row_block0 + i, 0))
    state_spec = pl.BlockSpec((1, TAPS_BACK, HIST, CONV_DIM), lambda i: (i // steps_per_seq, 0, 0, 0))
    cst_spec = pl.BlockSpec((1, HIST, CONV_DIM), lambda i: (i // steps_per_seq, 0, 0))
    in_specs = [row(D_MODEL), state_spec, _full(w_in.shape), _full(cw.shape), _full(alog.shape), _full(dtb.shape)]
    in_specs += [pl.BlockSpec(memory_space=pl.ANY)] * n_alias
    out_shape = [jax.ShapeDtypeStruct((T, DN_KEY_DIM), BF16), jax.ShapeDtypeStruct((T, DN_KEY_DIM), BF16),
                 jax.ShapeDtypeStruct((T, DN_VAL_DIM), BF16), jax.ShapeDtypeStruct((T, DN_VAL_DIM), BF16),
                 jax.ShapeDtypeStruct((T, LANES), F32),
                 jax.ShapeDtypeStruct((n_seq, HIST, CONV_DIM), F32)]
    out_specs = [row(DN_KEY_DIM), row(DN_KEY_DIM), row(DN_VAL_DIM), row(DN_VAL_DIM), row(LANES), cst_spec]
    n_in = 6
    args = (x, state, w_in, cw, alog, dtb) + (tuple(prev) if prev is not None else ())
    return pl.pallas_call(
        functools.partial(_dn_in_kernel, tm=tm, steps_per_seq=steps_per_seq, n_alias=n_alias),
        grid=(n_steps,),
        in_specs=in_specs,
        out_specs=out_specs,
        out_shape=out_shape,
        scratch_shapes=[pltpu.VMEM((TAPS_BACK, HIST, CONV_DIM), F32),
                        pltpu.VMEM((2, CONV_WIDTH, tm + 2 * HIST, DN_COLS), F32)],
        input_output_aliases={n_in + k: k for k in range(n_alias)},
        compiler_params=_params(),
        name="dn_in_prompt" if n_seq == 1 else "dn_in_sample",
    )(*args)


GROUP_HEADS = MXU_DIM // CHUNK


def _split3(x):
    hi = x.astype(BF16)
    r = x - hi.astype(F32)
    mid = r.astype(BF16)
    lo = (r - mid.astype(F32)).astype(BF16)
    return hi, mid, lo


def _unit_lower_inverses(lcats, eye_cat, bd01):
    c = lcats[0].shape[0]

    def mm(a, b):
        b_bd = jnp.concatenate([b.astype(BF16)] * GROUP_HEADS, axis=0) * bd01
        return jnp.dot(a.astype(BF16), b_bd, preferred_element_type=F32)

    ms = [mm(l, l) for l in lcats]
    ps = [eye_cat - l for l in lcats]
    power = 2
    while 2 * power < c:
        pms = [mm(jnp.concatenate([p, m], axis=0), m) for p, m in zip(ps, ms)]
        ps = [p + pm[:c] for p, pm in zip(ps, pms)]
        ms = [pm[c:] for pm in pms]
        power *= 2
    return [p + mm(p, m) for p, m in zip(ps, ms)]


def _gdn_wy_kernel(q_ref, k_ref, v_ref, gates_ref, u_ref, w_ref, qd_ref, kdt_ref, qkd_ref, egl_ref, *, cb):
    c = CHUNK
    grp = GROUP_HEADS * c
    row = _iota((c, LANES), 0)
    lane = _iota((c, LANES), 1)
    col = lane & (c - 1)
    lo = lane < c
    causal2 = row >= col
    strict2 = row > col
    tril = jnp.where(_iota((c, c), 0) >= _iota((c, c), 1), 1.0, 0.0).astype(BF16)
    eye_cat = jnp.where(_iota((c, grp), 0) == (_iota((c, grp), 1) & (c - 1)), 1.0, 0.0).astype(F32)
    shift = c.bit_length() - 1
    bd01 = jnp.where((_iota((grp, grp), 0) >> shift) == (_iota((grp, grp), 1) >> shift), 1.0, 0.0).astype(BF16)
    nt = (((1,), (1,)), ((), ()))

    lcats, rhs = [], []
    for cc in range(cb):
        rows = slice(cc * c, (cc + 1) * c)
        gates = gates_ref[rows, :]
        cum = sum(jnp.dot(tril, part, preferred_element_type=F32) for part in _split3(gates))
        cum_t = jnp.concatenate([cum, cum], axis=0).T
        egl_ref[cc * DN_V_HEADS:(cc + 1) * DN_V_HEADS, :] = jnp.broadcast_to(
            jnp.exp(cum_t[0:DN_V_HEADS, c - 1:c]), (DN_V_HEADS, LANES))
        lmats = []
        for j in range(DN_QK_HEADS):
            a, b = 2 * j, 2 * j + 1
            tile = slice(j * LANES, (j + 1) * LANES)
            kb = k_ref[rows, tile]
            qb = q_ref[rows, tile]
            kq = lax.dot_general(jnp.concatenate([kb, qb], axis=0), jnp.concatenate([kb, kb], axis=0), nt,
                                 preferred_element_type=F32)
            gbc = jnp.where(lo, cum[:, a:a + 1], cum[:, b:b + 1])
            gbr = jnp.where(lo, cum_t[a:a + 1, :], cum_t[b:b + 1, :])
            beta2 = jnp.where(lo, gates[:, DN_V_HEADS + a:DN_V_HEADS + a + 1],
                              gates[:, DN_V_HEADS + b:DN_V_HEADS + b + 1])
            decay = jnp.where(causal2, jnp.exp(jnp.where(causal2, gbc - gbr, 0.0)), 0.0)
            lmats.append(jnp.where(strict2, kq[:c] * beta2 * decay, 0.0))
            qkd_ref[rows, tile] = (kq[c:] * decay).astype(BF16)
            kf = kb.astype(F32)
            qf = qb.astype(F32)
            kdec = []
            for h in (a, b):
                head = slice(h * DN_DV, (h + 1) * DN_DV)
                gb = cum[:, h:h + 1]
                beta = gates[:, DN_V_HEADS + h:DN_V_HEADS + h + 1]
                egb = jnp.exp(gb)
                vf = v_ref[rows, head].astype(F32)
                rhs.append(jnp.concatenate([vf * beta, kf * (beta * egb)], axis=1).astype(BF16))
                qd_ref[rows, head] = (qf * egb).astype(BF16)
                kdec.append(kf * jnp.exp(cum[c - 1:c, h:h + 1] - gb))
            kdt_ref[cc * DN_DK:(cc + 1) * DN_DK, tile] = jnp.concatenate(kdec, axis=0).T.astype(BF16)
        lcats += [jnp.concatenate(lmats[2 * m:2 * m + 2], axis=1) for m in range(DN_V_HEADS // GROUP_HEADS)]

    tinvs = _unit_lower_inverses(lcats, eye_cat, bd01)
    zeros = jnp.zeros((c, 2 * DN_DV), BF16)
    for cc in range(cb):
        rows = slice(cc * c, (cc + 1) * c)
        for j in range(DN_QK_HEADS):
            a = 2 * j
            h0 = cc * DN_V_HEADS + a
            pair = slice(a * DN_DV, (a + 2) * DN_DV)
            tinv = tinvs[cc * (DN_V_HEADS // GROUP_HEADS) + j // 2][:, (j % 2) * LANES:(j % 2 + 1) * LANES]
            rhs_bd = jnp.concatenate([jnp.concatenate([rhs[h0], zeros], axis=1),
                                      jnp.concatenate([zeros, rhs[h0 + 1]], axis=1)], axis=0)
            sol = jnp.dot(tinv.astype(BF16), rhs_bd, preferred_element_type=F32)
            u_ref[rows, pair] = jnp.concatenate(
                [sol[:, 0:DN_DV], sol[:, 2 * DN_DV:3 * DN_DV]], axis=1).astype(BF16)
            w_ref[rows, pair] = jnp.concatenate(
                [sol[:, DN_DV:2 * DN_DV], sol[:, 3 * DN_DV:4 * DN_DV]], axis=1).astype(BF16)


def _gdn_wy(qn, kn, v, gates, *, chunk0, n_chunks):
    cb = WY_CHUNKS
    tm = cb * CHUNK
    blk0 = chunk0 // cb
    tok_in = lambda width: pl.BlockSpec((tm, width), lambda i: (blk0 + i, 0))
    tok = lambda width: jax.ShapeDtypeStruct((n_chunks * CHUNK, width), BF16)
    return pl.pallas_call(
        functools.partial(_gdn_wy_kernel, cb=cb),
        grid=(n_chunks // cb,),
        in_specs=[tok_in(DN_KEY_DIM), tok_in(DN_KEY_DIM), tok_in(DN_VAL_DIM), tok_in(LANES)],
        out_specs=[_rows(tm, DN_VAL_DIM), _rows(tm, DN_VAL_DIM), _rows(tm, DN_VAL_DIM),
                   _rows(cb * DN_DK, DN_V_HEADS * CHUNK), _rows(tm, DN_V_HEADS * CHUNK),
                   _rows(cb * DN_V_HEADS, LANES)],
        out_shape=[tok(DN_VAL_DIM), tok(DN_VAL_DIM), tok(DN_VAL_DIM),
                   jax.ShapeDtypeStruct((n_chunks * DN_DK, DN_V_HEADS * CHUNK), BF16), tok(DN_V_HEADS * CHUNK),
                   jax.ShapeDtypeStruct((n_chunks * DN_V_HEADS, LANES), F32)],
        compiler_params=_params(),
        name="gdn_wy",
    )(qn, kn, v, gates)


def _gdn_rec_body(u_ref, w_ref, qd_ref, kdt_ref, qkd_ref, egl_ref, z_ref, nw_ref, o_ref, s_ref, cb):
    c = CHUNK
    nw = nw_ref[...]
    zero_s = jnp.zeros((DN_DK, DN_DV), BF16)
    zero_v = jnp.zeros((c, DN_DV), BF16)
    for cc in range(cb):
        rows = slice(cc * c, (cc + 1) * c)
        egl = egl_ref[cc * DN_V_HEADS:(cc + 1) * DN_V_HEADS, :]
        v_new, o_inter = [], []
        for m in range(DN_V_HEADS // 2):
            a = 2 * m
            pair = slice(a * DN_DV, (a + 2) * DN_DV)
            s_bd = jnp.concatenate(
                [jnp.concatenate([s_ref[0, a].astype(BF16), zero_s], axis=1),
                 jnp.concatenate([zero_s, s_ref[0, a + 1].astype(BF16)], axis=1)], axis=0)
            r = jnp.dot(jnp.concatenate([w_ref[rows, pair], qd_ref[rows, pair]], axis=0), s_bd,
                        preferred_element_type=F32)
            vn = u_ref[rows, pair].astype(F32) - r[:c]
            v_new += [vn[:, :DN_DV].astype(BF16), vn[:, DN_DV:].astype(BF16)]
            o_inter += [r[c:, :DN_DV], r[c:, DN_DV:]]
        for n in range(DN_V_HEADS // GROUP_HEADS):
            heads = range(GROUP_HEADS * n, GROUP_HEADS * (n + 1))
            v_bd = jnp.concatenate(
                [jnp.concatenate([v_new[h] if t == i else zero_v for t in range(GROUP_HEADS)], axis=1)
                 for i, h in enumerate(heads)], axis=0)
            grp = slice(n * GROUP_HEADS * c, (n + 1) * GROUP_HEADS * c)
            r2 = jnp.dot(jnp.concatenate([qkd_ref[rows, grp], kdt_ref[cc * DN_DK:(cc + 1) * DN_DK, grp]], axis=0),
                         v_bd, preferred_element_type=F32)
            for i, h in enumerate(heads):
                head = slice(h * DN_DV, (h + 1) * DN_DV)
                blk = slice(i * DN_DV, (i + 1) * DN_DV)
                s_ref[0, h] = s_ref[0, h] * egl[h:h + 1, :] + r2[c:, blk]
                o = o_inter[h] + r2[:c, blk]
                o = o * lax.rsqrt(jnp.mean(o * o, axis=-1, keepdims=True) + RMS_EPS) * nw
                zf = z_ref[rows, head].astype(F32)
                o_ref[rows, head] = (o * (zf * _sigmoid(zf))).astype(BF16)


def _gdn_rec_kernel(u_ref, w_ref, qd_ref, kdt_ref, qkd_ref, egl_ref, z_ref, nw_ref, sinit_ref, alias_ref,
                    o_ref, s_ref):
    del alias_ref
    s_ref[...] = sinit_ref[...]
    _gdn_rec_body(u_ref, w_ref, qd_ref, kdt_ref, qkd_ref, egl_ref, z_ref, nw_ref, o_ref, s_ref, 1)


def _gdn_rec_sample(u, w, qd, kdt, qkd, egl, z, nw, s_init, o_prev):
    rows = lambda r, width: pl.BlockSpec((r, width), lambda i: (i, 0))
    tok = pl.BlockSpec((CHUNK, DN_VAL_DIM), lambda i: (NPC + i, 0))
    state_spec = pl.BlockSpec((1, DN_V_HEADS, DN_DK, DN_DV), lambda i: (i, 0, 0, 0))
    return pl.pallas_call(
        _gdn_rec_kernel,
        grid=(DEC_BATCH,),
        in_specs=[rows(CHUNK, DN_VAL_DIM), rows(CHUNK, DN_VAL_DIM), rows(CHUNK, DN_VAL_DIM),
                  rows(DN_DK, DN_V_HEADS * CHUNK), rows(CHUNK, DN_V_HEADS * CHUNK), rows(DN_V_HEADS, LANES),
                  tok, _full(nw.shape), state_spec, pl.BlockSpec(memory_space=pl.ANY)],
        out_specs=[tok, state_spec],
        out_shape=[jax.ShapeDtypeStruct((T, DN_VAL_DIM), BF16), jax.ShapeDtypeStruct(s_init.shape, F32)],
        input_output_aliases={9: 0},
        compiler_params=_params(),
        name="gdn_rec_sample",
    )(u, w, qd, kdt, qkd, egl, z, nw, s_init, o_prev)


def _gdn_prompt_kernel(q_ref, k_ref, v_ref, gates_ref, z_ref, nw_ref, sinit_ref, o_ref, s_ref,
                       u_s, w_s, qd_s, kdt_s, qkd_s, egl_s, *, cb):
    i = pl.program_id(0)
    cur = i % 2
    prev = 1 - cur

    @pl.when(i == 0)
    def _():
        for ref in (u_s, w_s, qd_s, kdt_s, qkd_s, egl_s):
            ref[1] = jnp.zeros(ref.shape[1:], ref.dtype)

    @pl.when(i <= 1)
    def _():
        s_ref[...] = sinit_ref[...]

    _gdn_wy_kernel(q_ref, k_ref, v_ref, gates_ref, u_s.at[cur], w_s.at[cur], qd_s.at[cur], kdt_s.at[cur],
                   qkd_s.at[cur], egl_s.at[cur], cb=cb)
    _gdn_rec_body(u_s.at[prev], w_s.at[prev], qd_s.at[prev], kdt_s.at[prev], qkd_s.at[prev], egl_s.at[prev],
                  z_ref, nw_ref, o_ref, s_ref, cb)


def _gdn_prompt(qn, kn, v, gates, z, nw, s_init):
    cb = WY_CHUNKS
    tm = cb * CHUNK
    nblk = NPC // cb
    ahead = lambda width: pl.BlockSpec((tm, width), lambda i: (jnp.minimum(i, nblk - 1), 0))
    behind = lambda width: pl.BlockSpec((tm, width), lambda i: (jnp.maximum(i - 1, 0), 0))
    state_spec = pl.BlockSpec((1, DN_V_HEADS, DN_DK, DN_DV), lambda i: (0, 0, 0, 0))
    slots = lambda rows_, width, dtype: pltpu.VMEM((2, rows_, width), dtype)
    return pl.pallas_call(
        functools.partial(_gdn_prompt_kernel, cb=cb),
        grid=(nblk + 1,),
        in_specs=[ahead(DN_KEY_DIM), ahead(DN_KEY_DIM), ahead(DN_VAL_DIM), ahead(LANES), behind(DN_VAL_DIM),
                  _full(nw.shape), state_spec],
        out_specs=[behind(DN_VAL_DIM), state_spec],
        out_shape=[jax.ShapeDtypeStruct((T, DN_VAL_DIM), BF16), jax.ShapeDtypeStruct(s_init.shape, F32)],
        scratch_shapes=[slots(tm, DN_VAL_DIM, BF16), slots(tm, DN_VAL_DIM, BF16), slots(tm, DN_VAL_DIM, BF16),
                        slots(cb * DN_DK, DN_V_HEADS * CHUNK, BF16), slots(tm, DN_V_HEADS * CHUNK, BF16),
                        slots(cb * DN_V_HEADS, LANES, F32)],
        compiler_params=_params(),
        name="gdn_prompt",
    )(qn, kn, v, gates, z, nw, s_init)


def _lane_row(vec):
    return jnp.pad(vec.astype(F32), (0, LANES - vec.shape[0])).reshape(1, LANES)


def _history_tiles(state):
    tiles = [jnp.pad(state[:, TAPS_BACK - j:], ((0, 0), (0, HIST - j), (0, 0))) for j in range(1, CONV_WIDTH)]
    return jnp.stack(tiles, axis=1)


def _kvx_columns(k, v):
    parts = [t[..., h, :] for t in (k, v) for h in range(N_KV_HEADS) for _ in range(2)]
    return jnp.concatenate(parts, axis=-1)


def kernel(x_prompt, x_sample, cache_k, cache_v, state_conv, state_ssm, p_prompt, p_sample, w_qkv, b_qkv, attn_sinks, w_attn_out, w_dn_in, dn_conv_w, dn_a_log, dn_dt_bias, dn_norm_w, w_dn_out, w_ffn_gu, w_ffn_down, w_ple_gate, w_ple_proj, ln_g, ln_b):
    assert TS == TM, "the sample tokens must form exactly one row block"
    xp = x_prompt.reshape(TP, D_MODEL)
    xs = x_sample.reshape(TS, D_MODEL)
    ps = p_sample.reshape(DEPTH, TS, PLE_DIM)
    row = lambda v: v.reshape(1, -1)

    channel_weights = (w_ffn_gu.astype(BF16), w_ffn_down.astype(BF16),
                       w_ple_gate.astype(BF16), w_ple_proj.astype(BF16))

    def ln(i, k):
        return row(ln_g[i, k]), row(ln_b[i, k])

    kv_heads = (N_KV_HEADS, HEAD_DIM)
    w_q, w_k, w_v = jnp.split(w_qkv[0], [Q_DIM, Q_DIM + N_KV_HEADS * HEAD_DIM], axis=1)
    b_q, b_k, b_v = jnp.split(b_qkv[0], [Q_DIM, Q_DIM + N_KV_HEADS * HEAD_DIM])
    w_x = jnp.concatenate([w_q, _kvx_columns(w_k.reshape((D_MODEL,) + kv_heads),
                                             w_v.reshape((D_MODEL,) + kv_heads))], axis=1).astype(BF16)
    b_x = jnp.concatenate([b_q, _kvx_columns(b_k.reshape(kv_heads), b_v.reshape(kv_heads))])
    q, kv = _qkv_proj(xp, xs, w_x, row(b_x))
    cache = _kvx_columns(cache_k[0], cache_v[0]).reshape(DEC_BATCH * WINDOW, KVX_DIM)
    o = _attention(attn_sinks[0], q, kv, cache)
    x = _mix_ffn(o, w_attn_out[0].astype(BF16), xp, xs, 0, p_prompt, ps, 0, ln(0, 0), channel_weights,
                 ln(0, 1), split_out=False)

    def heads_of(rows_, first_tile):
        cols = [rows_[..., (first_tile + h) * LANES:(first_tile + h) * LANES + HEAD_DIM] for h in range(N_KV_HEADS)]
        return jnp.stack(cols, axis=-2)

    kv_new_s = kv[TP:].reshape(DEC_BATCH, DEC_SEQ, KVX_DIM)
    new_k_prompt = heads_of(kv[TP - WINDOW:TP], 0)[None, None]
    new_v_prompt = heads_of(kv[TP - WINDOW:TP], N_KV_HEADS)[None, None]
    new_k_sample = jnp.concatenate([cache_k[0][:, DEC_SEQ:], heads_of(kv_new_s, 0)], axis=1)[None]
    new_v_sample = jnp.concatenate([cache_v[0][:, DEC_SEQ:], heads_of(kv_new_s, N_KV_HEADS)], axis=1)[None]

    alog = _lane_row(dn_a_log[0])
    dtb = _lane_row(dn_dt_bias[0])
    st_prompt = jnp.zeros((1, TAPS_BACK, HIST, CONV_DIM), F32)
    st_sample = _history_tiles(state_conv[0])
    dn_w = (w_dn_in[0].astype(BF16), dn_conv_w[0], alog, dtb)
    *tok_p, cst_p = _dn_in(x, st_prompt, *dn_w, tm=TM, row_block0=0, n_seq=1)
    *tok, cst_s = _dn_in(x, st_sample, *dn_w, tm=DEC_SEQ, row_block0=NPC, n_seq=DEC_BATCH, prev=tok_p)
    qn, kn, v, z, gates = tok
    nw = row(dn_norm_w[0])
    s0_prompt = jnp.zeros((1,) + state_ssm.shape[2:], F32)
    o_p, s_p = _gdn_prompt(qn, kn, v, gates, z, nw, s0_prompt)
    wy_s = _gdn_wy(qn, kn, v, gates, chunk0=NPC, n_chunks=NCH - NPC)
    o, s_s = _gdn_rec_sample(*wy_s, z, nw, state_ssm[0], o_p)
    y_prompt, y_sample = _mix_ffn(o, w_dn_out[0].astype(BF16), x, x, TP // TM, p_prompt, ps, 1, ln(1, 0),
                                  channel_weights, ln(1, 1), split_out=True)

    return (y_prompt.reshape(1, TP, D_MODEL), y_sample.reshape(DEC_BATCH, DEC_SEQ, D_MODEL),
            new_k_prompt, new_v_prompt, new_k_sample, new_v_sample,
            cst_p[:, :TAPS_BACK][None], s_p[None], cst_s[:, :TAPS_BACK][None], s_s[None])
```

```python
import functools

import jax
import jax.numpy as jnp
from jax import lax
from jax.experimental import pallas as pl
from jax.experimental.pallas import tpu as pltpu

F32 = jnp.float32
BF16 = jnp.bfloat16

D_MODEL = 1024
SEQ = 16384
DEC_BATCH = 8
DEC_SEQ = 64
TP = SEQ
TS = DEC_BATCH * DEC_SEQ
T = TP + TS
CHUNK = 64
NPC = TP // CHUNK
NCH = T // CHUNK

N_HEADS = 16
N_KV_HEADS = 2
HEAD_DIM = 64
WINDOW = 128
Q_DIM = N_HEADS * HEAD_DIM
KVX_DIM = 2 * N_KV_HEADS * 2 * HEAD_DIM

DN_QK_HEADS = 8
DN_V_HEADS = 16
DN_DK = 128
DN_DV = 128
DN_KEY_DIM = DN_QK_HEADS * DN_DK
DN_VAL_DIM = DN_V_HEADS * DN_DV
CONV_DIM = 2 * DN_KEY_DIM + DN_VAL_DIM
CONV_WIDTH = 4

D_FF = 2816
PLE_DIM = 256
DEPTH = 2
ALPHA = (2 * DEPTH) ** 0.25
LN_EPS = 1e-5
RMS_EPS = 1e-6
NEG_INF = -1e30

LANES = 128
SUBLANES = 8
MXU_DIM = 256
TM = 512
WY_CHUNKS = 4
ATTN_CHUNKS = 8
VMEM_LIMIT = 56 * 1024 * 1024


def _params(vmem=VMEM_LIMIT):
    return pltpu.CompilerParams(dimension_semantics=("arbitrary",), vmem_limit_bytes=vmem)


def _full(shape):
    nd = len(shape)
    return pl.BlockSpec(shape, lambda i: (0,) * nd, pipeline_mode=pl.Buffered(1))


def _rows(tm, width):
    return pl.BlockSpec((tm, width), lambda i: (i, 0))


def _sigmoid(x):
    return 1.0 / (1.0 + jnp.exp(-x))


def _layer_norm(y, g, b):
    mu = jnp.mean(y, axis=-1, keepdims=True)
    d = y - mu
    var = jnp.mean(d * d, axis=-1, keepdims=True)
    return d * lax.rsqrt(var + LN_EPS) * g + b


def _iota(shape, axis):
    return lax.broadcasted_iota(jnp.int32, shape, axis)


def _main_tail(block, n_main, tail_block):
    main = pl.BlockSpec(block, lambda i: (jnp.minimum(i, n_main - 1), 0))
    tail = pl.BlockSpec(block, lambda i: (tail_block, 0))
    return main, tail


def _pick(main_ref, tail_ref, n_main):
    return jnp.where(pl.program_id(0) < n_main, main_ref[...], tail_ref[...])


def _qkv_kernel(xm_ref, xt_ref, w_ref, b_ref, q_ref, kv_ref, *, n_main):
    xb = _pick(xm_ref, xt_ref, n_main).astype(BF16)
    y = jnp.dot(xb, w_ref[...], preferred_element_type=F32) + b_ref[...]
    q_ref[...] = y[:, :Q_DIM].astype(BF16)
    kv_ref[...] = y[:, Q_DIM:]


def _qkv_proj(x_main, x_tail, w, b):
    n_main = TP // TM
    return pl.pallas_call(
        functools.partial(_qkv_kernel, n_main=n_main),
        grid=(T // TM,),
        in_specs=[*_main_tail((TM, D_MODEL), n_main, 0), _full(w.shape), _full(b.shape)],
        out_specs=[_rows(TM, Q_DIM), _rows(TM, KVX_DIM)],
        out_shape=[jax.ShapeDtypeStruct((T, Q_DIM), BF16), jax.ShapeDtypeStruct((T, KVX_DIM), F32)],
        compiler_params=_params(),
        name="qkv_proj",
    )(x_main, x_tail, w, b)


def _attn_kernel(sink_ref, q_ref, cur_ref, old_ref, cache_ref, o_ref, *, cb):
    step = pl.program_id(0)
    is_prompt = step < NPC // cb
    band = 3 * CHUNK
    keys = 2 * LANES
    pairs = Q_DIM // LANES
    per_kv = pairs // N_KV_HEADS
    key = _iota((CHUNK, 2 * keys), 1) & (keys - 1)
    lo = _iota((keys, LANES), 1) < HEAD_DIM
    zero = jnp.zeros((), BF16)
    pad = jnp.zeros((keys - band, KVX_DIM), F32)

    def halves(tile):
        return jnp.concatenate([jnp.where(lo, tile, zero), jnp.where(lo, zero, tile)], axis=0)

    for t in range(cb):
        rows = slice(t * CHUNK, (t + 1) * CHUNK)
        if t >= 2:
            before_p = cur_ref[(t - 2) * CHUNK:t * CHUNK, :]
        elif t == 1:
            before_p = jnp.concatenate([old_ref[CHUNK:2 * CHUNK, :], cur_ref[0:CHUNK, :]], axis=0)
        else:
            before_p = old_ref[...]
        before = jnp.where(is_prompt, before_p, cache_ref[t * WINDOW:(t + 1) * WINDOW, :])
        kvb = jnp.concatenate([before, cur_ref[rows, :], pad], axis=0).astype(BF16)
        start = jnp.where(is_prompt, jnp.maximum(2 - (step * cb + t), 0) * CHUNK, 0)
        valid = (key >= start) & (key < band)
        k_ops = [halves(kvb[:, kh * LANES:(kh + 1) * LANES]) for kh in range(N_KV_HEADS)]
        v_ops = [halves(kvb[:, (N_KV_HEADS + kh) * LANES:(N_KV_HEADS + kh + 1) * LANES])
                 for kh in range(N_KV_HEADS)]
        scores = [lax.dot_general(q_ref[rows, i * LANES:(i + 1) * LANES], k_ops[i // per_kv],
                                  (((1,), (1,)), ((), ())), preferred_element_type=F32) for i in range(pairs)]
        probs = []
        for i in range(pairs):
            s = jnp.where(valid, scores[i] * (HEAD_DIM ** -0.5), NEG_INF)
            ps = []
            for hh in range(2):
                sh = s[:, hh * keys:(hh + 1) * keys]
                sink = sink_ref[2 * i + hh]
                m = jnp.maximum(jnp.max(sh, axis=-1, keepdims=True), sink)
                p = jnp.exp(sh - m)
                denom = jnp.sum(p, axis=-1, keepdims=True) + jnp.exp(sink - m)
                ps.append(p * (1.0 / denom))
            probs.append(jnp.concatenate(ps, axis=1).astype(BF16))
        for i in range(pairs):
            o = jnp.dot(probs[i], v_ops[i // per_kv], preferred_element_type=F32)
            o_ref[rows, i * LANES:(i + 1) * LANES] = o.astype(BF16)


def _attention(sinks, q, kv, cache):
    cb = ATTN_CHUNKS
    tm = cb * CHUNK
    n_prompt = NPC // cb
    old_map = lambda i: (jnp.where(i < n_prompt, jnp.maximum(tm // WINDOW * i - 1, 0), 0), 0)
    cache_map = lambda i: (jnp.maximum(i - n_prompt, 0), 0)
    return pl.pallas_call(
        functools.partial(_attn_kernel, cb=cb),
        grid=(NCH // cb,),
        in_specs=[pl.BlockSpec(memory_space=pltpu.SMEM), _rows(tm, Q_DIM), _rows(tm, KVX_DIM),
                  pl.BlockSpec((WINDOW, KVX_DIM), old_map), pl.BlockSpec((cb * WINDOW, KVX_DIM), cache_map)],
        out_specs=_rows(tm, Q_DIM),
        out_shape=jax.ShapeDtypeStruct((T, Q_DIM), BF16),
        compiler_params=_params(),
        name="swa_attn",
    )(sinks, q, kv, kv, cache)


FF_CHUNK = 256
FF_ROWS = 512


def _mix_ffn_kernel(a_ref, wo_ref, xm_ref, xt_ref, pm_ref, pt_ref, g0_ref, b0_ref,
                    wgu_ref, wd_ref, wg_ref, wp_ref, g1_ref, b1_ref, *rest, n_main, split_out):
    act_ref = rest[-1]
    is_main = pl.program_id(0) < n_main
    outs = []
    for r0 in range(0, TM, FF_ROWS):
        rows = slice(r0, r0 + FF_ROWS)
        y = jnp.dot(a_ref[rows, :], wo_ref[...], preferred_element_type=F32)
        x_res = jnp.where(is_main, xm_ref[rows, :], xt_ref[rows, :])
        x = _layer_norm(ALPHA * x_res + y, g0_ref[...], b0_ref[...])
        xb = x.astype(BF16)
        for c in range(D_FF // FF_CHUNK):
            lo = c * FF_CHUNK
            gate = jnp.dot(xb, wgu_ref[:, lo:lo + FF_CHUNK], preferred_element_type=F32)
            up = jnp.dot(xb, wgu_ref[:, D_FF + lo:D_FF + lo + FF_CHUNK], preferred_element_type=F32)
            act_ref[rows, lo:lo + FF_CHUNK] = (gate * _sigmoid(gate) * up).astype(BF16)
        f = jnp.dot(act_ref[rows, :], wd_ref[...], preferred_element_type=F32)
        pb = jnp.where(is_main, pm_ref[rows, :], pt_ref[rows, :]).astype(BF16)
        e = _sigmoid(jnp.dot(xb, wg_ref[...], preferred_element_type=F32)) * jnp.dot(
            pb, wp_ref[...], preferred_element_type=F32)
        out = _layer_norm(ALPHA * x + f + e, g1_ref[...], b1_ref[...])
        if split_out:
            outs.append(out)
        else:
            rest[0][rows, :] = out
    if split_out:
        om_ref, ot_ref = rest[:2]

        @pl.when(is_main)
        def _():
            for k, out in enumerate(outs):
                om_ref[k * FF_ROWS:(k + 1) * FF_ROWS, :] = out

        @pl.when(jnp.logical_not(is_main))
        def _():
            for k, out in enumerate(outs):
                ot_ref[k * FF_ROWS:(k + 1) * FF_ROWS, :] = out


def _layer_of(stacked, layer):
    nd = stacked.ndim - 1
    return pl.BlockSpec((None,) + stacked.shape[1:], lambda i: (layer,) + (0,) * nd, pipeline_mode=pl.Buffered(1))


def _mix_ffn(a, wo, x_main, x_tail, x_tail_block, p_main, p_tail, layer, ln0, weights, ln1, *, split_out):
    n_main = TP // TM
    in_specs = [_rows(TM, a.shape[1]), _full(wo.shape),
                *_main_tail((TM, D_MODEL), n_main, x_tail_block),
                pl.BlockSpec((None, None, TM, PLE_DIM), lambda i: (layer, 0, jnp.minimum(i, n_main - 1), 0)),
                pl.BlockSpec((None, TM, PLE_DIM), lambda i: (layer, 0, 0)),
                _full(ln0[0].shape), _full(ln0[1].shape), *[_layer_of(w, layer) for w in weights],
                _full(ln1[0].shape), _full(ln1[1].shape)]
    if split_out:
        out_specs = list(_main_tail((TM, D_MODEL), n_main, 0))
        out_shape = [jax.ShapeDtypeStruct((TP, D_MODEL), F32), jax.ShapeDtypeStruct((TS, D_MODEL), F32)]
    else:
        out_specs = _rows(TM, D_MODEL)
        out_shape = jax.ShapeDtypeStruct((T, D_MODEL), F32)
    return pl.pallas_call(
        functools.partial(_mix_ffn_kernel, n_main=n_main, split_out=split_out),
        grid=(T // TM,),
        in_specs=in_specs,
        out_specs=out_specs,
        out_shape=out_shape,
        scratch_shapes=[pltpu.VMEM((TM, D_FF), BF16)],
        compiler_params=_params(),
        name="mix_ffn",
    )(a, wo, x_main, x_tail, p_main, p_tail, *ln0, *weights, *ln1)


DN_COLS = 512
HIST = SUBLANES
TAPS_BACK = CONV_WIDTH - 1


def _dn_in_kernel(x_ref, st_ref, win_ref, cw_ref, alog_ref, dtb_ref, *rest, tm, steps_per_seq, n_alias):
    qn_ref, kn_ref, v_ref, z_ref, gates_ref, cst_ref, hist_ref, tmp_ref = rest[n_alias:]
    i = pl.program_id(0)

    @pl.when(i % steps_per_seq == 0)
    def _():
        hist_ref[...] = st_ref[0]

    xb = x_ref[...].astype(BF16)
    for c in range(CONV_DIM // DN_COLS):
        lo = c * DN_COLS
        cols = slice(lo, lo + DN_COLS)
        buf = tmp_ref.at[c % 2]
        mm = jnp.dot(xb, win_ref[:, cols], preferred_element_type=F32)
        buf[0, HIST:HIST + tm, :] = mm
        for j in range(1, CONV_WIDTH):
            buf[j, HIST:2 * HIST, :] = hist_ref[j - 1, :, cols]
            buf[j, HIST + j:HIST + j + tm, :] = mm
            hist_ref[j - 1, :, cols] = buf[j, HIST + tm:2 * HIST + tm, :]
        cw = cw_ref[:, cols]
        conv = cw[3:4] * buf[0, HIST:HIST + tm, :]
        for j in range(1, CONV_WIDTH):
            conv = conv + cw[3 - j:4 - j] * buf[j, HIST:HIST + tm, :]
        y = conv * _sigmoid(conv)
        if lo < 2 * DN_KEY_DIM:
            is_q = lo < DN_KEY_DIM
            dst = qn_ref if is_q else kn_ref
            base = lo if is_q else lo - DN_KEY_DIM
            for t in range(DN_COLS // DN_DK):
                yt = y[:, t * DN_DK:(t + 1) * DN_DK]
                n = yt * lax.rsqrt(jnp.sum(yt * yt, axis=-1, keepdims=True) + 1e-6)
                if is_q:
                    n = n * (DN_DK ** -0.5)
                dst[:, base + t * DN_DK:base + (t + 1) * DN_DK] = n.astype(BF16)
        else:
            v_ref[:, lo - 2 * DN_KEY_DIM:lo - 2 * DN_KEY_DIM + DN_COLS] = y.astype(BF16)
    cst_ref[0] = hist_ref[TAPS_BACK - 1]

    for c in range(DN_VAL_DIM // DN_COLS):
        lo = c * DN_COLS
        z_ref[:, lo:lo + DN_COLS] = jnp.dot(xb, win_ref[:, CONV_DIM + lo:CONV_DIM + lo + DN_COLS],
                                            preferred_element_type=F32).astype(BF16)

    n_gate = 2 * DN_V_HEADS
    w_ab = jnp.concatenate([win_ref[:, CONV_DIM + DN_VAL_DIM:CONV_DIM + DN_VAL_DIM + n_gate],
                            jnp.zeros((D_MODEL, LANES - n_gate), BF16)], axis=1)
    ab = jnp.dot(xb, w_ab, preferred_element_type=F32)
    sp_in = ab + dtb_ref[...]
    softplus = jnp.maximum(sp_in, 0.0) + jnp.log1p(jnp.exp(-jnp.abs(sp_in)))
    gval = -jnp.exp(alog_ref[...]) * softplus
    gates_ref[...] = jnp.where(_iota(ab.shape, 1) < DN_V_HEADS, gval, _sigmoid(ab))


def _dn_in(x, state, w_in, cw, alog, dtb, *, tm, row_block0, n_seq, prev=None):
    steps_per_seq = (TP // tm) if n_seq == 1 else 1
    n_steps = steps_per_seq * n_seq
    n_alias = 0 if prev is None else len(prev)
    row = lambda width: pl.BlockSpec((tm, width), lambda i: (row_block0 + i, 0))
    state_spec = pl.BlockSpec((1, TAPS_BACK, HIST, CONV_DIM), lambda i: (i // steps_per_seq, 0, 0, 0))
    cst_spec = pl.BlockSpec((1, HIST, CONV_DIM), lambda i: (i // steps_per_seq, 0, 0))
    in_specs = [row(D_MODEL), state_spec, _full(w_in.shape), _full(cw.shape), _full(alog.shape), _full(dtb.shape)]
    in_specs += [pl.BlockSpec(memory_space=pl.ANY)] * n_alias
    out_shape = [jax.ShapeDtypeStruct((T, DN_KEY_DIM), BF16), jax.ShapeDtypeStruct((T, DN_KEY_DIM), BF16),
                 jax.ShapeDtypeStruct((T, DN_VAL_DIM), BF16), jax.ShapeDtypeStruct((T, DN_VAL_DIM), BF16),
                 jax.ShapeDtypeStruct((T, LANES), F32),
                 jax.ShapeDtypeStruct((n_seq, HIST, CONV_DIM), F32)]
    out_specs = [row(DN_KEY_DIM), row(DN_KEY_DIM), row(DN_VAL_DIM), row(DN_VAL_DIM), row(LANES), cst_spec]
    n_in = 6
    args = (x, state, w_in, cw, alog, dtb) + (tuple(prev) if prev is not None else ())
    return pl.pallas_call(
        functools.partial(_dn_in_kernel, tm=tm, steps_per_seq=steps_per_seq, n_alias=n_alias),
        grid=(n_steps,),
        in_specs=in_specs,
        out_specs=out_specs,
        out_shape=out_shape,
        scratch_shapes=[pltpu.VMEM((TAPS_BACK, HIST, CONV_DIM), F32),
                        pltpu.VMEM((2, CONV_WIDTH, tm + 2 * HIST, DN_COLS), F32)],
        input_output_aliases={n_in + k: k for k in range(n_alias)},
        compiler_params=_params(),
        name="dn_in_prompt" if n_seq == 1 else "dn_in_sample",
    )(*args)


GROUP_HEADS = MXU_DIM // CHUNK


def _split3(x):
    hi = x.astype(BF16)
    r = x - hi.astype(F32)
    mid = r.astype(BF16)
    lo = (r - mid.astype(F32)).astype(BF16)
    return hi, mid, lo


def _unit_lower_inverses(lcats, eye_cat, bd01):
    c = lcats[0].shape[0]

    def mm(a, b):
        b_bd = jnp.concatenate([b.astype(BF16)] * GROUP_HEADS, axis=0) * bd01
        return jnp.dot(a.astype(BF16), b_bd, preferred_element_type=F32)

    ms = [mm(l, l) for l in lcats]
    ps = [eye_cat - l for l in lcats]
    power = 2
    while 2 * power < c:
        pms = [mm(jnp.concatenate([p, m], axis=0), m) for p, m in zip(ps, ms)]
        ps = [p + pm[:c] for p, pm in zip(ps, pms)]
        ms = [pm[c:] for pm in pms]
        power *= 2
    return [p + mm(p, m) for p, m in zip(ps, ms)]


def _gdn_wy_kernel(q_ref, k_ref, v_ref, gates_ref, u_ref, w_ref, qd_ref, kdt_ref, qkd_ref, egl_ref, *, cb):
    c = CHUNK
    grp = GROUP_HEADS * c
    row = _iota((c, LANES), 0)
    lane = _iota((c, LANES), 1)
    col = lane & (c - 1)
    lo = lane < c
    causal2 = row >= col
    strict2 = row > col
    tril = jnp.where(_iota((c, c), 0) >= _iota((c, c), 1), 1.0, 0.0).astype(BF16)
    eye_cat = jnp.where(_iota((c, grp), 0) == (_iota((c, grp), 1) & (c - 1)), 1.0, 0.0).astype(F32)
    shift = c.bit_length() - 1
    bd01 = jnp.where((_iota((grp, grp), 0) >> shift) == (_iota((grp, grp), 1) >> shift), 1.0, 0.0).astype(BF16)
    nt = (((1,), (1,)), ((), ()))

    lcats, rhs = [], []
    for cc in range(cb):
        rows = slice(cc * c, (cc + 1) * c)
        gates = gates_ref[rows, :]
        cum = sum(jnp.dot(tril, part, preferred_element_type=F32) for part in _split3(gates))
        cum_t = jnp.concatenate([cum, cum], axis=0).T
        egl_ref[cc * DN_V_HEADS:(cc + 1) * DN_V_HEADS, :] = jnp.broadcast_to(
            jnp.exp(cum_t[0:DN_V_HEADS, c - 1:c]), (DN_V_HEADS, LANES))
        lmats = []
        for j in range(DN_QK_HEADS):
            a, b = 2 * j, 2 * j + 1
            tile = slice(j * LANES, (j + 1) * LANES)
            kb = k_ref[rows, tile]
            qb = q_ref[rows, tile]
            kq = lax.dot_general(jnp.concatenate([kb, qb], axis=0), jnp.concatenate([kb, kb], axis=0), nt,
                                 preferred_element_type=F32)
            gb_lanes = [jnp.broadcast_to(cum[:, h:h + 1], (c, LANES)) for h in (a, b)]
            beta_lanes = [jnp.broadcast_to(gates[:, DN_V_HEADS + h:DN_V_HEADS + h + 1], (c, LANES)) for h in (a, b)]
            gbc = jnp.where(lo, gb_lanes[0], gb_lanes[1])
            gbr = jnp.where(lo, cum_t[a:a + 1, :], cum_t[b:b + 1, :])
            beta2 = jnp.where(lo, beta_lanes[0], beta_lanes[1])
            decay = jnp.where(causal2, jnp.exp(jnp.where(causal2, gbc - gbr, 0.0)), 0.0)
            lmats.append(jnp.where(strict2, kq[:c] * beta2 * decay, 0.0))
            qkd_ref[rows, tile] = (kq[c:] * decay).astype(BF16)
            kf = kb.astype(F32)
            qf = qb.astype(F32)
            kdec = []
            for h, gb, beta in zip((a, b), gb_lanes, beta_lanes):
                head = slice(h * DN_DV, (h + 1) * DN_DV)
                egb = jnp.exp(gb)
                vf = v_ref[rows, head].astype(F32)
                rhs.append(jnp.concatenate([vf * beta, kf * (beta * egb)], axis=1).astype(BF16))
                qd_ref[rows, head] = (qf * egb).astype(BF16)
                kdec.append(kf * jnp.exp(gb[c - 1:c, :] - gb))
            kdt_ref[cc * DN_DK:(cc + 1) * DN_DK, tile] = jnp.concatenate(kdec, axis=0).T.astype(BF16)
        lcats += [jnp.concatenate(lmats[2 * m:2 * m + 2], axis=1) for m in range(DN_V_HEADS // GROUP_HEADS)]

    tinvs = _unit_lower_inverses(lcats, eye_cat, bd01)
    zeros = jnp.zeros((c, 2 * DN_DV), BF16)
    for cc in range(cb):
        rows = slice(cc * c, (cc + 1) * c)
        for j in range(DN_QK_HEADS):
            a = 2 * j
            h0 = cc * DN_V_HEADS + a
            pair = slice(a * DN_DV, (a + 2) * DN_DV)
            tinv = tinvs[cc * (DN_V_HEADS // GROUP_HEADS) + j // 2][:, (j % 2) * LANES:(j % 2 + 1) * LANES]
            rhs_bd = jnp.concatenate([jnp.concatenate([rhs[h0], zeros], axis=1),
                                      jnp.concatenate([zeros, rhs[h0 + 1]], axis=1)], axis=0)
            sol = jnp.dot(tinv.astype(BF16), rhs_bd, preferred_element_type=F32)
            u_ref[rows, pair] = jnp.concatenate(
                [sol[:, 0:DN_DV], sol[:, 2 * DN_DV:3 * DN_DV]], axis=1).astype(BF16)
            w_ref[rows, pair] = jnp.concatenate(
                [sol[:, DN_DV:2 * DN_DV], sol[:, 3 * DN_DV:4 * DN_DV]], axis=1).astype(BF16)


def _gdn_wy(qn, kn, v, gates, *, chunk0, n_chunks):
    cb = WY_CHUNKS
    tm = cb * CHUNK
    blk0 = chunk0 // cb
    tok_in = lambda width: pl.BlockSpec((tm, width), lambda i: (blk0 + i, 0))
    tok = lambda width: jax.ShapeDtypeStruct((n_chunks * CHUNK, width), BF16)
    return pl.pallas_call(
        functools.partial(_gdn_wy_kernel, cb=cb),
        grid=(n_chunks // cb,),
        in_specs=[tok_in(DN_KEY_DIM), tok_in(DN_KEY_DIM), tok_in(DN_VAL_DIM), tok_in(LANES)],
        out_specs=[_rows(tm, DN_VAL_DIM), _rows(tm, DN_VAL_DIM), _rows(tm, DN_VAL_DIM),
                   _rows(cb * DN_DK, DN_V_HEADS * CHUNK), _rows(tm, DN_V_HEADS * CHUNK),
                   _rows(cb * DN_V_HEADS, LANES)],
        out_shape=[tok(DN_VAL_DIM), tok(DN_VAL_DIM), tok(DN_VAL_DIM),
                   jax.ShapeDtypeStruct((n_chunks * DN_DK, DN_V_HEADS * CHUNK), BF16), tok(DN_V_HEADS * CHUNK),
                   jax.ShapeDtypeStruct((n_chunks * DN_V_HEADS, LANES), F32)],
        compiler_params=_params(),
        name="gdn_wy",
    )(qn, kn, v, gates)


def _gdn_rec_body(u_ref, w_ref, qd_ref, kdt_ref, qkd_ref, egl_ref, z_ref, nw_ref, o_ref, s_ref, cb):
    c = CHUNK
    nw = nw_ref[...]
    zero_s = jnp.zeros((DN_DK, DN_DV), BF16)
    zero_v = jnp.zeros((c, DN_DV), BF16)
    for cc in range(cb):
        rows = slice(cc * c, (cc + 1) * c)
        egl = egl_ref[cc * DN_V_HEADS:(cc + 1) * DN_V_HEADS, :]
        v_new, o_inter = [], []
        for m in range(DN_V_HEADS // 2):
            a = 2 * m
            pair = slice(a * DN_DV, (a + 2) * DN_DV)
            s_bd = jnp.concatenate(
                [jnp.concatenate([s_ref[0, a].astype(BF16), zero_s], axis=1),
                 jnp.concatenate([zero_s, s_ref[0, a + 1].astype(BF16)], axis=1)], axis=0)
            r = jnp.dot(jnp.concatenate([w_ref[rows, pair], qd_ref[rows, pair]], axis=0), s_bd,
                        preferred_element_type=F32)
            vn = u_ref[rows, pair].astype(F32) - r[:c]
            v_new += [vn[:, :DN_DV].astype(BF16), vn[:, DN_DV:].astype(BF16)]
            o_inter += [r[c:, :DN_DV], r[c:, DN_DV:]]
        for n in range(DN_V_HEADS // GROUP_HEADS):
            heads = range(GROUP_HEADS * n, GROUP_HEADS * (n + 1))
            v_bd = jnp.concatenate(
                [jnp.concatenate([v_new[h] if t == i else zero_v for t in range(GROUP_HEADS)], axis=1)
                 for i, h in enumerate(heads)], axis=0)
            grp = slice(n * GROUP_HEADS * c, (n + 1) * GROUP_HEADS * c)
            r2 = jnp.dot(jnp.concatenate([qkd_ref[rows, grp], kdt_ref[cc * DN_DK:(cc + 1) * DN_DK, grp]], axis=0),
                         v_bd, preferred_element_type=F32)
            for i, h in enumerate(heads):
                head = slice(h * DN_DV, (h + 1) * DN_DV)
                blk = slice(i * DN_DV, (i + 1) * DN_DV)
                s_ref[0, h] = s_ref[0, h] * egl[h:h + 1, :] + r2[c:, blk]
                o = o_inter[h] + r2[:c, blk]
                o = o * lax.rsqrt(jnp.mean(o * o, axis=-1, keepdims=True) + RMS_EPS) * nw
                zf = z_ref[rows, head].astype(F32)
                o_ref[rows, head] = (o * (zf * _sigmoid(zf))).astype(BF16)


def _gdn_rec_kernel(u_ref, w_ref, qd_ref, kdt_ref, qkd_ref, egl_ref, z_ref, nw_ref, sinit_ref, alias_ref,
                    o_ref, s_ref):
    del alias_ref
    s_ref[...] = sinit_ref[...]
    _gdn_rec_body(u_ref, w_ref, qd_ref, kdt_ref, qkd_ref, egl_ref, z_ref, nw_ref, o_ref, s_ref, 1)


def _gdn_rec_sample(u, w, qd, kdt, qkd, egl, z, nw, s_init, o_prev):
    rows = lambda r, width: pl.BlockSpec((r, width), lambda i: (i, 0))
    tok = pl.BlockSpec((CHUNK, DN_VAL_DIM), lambda i: (NPC + i, 0))
    state_spec = pl.BlockSpec((1, DN_V_HEADS, DN_DK, DN_DV), lambda i: (i, 0, 0, 0))
    return pl.pallas_call(
        _gdn_rec_kernel,
        grid=(DEC_BATCH,),
        in_specs=[rows(CHUNK, DN_VAL_DIM), rows(CHUNK, DN_VAL_DIM), rows(CHUNK, DN_VAL_DIM),
                  rows(DN_DK, DN_V_HEADS * CHUNK), rows(CHUNK, DN_V_HEADS * CHUNK), rows(DN_V_HEADS, LANES),
                  tok, _full(nw.shape), state_spec, pl.BlockSpec(memory_space=pl.ANY)],
        out_specs=[tok, state_spec],
        out_shape=[jax.ShapeDtypeStruct((T, DN_VAL_DIM), BF16), jax.ShapeDtypeStruct(s_init.shape, F32)],
        input_output_aliases={9: 0},
        compiler_params=_params(),
        name="gdn_rec_sample",
    )(u, w, qd, kdt, qkd, egl, z, nw, s_init, o_prev)


def _gdn_prompt_kernel(q_ref, k_ref, v_ref, gates_ref, z_ref, nw_ref, sinit_ref, o_ref, s_ref,
                       u_s, w_s, qd_s, kdt_s, qkd_s, egl_s, *, cb):
    i = pl.program_id(0)
    cur = i % 2
    prev = 1 - cur

    @pl.when(i == 0)
    def _():
        for ref in (u_s, w_s, qd_s, kdt_s, qkd_s, egl_s):
            ref[1] = jnp.zeros(ref.shape[1:], ref.dtype)

    @pl.when(i <= 1)
    def _():
        s_ref[...] = sinit_ref[...]

    _gdn_wy_kernel(q_ref, k_ref, v_ref, gates_ref, u_s.at[cur], w_s.at[cur], qd_s.at[cur], kdt_s.at[cur],
                   qkd_s.at[cur], egl_s.at[cur], cb=cb)
    _gdn_rec_body(u_s.at[prev], w_s.at[prev], qd_s.at[prev], kdt_s.at[prev], qkd_s.at[prev], egl_s.at[prev],
                  z_ref, nw_ref, o_ref, s_ref, cb)


def _gdn_prompt(qn, kn, v, gates, z, nw, s_init):
    cb = WY_CHUNKS
    tm = cb * CHUNK
    nblk = NPC // cb
    ahead = lambda width: pl.BlockSpec((tm, width), lambda i: (jnp.minimum(i, nblk - 1), 0))
    behind = lambda width: pl.BlockSpec((tm, width), lambda i: (jnp.maximum(i - 1, 0), 0))
    state_spec = pl.BlockSpec((1, DN_V_HEADS, DN_DK, DN_DV), lambda i: (0, 0, 0, 0))
    slots = lambda rows_, width, dtype: pltpu.VMEM((2, rows_, width), dtype)
    return pl.pallas_call(
        functools.partial(_gdn_prompt_kernel, cb=cb),
        grid=(nblk + 1,),
        in_specs=[ahead(DN_KEY_DIM), ahead(DN_KEY_DIM), ahead(DN_VAL_DIM), ahead(LANES), behind(DN_VAL_DIM),
                  _full(nw.shape), state_spec],
        out_specs=[behind(DN_VAL_DIM), state_spec],
        out_shape=[jax.ShapeDtypeStruct((T, DN_VAL_DIM), BF16), jax.ShapeDtypeStruct(s_init.shape, F32)],
        scratch_shapes=[slots(tm, DN_VAL_DIM, BF16), slots(tm, DN_VAL_DIM, BF16), slots(tm, DN_VAL_DIM, BF16),
                        slots(cb * DN_DK, DN_V_HEADS * CHUNK, BF16), slots(tm, DN_V_HEADS * CHUNK, BF16),
                        slots(cb * DN_V_HEADS, LANES, F32)],
        compiler_params=_params(),
        name="gdn_prompt",
    )(qn, kn, v, gates, z, nw, s_init)


def _lane_row(vec):
    return jnp.pad(vec.astype(F32), (0, LANES - vec.shape[0])).reshape(1, LANES)


def _history_tiles(state):
    tiles = [jnp.pad(state[:, TAPS_BACK - j:], ((0, 0), (0, HIST - j), (0, 0))) for j in range(1, CONV_WIDTH)]
    return jnp.stack(tiles, axis=1)


def _kvx_columns(k, v):
    parts = [t[..., h, :] for t in (k, v) for h in range(N_KV_HEADS) for _ in range(2)]
    return jnp.concatenate(parts, axis=-1)


def kernel(x_prompt, x_sample, cache_k, cache_v, state_conv, state_ssm, p_prompt, p_sample, w_qkv, b_qkv, attn_sinks, w_attn_out, w_dn_in, dn_conv_w, dn_a_log, dn_dt_bias, dn_norm_w, w_dn_out, w_ffn_gu, w_ffn_down, w_ple_gate, w_ple_proj, ln_g, ln_b):
    assert TS == TM, "the sample tokens must form exactly one row block"
    xp = x_prompt.reshape(TP, D_MODEL)
    xs = x_sample.reshape(TS, D_MODEL)
    ps = p_sample.reshape(DEPTH, TS, PLE_DIM)
    row = lambda v: v.reshape(1, -1)

    channel_weights = (w_ffn_gu.astype(BF16), w_ffn_down.astype(BF16),
                       w_ple_gate.astype(BF16), w_ple_proj.astype(BF16))

    def ln(i, k):
        return row(ln_g[i, k]), row(ln_b[i, k])

    kv_heads = (N_KV_HEADS, HEAD_DIM)
    w_q, w_k, w_v = jnp.split(w_qkv[0], [Q_DIM, Q_DIM + N_KV_HEADS * HEAD_DIM], axis=1)
    b_q, b_k, b_v = jnp.split(b_qkv[0], [Q_DIM, Q_DIM + N_KV_HEADS * HEAD_DIM])
    w_x = jnp.concatenate([w_q, _kvx_columns(w_k.reshape((D_MODEL,) + kv_heads),
                                             w_v.reshape((D_MODEL,) + kv_heads))], axis=1).astype(BF16)
    b_x = jnp.concatenate([b_q, _kvx_columns(b_k.reshape(kv_heads), b_v.reshape(kv_heads))])
    q, kv = _qkv_proj(xp, xs, w_x, row(b_x))
    cache = _kvx_columns(cache_k[0], cache_v[0]).reshape(DEC_BATCH * WINDOW, KVX_DIM)
    o = _attention(attn_sinks[0], q, kv, cache)
    x = _mix_ffn(o, w_attn_out[0].astype(BF16), xp, xs, 0, p_prompt, ps, 0, ln(0, 0), channel_weights,
                 ln(0, 1), split_out=False)

    def heads_of(rows_, first_tile):
        cols = [rows_[..., (first_tile + h) * LANES:(first_tile + h) * LANES + HEAD_DIM] for h in range(N_KV_HEADS)]
        return jnp.stack(cols, axis=-2)

    kv_new_s = kv[TP:].reshape(DEC_BATCH, DEC_SEQ, KVX_DIM)
    new_k_prompt = heads_of(kv[TP - WINDOW:TP], 0)[None, None]
    new_v_prompt = heads_of(kv[TP - WINDOW:TP], N_KV_HEADS)[None, None]
    new_k_sample = jnp.concatenate([cache_k[0][:, DEC_SEQ:], heads_of(kv_new_s, 0)], axis=1)[None]
    new_v_sample = jnp.concatenate([cache_v[0][:, DEC_SEQ:], heads_of(kv_new_s, N_KV_HEADS)], axis=1)[None]

    alog = _lane_row(dn_a_log[0])
    dtb = _lane_row(dn_dt_bias[0])
    st_prompt = jnp.zeros((1, TAPS_BACK, HIST, CONV_DIM), F32)
    st_sample = _history_tiles(state_conv[0])
    w_in = jnp.pad(w_dn_in[0], ((0, 0), (0, -w_dn_in.shape[-1] % LANES))).astype(BF16)
    dn_w = (w_in, dn_conv_w[0], alog, dtb)
    *tok_p, cst_p = _dn_in(x, st_prompt, *dn_w, tm=TM, row_block0=0, n_seq=1)
    *tok, cst_s = _dn_in(x, st_sample, *dn_w, tm=DEC_SEQ, row_block0=NPC, n_seq=DEC_BATCH, prev=tok_p)
    qn, kn, v, z, gates = tok
    nw = row(dn_norm_w[0])
    s0_prompt = jnp.zeros((1,) + state_ssm.shape[2:], F32)
    o_p, s_p = _gdn_prompt(qn, kn, v, gates, z, nw, s0_prompt)
    wy_s = _gdn_wy(qn, kn, v, gates, chunk0=NPC, n_chunks=NCH - NPC)
    o, s_s = _gdn_rec_sample(*wy_s, z, nw, state_ssm[0], o_p)
    y_prompt, y_sample = _mix_ffn(o, w_dn_out[0].astype(BF16), x, x, TP // TM, p_prompt, ps, 1, ln(1, 0),
                                  channel_weights, ln(1, 1), split_out=True)

    return (y_prompt.reshape(1, TP, D_MODEL), y_sample.reshape(DEC_BATCH, DEC_SEQ, D_MODEL),
            new_k_prompt, new_v_prompt, new_k_sample, new_v_sample,
            cst_p[:, :TAPS_BACK][None], s_p[None], cst_s[:, :TAPS_BACK][None], s_s[None])
```

```python
import functools

import jax
import jax.numpy as jnp
from jax import lax
from jax.experimental import pallas as pl
from jax.experimental.pallas import tpu as pltpu

F32 = jnp.float32
BF16 = jnp.bfloat16

D_MODEL = 1024
SEQ = 16384
DEC_BATCH = 8
DEC_SEQ = 64
TP = SEQ
TS = DEC_BATCH * DEC_SEQ
T = TP + TS
CHUNK = 64
NPC = TP // CHUNK
NCH = T // CHUNK

N_HEADS = 16
N_KV_HEADS = 2
HEAD_DIM = 64
WINDOW = 128
Q_DIM = N_HEADS * HEAD_DIM
KVX_DIM = 2 * N_KV_HEADS * 2 * HEAD_DIM

DN_QK_HEADS = 8
DN_V_HEADS = 16
DN_DK = 128
DN_DV = 128
DN_KEY_DIM = DN_QK_HEADS * DN_DK
DN_VAL_DIM = DN_V_HEADS * DN_DV
CONV_DIM = 2 * DN_KEY_DIM + DN_VAL_DIM
CONV_WIDTH = 4

D_FF = 2816
PLE_DIM = 256
DEPTH = 2
ALPHA = (2 * DEPTH) ** 0.25
LN_EPS = 1e-5
RMS_EPS = 1e-6
NEG_INF = -1e30

LANES = 128
SUBLANES = 8
MXU_DIM = 256
TM = 512
WY_CHUNKS = 4
VMEM_LIMIT = 56 * 1024 * 1024


def _params(vmem=VMEM_LIMIT):
    return pltpu.CompilerParams(dimension_semantics=("arbitrary",), vmem_limit_bytes=vmem)


def _full(shape):
    nd = len(shape)
    return pl.BlockSpec(shape, lambda i: (0,) * nd, pipeline_mode=pl.Buffered(1))


def _rows(tm, width):
    return pl.BlockSpec((tm, width), lambda i: (i, 0))


def _sigmoid(x):
    return 1.0 / (1.0 + jnp.exp(-x))


def _layer_norm(y, g, b):
    mu = jnp.mean(y, axis=-1, keepdims=True)
    d = y - mu
    var = jnp.mean(d * d, axis=-1, keepdims=True)
    return d * lax.rsqrt(var + LN_EPS) * g + b


def _iota(shape, axis):
    return lax.broadcasted_iota(jnp.int32, shape, axis)


def _main_tail(block, n_main, tail_block):
    main = pl.BlockSpec(block, lambda i: (jnp.minimum(i, n_main - 1), 0))
    tail = pl.BlockSpec(block, lambda i: (tail_block, 0))
    return main, tail


def _attn_kernel(sink_ref, xm_ref, xt_ref, w_ref, b_ref, cache_ref, o_ref, kv_ref, q_ref, old_ref, *, cb):
    step = pl.program_id(0)
    is_prompt = step < NPC // cb

    @pl.when(step == 0)
    def _():
        old_ref[...] = jnp.zeros(old_ref.shape, old_ref.dtype)

    xb = jnp.where(is_prompt, xm_ref[...], xt_ref[...]).astype(BF16)
    y = jnp.dot(xb, w_ref[...], preferred_element_type=F32) + b_ref[...]
    q_ref[...] = y[:, :Q_DIM].astype(BF16)
    kv_ref[...] = y[:, Q_DIM:]

    band = 3 * CHUNK
    keys = 2 * LANES
    pairs = Q_DIM // LANES
    per_kv = pairs // N_KV_HEADS
    key = _iota((CHUNK, 2 * keys), 1) & (keys - 1)
    lo = _iota((keys, LANES), 1) < HEAD_DIM
    zero = jnp.zeros((), BF16)
    pad = jnp.zeros((keys - band, KVX_DIM), F32)

    def halves(tile):
        return jnp.concatenate([jnp.where(lo, tile, zero), jnp.where(lo, zero, tile)], axis=0)

    for t in range(cb):
        rows = slice(t * CHUNK, (t + 1) * CHUNK)
        if t >= 2:
            before_p = kv_ref[(t - 2) * CHUNK:t * CHUNK, :]
        elif t == 1:
            before_p = jnp.concatenate([old_ref[CHUNK:2 * CHUNK, :], kv_ref[0:CHUNK, :]], axis=0)
        else:
            before_p = old_ref[...]
        before = jnp.where(is_prompt, before_p, cache_ref[t * WINDOW:(t + 1) * WINDOW, :])
        kvb = jnp.concatenate([before, kv_ref[rows, :], pad], axis=0).astype(BF16)
        start = jnp.where(is_prompt, jnp.maximum(2 - (step * cb + t), 0) * CHUNK, 0)
        valid = (key >= start) & (key < band)
        k_ops = [halves(kvb[:, kh * LANES:(kh + 1) * LANES]) for kh in range(N_KV_HEADS)]
        v_ops = [halves(kvb[:, (N_KV_HEADS + kh) * LANES:(N_KV_HEADS + kh + 1) * LANES])
                 for kh in range(N_KV_HEADS)]
        scores = [lax.dot_general(q_ref[rows, i * LANES:(i + 1) * LANES], k_ops[i // per_kv],
                                  (((1,), (1,)), ((), ())), preferred_element_type=F32) for i in range(pairs)]
        probs = []
        for i in range(pairs):
            s = jnp.where(valid, scores[i] * (HEAD_DIM ** -0.5), NEG_INF)
            ps = []
            for hh in range(2):
                sh = s[:, hh * keys:(hh + 1) * keys]
                sink = sink_ref[2 * i + hh]
                m = jnp.maximum(jnp.max(sh, axis=-1, keepdims=True), sink)
                p = jnp.exp(sh - m)
                denom = jnp.sum(p, axis=-1, keepdims=True) + jnp.exp(sink - m)
                ps.append(p * (1.0 / denom))
            probs.append(jnp.concatenate(ps, axis=1).astype(BF16))
        for i in range(pairs):
            o = jnp.dot(probs[i], v_ops[i // per_kv], preferred_element_type=F32)
            o_ref[rows, i * LANES:(i + 1) * LANES] = o.astype(BF16)
    old_ref[...] = kv_ref[cb * CHUNK - WINDOW:cb * CHUNK, :]


def _attention(sinks, x_main, x_tail, w, b, cache):
    cb = TM // CHUNK
    assert cb == DEC_BATCH and TP % TM == 0, "one grid step must hold all sample streams"
    return pl.pallas_call(
        functools.partial(_attn_kernel, cb=cb),
        grid=(T // TM,),
        in_specs=[pl.BlockSpec(memory_space=pltpu.SMEM), *_main_tail((TM, D_MODEL), TP // TM, 0),
                  _full(w.shape), _full(b.shape), _full(cache.shape)],
        out_specs=[_rows(TM, Q_DIM), _rows(TM, KVX_DIM)],
        out_shape=[jax.ShapeDtypeStruct((T, Q_DIM), BF16), jax.ShapeDtypeStruct((T, KVX_DIM), F32)],
        scratch_shapes=[pltpu.VMEM((TM, Q_DIM), BF16), pltpu.VMEM((WINDOW, KVX_DIM), F32)],
        compiler_params=_params(),
        name="swa_attn",
    )(sinks, x_main, x_tail, w, b, cache)


FF_CHUNK = 256
FF_ROWS = 512


def _mix_ffn_kernel(a_ref, wo_ref, xm_ref, xt_ref, pm_ref, pt_ref, g0_ref, b0_ref,
                    wgu_ref, wd_ref, wg_ref, wp_ref, g1_ref, b1_ref, *rest, n_main, split_out):
    act_ref = rest[-1]
    is_main = pl.program_id(0) < n_main
    outs = []
    for r0 in range(0, TM, FF_ROWS):
        rows = slice(r0, r0 + FF_ROWS)
        y = jnp.dot(a_ref[rows, :], wo_ref[...], preferred_element_type=F32)
        x_res = jnp.where(is_main, xm_ref[rows, :], xt_ref[rows, :])
        x = _layer_norm(ALPHA * x_res + y, g0_ref[...], b0_ref[...])
        xb = x.astype(BF16)
        for c in range(D_FF // FF_CHUNK):
            lo = c * FF_CHUNK
            gate = jnp.dot(xb, wgu_ref[:, lo:lo + FF_CHUNK], preferred_element_type=F32)
            up = jnp.dot(xb, wgu_ref[:, D_FF + lo:D_FF + lo + FF_CHUNK], preferred_element_type=F32)
            act_ref[rows, lo:lo + FF_CHUNK] = (gate * _sigmoid(gate) * up).astype(BF16)
        f = jnp.dot(act_ref[rows, :], wd_ref[...], preferred_element_type=F32)
        pb = jnp.where(is_main, pm_ref[rows, :], pt_ref[rows, :]).astype(BF16)
        e = _sigmoid(jnp.dot(xb, wg_ref[...], preferred_element_type=F32)) * jnp.dot(
            pb, wp_ref[...], preferred_element_type=F32)
        out = _layer_norm(ALPHA * x + f + e, g1_ref[...], b1_ref[...])
        if split_out:
            outs.append(out)
        else:
            rest[0][rows, :] = out
    if split_out:
        om_ref, ot_ref = rest[:2]

        @pl.when(is_main)
        def _():
            for k, out in enumerate(outs):
                om_ref[k * FF_ROWS:(k + 1) * FF_ROWS, :] = out

        @pl.when(jnp.logical_not(is_main))
        def _():
            for k, out in enumerate(outs):
                ot_ref[k * FF_ROWS:(k + 1) * FF_ROWS, :] = out


def _layer_of(stacked, layer):
    nd = stacked.ndim - 1
    return pl.BlockSpec((None,) + stacked.shape[1:], lambda i: (layer,) + (0,) * nd, pipeline_mode=pl.Buffered(1))


def _mix_ffn(a, wo, x_main, x_tail, x_tail_block, p_main, p_tail, layer, ln0, weights, ln1, *, split_out):
    n_main = TP // TM
    in_specs = [_rows(TM, a.shape[1]), _full(wo.shape),
                *_main_tail((TM, D_MODEL), n_main, x_tail_block),
                pl.BlockSpec((None, None, TM, PLE_DIM), lambda i: (layer, 0, jnp.minimum(i, n_main - 1), 0)),
                pl.BlockSpec((None, TM, PLE_DIM), lambda i: (layer, 0, 0)),
                _full(ln0[0].shape), _full(ln0[1].shape), *[_layer_of(w, layer) for w in weights],
                _full(ln1[0].shape), _full(ln1[1].shape)]
    if split_out:
        out_specs = list(_main_tail((TM, D_MODEL), n_main, 0))
        out_shape = [jax.ShapeDtypeStruct((TP, D_MODEL), F32), jax.ShapeDtypeStruct((TS, D_MODEL), F32)]
    else:
        out_specs = _rows(TM, D_MODEL)
        out_shape = jax.ShapeDtypeStruct((T, D_MODEL), F32)
    return pl.pallas_call(
        functools.partial(_mix_ffn_kernel, n_main=n_main, split_out=split_out),
        grid=(T // TM,),
        in_specs=in_specs,
        out_specs=out_specs,
        out_shape=out_shape,
        scratch_shapes=[pltpu.VMEM((TM, D_FF), BF16)],
        compiler_params=_params(),
        name="mix_ffn",
    )(a, wo, x_main, x_tail, p_main, p_tail, *ln0, *weights, *ln1)


DN_COLS = 512
HIST = SUBLANES
TAPS_BACK = CONV_WIDTH - 1


def _dn_in_kernel(x_ref, st_ref, win_ref, cw_ref, alog_ref, dtb_ref, *rest, tm, steps_per_seq, n_alias):
    qn_ref, kn_ref, v_ref, z_ref, gates_ref, cst_ref, hist_ref, tmp_ref = rest[n_alias:]
    i = pl.program_id(0)

    @pl.when(i % steps_per_seq == 0)
    def _():
        hist_ref[...] = st_ref[0]

    xb = x_ref[...].astype(BF16)
    for c in range(CONV_DIM // DN_COLS):
        lo = c * DN_COLS
        cols = slice(lo, lo + DN_COLS)
        buf = tmp_ref.at[c % 2]
        mm = jnp.dot(xb, win_ref[:, cols], preferred_element_type=F32)
        buf[0, HIST:HIST + tm, :] = mm
        for j in range(1, CONV_WIDTH):
            buf[j, HIST:2 * HIST, :] = hist_ref[j - 1, :, cols]
            buf[j, HIST + j:HIST + j + tm, :] = mm
            hist_ref[j - 1, :, cols] = buf[j, HIST + tm:2 * HIST + tm, :]
        cw = cw_ref[:, cols]
        conv = cw[3:4] * buf[0, HIST:HIST + tm, :]
        for j in range(1, CONV_WIDTH):
            conv = conv + cw[3 - j:4 - j] * buf[j, HIST:HIST + tm, :]
        y = conv * _sigmoid(conv)
        if lo < 2 * DN_KEY_DIM:
            is_q = lo < DN_KEY_DIM
            dst = qn_ref if is_q else kn_ref
            base = lo if is_q else lo - DN_KEY_DIM
            for t in range(DN_COLS // DN_DK):
                yt = y[:, t * DN_DK:(t + 1) * DN_DK]
                n = yt * lax.rsqrt(jnp.sum(yt * yt, axis=-1, keepdims=True) + 1e-6)
                if is_q:
                    n = n * (DN_DK ** -0.5)
                dst[:, base + t * DN_DK:base + (t + 1) * DN_DK] = n.astype(BF16)
        else:
            v_ref[:, lo - 2 * DN_KEY_DIM:lo - 2 * DN_KEY_DIM + DN_COLS] = y.astype(BF16)
    cst_ref[0] = hist_ref[TAPS_BACK - 1]

    for c in range(DN_VAL_DIM // DN_COLS):
        lo = c * DN_COLS
        z_ref[:, lo:lo + DN_COLS] = jnp.dot(xb, win_ref[:, CONV_DIM + lo:CONV_DIM + lo + DN_COLS],
                                            preferred_element_type=F32).astype(BF16)

    n_gate = 2 * DN_V_HEADS
    w_ab = jnp.concatenate([win_ref[:, CONV_DIM + DN_VAL_DIM:CONV_DIM + DN_VAL_DIM + n_gate],
                            jnp.zeros((D_MODEL, LANES - n_gate), BF16)], axis=1)
    ab = jnp.dot(xb, w_ab, preferred_element_type=F32)
    sp_in = ab + dtb_ref[...]
    softplus = jnp.maximum(sp_in, 0.0) + jnp.log1p(jnp.exp(-jnp.abs(sp_in)))
    gval = -jnp.exp(alog_ref[...]) * softplus
    gates_ref[...] = jnp.where(_iota(ab.shape, 1) < DN_V_HEADS, gval, _sigmoid(ab))


def _dn_in(x, state, w_in, cw, alog, dtb, *, tm, row_block0, n_seq, prev=None):
    steps_per_seq = (TP // tm) if n_seq == 1 else 1
    n_steps = steps_per_seq * n_seq
    n_alias = 0 if prev is None else len(prev)
    row = lambda width: pl.BlockSpec((tm, width), lambda i: (row_block0 + i, 0))
    state_spec = pl.BlockSpec((1, TAPS_BACK, HIST, CONV_DIM), lambda i: (i // steps_per_seq, 0, 0, 0))
    cst_spec = pl.BlockSpec((1, HIST, CONV_DIM), lambda i: (i // steps_per_seq, 0, 0))
    in_specs = [row(D_MODEL), state_spec, _full(w_in.shape), _full(cw.shape), _full(alog.shape), _full(dtb.shape)]
    in_specs += [pl.BlockSpec(memory_space=pl.ANY)] * n_alias
    out_shape = [jax.ShapeDtypeStruct((T, DN_KEY_DIM), BF16), jax.ShapeDtypeStruct((T, DN_KEY_DIM), BF16),
                 jax.ShapeDtypeStruct((T, DN_VAL_DIM), BF16), jax.ShapeDtypeStruct((T, DN_VAL_DIM), BF16),
                 jax.ShapeDtypeStruct((T, LANES), F32),
                 jax.ShapeDtypeStruct((n_seq, HIST, CONV_DIM), F32)]
    out_specs = [row(DN_KEY_DIM), row(DN_KEY_DIM), row(DN_VAL_DIM), row(DN_VAL_DIM), row(LANES), cst_spec]
    n_in = 6
    args = (x, state, w_in, cw, alog, dtb) + (tuple(prev) if prev is not None else ())
    return pl.pallas_call(
        functools.partial(_dn_in_kernel, tm=tm, steps_per_seq=steps_per_seq, n_alias=n_alias),
        grid=(n_steps,),
        in_specs=in_specs,
        out_specs=out_specs,
        out_shape=out_shape,
        scratch_shapes=[pltpu.VMEM((TAPS_BACK, HIST, CONV_DIM), F32),
                        pltpu.VMEM((2, CONV_WIDTH, tm + 2 * HIST, DN_COLS), F32)],
        input_output_aliases={n_in + k: k for k in range(n_alias)},
        compiler_params=_params(),
        name="dn_in_prompt" if n_seq == 1 else "dn_in_sample",
    )(*args)


GROUP_HEADS = MXU_DIM // CHUNK


def _split3(x):
    hi = x.astype(BF16)
    r = x - hi.astype(F32)
    mid = r.astype(BF16)
    lo = (r - mid.astype(F32)).astype(BF16)
    return hi, mid, lo


def _unit_lower_inverses(lcats, eye_cat, bd01):
    c = lcats[0].shape[0]

    def mm(a, b):
        b_bd = jnp.concatenate([b.astype(BF16)] * GROUP_HEADS, axis=0) * bd01
        return jnp.dot(a.astype(BF16), b_bd, preferred_element_type=F32)

    ms = [mm(l, l) for l in lcats]
    ps = [eye_cat - l for l in lcats]
    power = 2
    while 2 * power < c:
        pms = [mm(jnp.concatenate([p, m], axis=0), m) for p, m in zip(ps, ms)]
        ps = [p + pm[:c] for p, pm in zip(ps, pms)]
        ms = [pm[c:] for pm in pms]
        power *= 2
    return [p + mm(p, m) for p, m in zip(ps, ms)]


def _gdn_wy_kernel(q_ref, k_ref, v_ref, gates_ref, u_ref, w_ref, qd_ref, kdt_ref, qkd_ref, egl_ref, *, cb):
    c = CHUNK
    grp = GROUP_HEADS * c
    row = _iota((c, LANES), 0)
    lane = _iota((c, LANES), 1)
    col = lane & (c - 1)
    lo = lane < c
    causal2 = row >= col
    strict2 = row > col
    tril = jnp.where(_iota((c, c), 0) >= _iota((c, c), 1), 1.0, 0.0).astype(BF16)
    eye_cat = jnp.where(_iota((c, grp), 0) == (_iota((c, grp), 1) & (c - 1)), 1.0, 0.0).astype(F32)
    shift = c.bit_length() - 1
    bd01 = jnp.where((_iota((grp, grp), 0) >> shift) == (_iota((grp, grp), 1) >> shift), 1.0, 0.0).astype(BF16)
    nt = (((1,), (1,)), ((), ()))

    lcats, rhs = [], []
    for cc in range(cb):
        rows = slice(cc * c, (cc + 1) * c)
        gates = gates_ref[rows, :]
        cum = sum(jnp.dot(tril, part, preferred_element_type=F32) for part in _split3(gates))
        cum_t = jnp.concatenate([cum, cum], axis=0).T
        egl_ref[cc * DN_V_HEADS:(cc + 1) * DN_V_HEADS, :] = jnp.broadcast_to(
            jnp.exp(cum_t[0:DN_V_HEADS, c - 1:c]), (DN_V_HEADS, LANES))
        lmats = []
        for j in range(DN_QK_HEADS):
            a, b = 2 * j, 2 * j + 1
            tile = slice(j * LANES, (j + 1) * LANES)
            kb = k_ref[rows, tile]
            qb = q_ref[rows, tile]
            kq = lax.dot_general(jnp.concatenate([kb, qb], axis=0), jnp.concatenate([kb, kb], axis=0), nt,
                                 preferred_element_type=F32)
            gb_lanes = [jnp.broadcast_to(cum[:, h:h + 1], (c, LANES)) for h in (a, b)]
            beta_lanes = [jnp.broadcast_to(gates[:, DN_V_HEADS + h:DN_V_HEADS + h + 1], (c, LANES)) for h in (a, b)]
            gbc = jnp.where(lo, gb_lanes[0], gb_lanes[1])
            gbr = jnp.where(lo, cum_t[a:a + 1, :], cum_t[b:b + 1, :])
            beta2 = jnp.where(lo, beta_lanes[0], beta_lanes[1])
            decay = jnp.where(causal2, jnp.exp(jnp.where(causal2, gbc - gbr, 0.0)), 0.0)
            lmats.append(jnp.where(strict2, kq[:c] * beta2 * decay, 0.0))
            qkd_ref[rows, tile] = (kq[c:] * decay).astype(BF16)
            kf = kb.astype(F32)
            qf = qb.astype(F32)
            kdec = []
            for h, gb, beta in zip((a, b), gb_lanes, beta_lanes):
                head = slice(h * DN_DV, (h + 1) * DN_DV)
                egb = jnp.exp(gb)
                vf = v_ref[rows, head].astype(F32)
                rhs.append(jnp.concatenate([vf * beta, kf * (beta * egb)], axis=1).astype(BF16))
                qd_ref[rows, head] = (qf * egb).astype(BF16)
                kdec.append(kf * jnp.exp(gb[c - 1:c, :] - gb))
            kdt_ref[cc * DN_DK:(cc + 1) * DN_DK, tile] = jnp.concatenate(kdec, axis=0).T.astype(BF16)
        lcats += [jnp.concatenate(lmats[2 * m:2 * m + 2], axis=1) for m in range(DN_V_HEADS // GROUP_HEADS)]

    tinvs = _unit_lower_inverses(lcats, eye_cat, bd01)
    zeros = jnp.zeros((c, 2 * DN_DV), BF16)
    for cc in range(cb):
        rows = slice(cc * c, (cc + 1) * c)
        for j in range(DN_QK_HEADS):
            a = 2 * j
            h0 = cc * DN_V_HEADS + a
            pair = slice(a * DN_DV, (a + 2) * DN_DV)
            tinv = tinvs[cc * (DN_V_HEADS // GROUP_HEADS) + j // 2][:, (j % 2) * LANES:(j % 2 + 1) * LANES]
            rhs_bd = jnp.concatenate([jnp.concatenate([rhs[h0], zeros], axis=1),
                                      jnp.concatenate([zeros, rhs[h0 + 1]], axis=1)], axis=0)
            sol = jnp.dot(tinv.astype(BF16), rhs_bd, preferred_element_type=F32)
            u_ref[rows, pair] = jnp.concatenate(
                [sol[:, 0:DN_DV], sol[:, 2 * DN_DV:3 * DN_DV]], axis=1).astype(BF16)
            w_ref[rows, pair] = jnp.concatenate(
                [sol[:, DN_DV:2 * DN_DV], sol[:, 3 * DN_DV:4 * DN_DV]], axis=1).astype(BF16)


def _gdn_wy(qn, kn, v, gates, *, chunk0, n_chunks):
    cb = WY_CHUNKS
    tm = cb * CHUNK
    blk0 = chunk0 // cb
    tok_in = lambda width: pl.BlockSpec((tm, width), lambda i: (blk0 + i, 0))
    tok = lambda width: jax.ShapeDtypeStruct((n_chunks * CHUNK, width), BF16)
    return pl.pallas_call(
        functools.partial(_gdn_wy_kernel, cb=cb),
        grid=(n_chunks // cb,),
        in_specs=[tok_in(DN_KEY_DIM), tok_in(DN_KEY_DIM), tok_in(DN_VAL_DIM), tok_in(LANES)],
        out_specs=[_rows(tm, DN_VAL_DIM), _rows(tm, DN_VAL_DIM), _rows(tm, DN_VAL_DIM),
                   _rows(cb * DN_DK, DN_V_HEADS * CHUNK), _rows(tm, DN_V_HEADS * CHUNK),
                   _rows(cb * DN_V_HEADS, LANES)],
        out_shape=[tok(DN_VAL_DIM), tok(DN_VAL_DIM), tok(DN_VAL_DIM),
                   jax.ShapeDtypeStruct((n_chunks * DN_DK, DN_V_HEADS * CHUNK), BF16), tok(DN_V_HEADS * CHUNK),
                   jax.ShapeDtypeStruct((n_chunks * DN_V_HEADS, LANES), F32)],
        compiler_params=_params(),
        name="gdn_wy",
    )(qn, kn, v, gates)


def _gdn_rec_body(u_ref, w_ref, qd_ref, kdt_ref, qkd_ref, egl_ref, z_ref, nw_ref, o_ref, s_ref, cb):
    c = CHUNK
    nw = nw_ref[...]
    zero_s = jnp.zeros((DN_DK, DN_DV), BF16)
    zero_v = jnp.zeros((c, DN_DV), BF16)
    for cc in range(cb):
        rows = slice(cc * c, (cc + 1) * c)
        egl = egl_ref[cc * DN_V_HEADS:(cc + 1) * DN_V_HEADS, :]
        v_new, o_inter = [], []
        for m in range(DN_V_HEADS // 2):
            a = 2 * m
            pair = slice(a * DN_DV, (a + 2) * DN_DV)
            s_bd = jnp.concatenate(
                [jnp.concatenate([s_ref[0, a].astype(BF16), zero_s], axis=1),
                 jnp.concatenate([zero_s, s_ref[0, a + 1].astype(BF16)], axis=1)], axis=0)
            r = jnp.dot(jnp.concatenate([w_ref[rows, pair], qd_ref[rows, pair]], axis=0), s_bd,
                        preferred_element_type=F32)
            vn = u_ref[rows, pair].astype(F32) - r[:c]
            v_new += [vn[:, :DN_DV].astype(BF16), vn[:, DN_DV:].astype(BF16)]
            o_inter += [r[c:, :DN_DV], r[c:, DN_DV:]]
        for n in range(DN_V_HEADS // GROUP_HEADS):
            heads = range(GROUP_HEADS * n, GROUP_HEADS * (n + 1))
            v_bd = jnp.concatenate(
                [jnp.concatenate([v_new[h] if t == i else zero_v for t in range(GROUP_HEADS)], axis=1)
                 for i, h in enumerate(heads)], axis=0)
            grp = slice(n * GROUP_HEADS * c, (n + 1) * GROUP_HEADS * c)
            r2 = jnp.dot(jnp.concatenate([qkd_ref[rows, grp], kdt_ref[cc * DN_DK:(cc + 1) * DN_DK, grp]], axis=0),
                         v_bd, preferred_element_type=F32)
            for i, h in enumerate(heads):
                head = slice(h * DN_DV, (h + 1) * DN_DV)
                blk = slice(i * DN_DV, (i + 1) * DN_DV)
                s_ref[0, h] = s_ref[0, h] * egl[h:h + 1, :] + r2[c:, blk]
                o = o_inter[h] + r2[:c, blk]
                o = o * lax.rsqrt(jnp.mean(o * o, axis=-1, keepdims=True) + RMS_EPS) * nw
                zf = z_ref[rows, head].astype(F32)
                o_ref[rows, head] = (o * (zf * _sigmoid(zf))).astype(BF16)


def _gdn_rec_kernel(u_ref, w_ref, qd_ref, kdt_ref, qkd_ref, egl_ref, z_ref, nw_ref, sinit_ref, alias_ref,
                    o_ref, s_ref):
    del alias_ref
    s_ref[...] = sinit_ref[...]
    _gdn_rec_body(u_ref, w_ref, qd_ref, kdt_ref, qkd_ref, egl_ref, z_ref, nw_ref, o_ref, s_ref, 1)


def _gdn_rec_sample(u, w, qd, kdt, qkd, egl, z, nw, s_init, o_prev):
    rows = lambda r, width: pl.BlockSpec((r, width), lambda i: (i, 0))
    tok = pl.BlockSpec((CHUNK, DN_VAL_DIM), lambda i: (NPC + i, 0))
    state_spec = pl.BlockSpec((1, DN_V_HEADS, DN_DK, DN_DV), lambda i: (i, 0, 0, 0))
    return pl.pallas_call(
        _gdn_rec_kernel,
        grid=(DEC_BATCH,),
        in_specs=[rows(CHUNK, DN_VAL_DIM), rows(CHUNK, DN_VAL_DIM), rows(CHUNK, DN_VAL_DIM),
                  rows(DN_DK, DN_V_HEADS * CHUNK), rows(CHUNK, DN_V_HEADS * CHUNK), rows(DN_V_HEADS, LANES),
                  tok, _full(nw.shape), state_spec, pl.BlockSpec(memory_space=pl.ANY)],
        out_specs=[tok, state_spec],
        out_shape=[jax.ShapeDtypeStruct((T, DN_VAL_DIM), BF16), jax.ShapeDtypeStruct(s_init.shape, F32)],
        input_output_aliases={9: 0},
        compiler_params=_params(),
        name="gdn_rec_sample",
    )(u, w, qd, kdt, qkd, egl, z, nw, s_init, o_prev)


def _gdn_prompt_kernel(q_ref, k_ref, v_ref, gates_ref, z_ref, nw_ref, sinit_ref, o_ref, s_ref,
                       u_s, w_s, qd_s, kdt_s, qkd_s, egl_s, *, cb):
    i = pl.program_id(0)
    cur = i % 2
    prev = 1 - cur

    @pl.when(i == 0)
    def _():
        for ref in (u_s, w_s, qd_s, kdt_s, qkd_s, egl_s):
            ref[1] = jnp.zeros(ref.shape[1:], ref.dtype)

    @pl.when(i <= 1)
    def _():
        s_ref[...] = sinit_ref[...]

    _gdn_wy_kernel(q_ref, k_ref, v_ref, gates_ref, u_s.at[cur], w_s.at[cur], qd_s.at[cur], kdt_s.at[cur],
                   qkd_s.at[cur], egl_s.at[cur], cb=cb)
    _gdn_rec_body(u_s.at[prev], w_s.at[prev], qd_s.at[prev], kdt_s.at[prev], qkd_s.at[prev], egl_s.at[prev],
                  z_ref, nw_ref, o_ref, s_ref, cb)


def _gdn_prompt(qn, kn, v, gates, z, nw, s_init):
    cb = WY_CHUNKS
    tm = cb * CHUNK
    nblk = NPC // cb
    ahead = lambda width: pl.BlockSpec((tm, width), lambda i: (jnp.minimum(i, nblk - 1), 0))
    behind = lambda width: pl.BlockSpec((tm, width), lambda i: (jnp.maximum(i - 1, 0), 0))
    state_spec = pl.BlockSpec((1, DN_V_HEADS, DN_DK, DN_DV), lambda i: (0, 0, 0, 0))
    slots = lambda rows_, width, dtype: pltpu.VMEM((2, rows_, width), dtype)
    return pl.pallas_call(
        functools.partial(_gdn_prompt_kernel, cb=cb),
        grid=(nblk + 1,),
        in_specs=[ahead(DN_KEY_DIM), ahead(DN_KEY_DIM), ahead(DN_VAL_DIM), ahead(LANES), behind(DN_VAL_DIM),
                  _full(nw.shape), state_spec],
        out_specs=[behind(DN_VAL_DIM), state_spec],
        out_shape=[jax.ShapeDtypeStruct((T, DN_VAL_DIM), BF16), jax.ShapeDtypeStruct(s_init.shape, F32)],
        scratch_shapes=[slots(tm, DN_VAL_DIM, BF16), slots(tm, DN_VAL_DIM, BF16), slots(tm, DN_VAL_DIM, BF16),
                        slots(cb * DN_DK, DN_V_HEADS * CHUNK, BF16), slots(tm, DN_V_HEADS * CHUNK, BF16),
                        slots(cb * DN_V_HEADS, LANES, F32)],
        compiler_params=_params(),
        name="gdn_prompt",
    )(qn, kn, v, gates, z, nw, s_init)


def _lane_row(vec):
    return jnp.pad(vec.astype(F32), (0, LANES - vec.shape[0])).reshape(1, LANES)


def _history_tiles(state):
    tiles = [jnp.pad(state[:, TAPS_BACK - j:], ((0, 0), (0, HIST - j), (0, 0))) for j in range(1, CONV_WIDTH)]
    return jnp.stack(tiles, axis=1)


def _kvx_columns(k, v):
    parts = [t[..., h, :] for t in (k, v) for h in range(N_KV_HEADS) for _ in range(2)]
    return jnp.concatenate(parts, axis=-1)


def kernel(x_prompt, x_sample, cache_k, cache_v, state_conv, state_ssm, p_prompt, p_sample, w_qkv, b_qkv, attn_sinks, w_attn_out, w_dn_in, dn_conv_w, dn_a_log, dn_dt_bias, dn_norm_w, w_dn_out, w_ffn_gu, w_ffn_down, w_ple_gate, w_ple_proj, ln_g, ln_b):
    assert TS == TM, "the sample tokens must form exactly one row block"
    xp = x_prompt.reshape(TP, D_MODEL)
    xs = x_sample.reshape(TS, D_MODEL)
    ps = p_sample.reshape(DEPTH, TS, PLE_DIM)
    row = lambda v: v.reshape(1, -1)

    channel_weights = (w_ffn_gu.astype(BF16), w_ffn_down.astype(BF16),
                       w_ple_gate.astype(BF16), w_ple_proj.astype(BF16))

    def ln(i, k):
        return row(ln_g[i, k]), row(ln_b[i, k])

    kv_heads = (N_KV_HEADS, HEAD_DIM)
    w_q, w_k, w_v = jnp.split(w_qkv[0], [Q_DIM, Q_DIM + N_KV_HEADS * HEAD_DIM], axis=1)
    b_q, b_k, b_v = jnp.split(b_qkv[0], [Q_DIM, Q_DIM + N_KV_HEADS * HEAD_DIM])
    w_x = jnp.concatenate([w_q, _kvx_columns(w_k.reshape((D_MODEL,) + kv_heads),
                                             w_v.reshape((D_MODEL,) + kv_heads))], axis=1).astype(BF16)
    b_x = jnp.concatenate([b_q, _kvx_columns(b_k.reshape(kv_heads), b_v.reshape(kv_heads))])
    cache = _kvx_columns(cache_k[0], cache_v[0]).reshape(DEC_BATCH * WINDOW, KVX_DIM)
    o, kv = _attention(attn_sinks[0], xp, xs, w_x, row(b_x), cache)
    x = _mix_ffn(o, w_attn_out[0].astype(BF16), xp, xs, 0, p_prompt, ps, 0, ln(0, 0), channel_weights,
                 ln(0, 1), split_out=False)

    def heads_of(rows_, first_tile):
        cols = [rows_[..., (first_tile + h) * LANES:(first_tile + h) * LANES + HEAD_DIM] for h in range(N_KV_HEADS)]
        return jnp.stack(cols, axis=-2)

    kv_new_s = kv[TP:].reshape(DEC_BATCH, DEC_SEQ, KVX_DIM)
    new_k_prompt = heads_of(kv[TP - WINDOW:TP], 0)[None, None]
    new_v_prompt = heads_of(kv[TP - WINDOW:TP], N_KV_HEADS)[None, None]
    new_k_sample = jnp.concatenate([cache_k[0][:, DEC_SEQ:], heads_of(kv_new_s, 0)], axis=1)[None]
    new_v_sample = jnp.concatenate([cache_v[0][:, DEC_SEQ:], heads_of(kv_new_s, N_KV_HEADS)], axis=1)[None]

    alog = _lane_row(dn_a_log[0])
    dtb = _lane_row(dn_dt_bias[0])
    st_prompt = jnp.zeros((1, TAPS_BACK, HIST, CONV_DIM), F32)
    st_sample = _history_tiles(state_conv[0])
    dn_w = (w_dn_in[0].astype(BF16), dn_conv_w[0], alog, dtb)
    *tok_p, cst_p = _dn_in(x, st_prompt, *dn_w, tm=TM, row_block0=0, n_seq=1)
    *tok, cst_s = _dn_in(x, st_sample, *dn_w, tm=DEC_SEQ, row_block0=NPC, n_seq=DEC_BATCH, prev=tok_p)
    qn, kn, v, z, gates = tok
    nw = row(dn_norm_w[0])
    s0_prompt = jnp.zeros((1,) + state_ssm.shape[2:], F32)
    o_p, s_p = _gdn_prompt(qn, kn, v, gates, z, nw, s0_prompt)
    wy_s = _gdn_wy(qn, kn, v, gates, chunk0=NPC, n_chunks=NCH - NPC)
    o, s_s = _gdn_rec_sample(*wy_s, z, nw, state_ssm[0], o_p)
    y_prompt, y_sample = _mix_ffn(o, w_dn_out[0].astype(BF16), x, x, TP // TM, p_prompt, ps, 1, ln(1, 0),
                                  channel_weights, ln(1, 1), split_out=True)

    return (y_prompt.reshape(1, TP, D_MODEL), y_sample.reshape(DEC_BATCH, DEC_SEQ, D_MODEL),
            new_k_prompt, new_v_prompt, new_k_sample, new_v_sample,
            cst_p[:, :TAPS_BACK][None], s_p[None], cst_s[:, :TAPS_BACK][None], s_s[None])
```

```python
import functools

import jax
import jax.numpy as jnp
from jax import lax
from jax.experimental import pallas as pl
from jax.experimental.pallas import tpu as pltpu

F32 = jnp.float32
BF16 = jnp.bfloat16

D_MODEL = 1024
SEQ = 16384
DEC_BATCH = 8
DEC_SEQ = 64
TP = SEQ
TS = DEC_BATCH * DEC_SEQ
T = TP + TS
CHUNK = 64
NPC = TP // CHUNK
NCH = T // CHUNK

N_HEADS = 16
N_KV_HEADS = 2
HEAD_DIM = 64
WINDOW = 128
Q_DIM = N_HEADS * HEAD_DIM
KVX_DIM = 2 * N_KV_HEADS * 2 * HEAD_DIM

DN_QK_HEADS = 8
DN_V_HEADS = 16
DN_DK = 128
DN_DV = 128
DN_KEY_DIM = DN_QK_HEADS * DN_DK
DN_VAL_DIM = DN_V_HEADS * DN_DV
CONV_DIM = 2 * DN_KEY_DIM + DN_VAL_DIM
CONV_WIDTH = 4

D_FF = 2816
PLE_DIM = 256
DEPTH = 2
ALPHA = (2 * DEPTH) ** 0.25
LN_EPS = 1e-5
RMS_EPS = 1e-6
NEG_INF = -1e30

LANES = 128
SUBLANES = 8
MXU_DIM = 256
TM = 512
WY_CHUNKS = 4
VMEM_LIMIT = 56 * 1024 * 1024


def _params(vmem=VMEM_LIMIT):
    return pltpu.CompilerParams(dimension_semantics=("arbitrary",), vmem_limit_bytes=vmem)


def _full(shape):
    nd = len(shape)
    return pl.BlockSpec(shape, lambda i: (0,) * nd, pipeline_mode=pl.Buffered(1))


def _rows(tm, width):
    return pl.BlockSpec((tm, width), lambda i: (i, 0))


def _sigmoid(x):
    return 1.0 / (1.0 + jnp.exp(-x))


def _layer_norm(y, g, b):
    mu = jnp.mean(y, axis=-1, keepdims=True)
    d = y - mu
    var = jnp.mean(d * d, axis=-1, keepdims=True)
    return d * lax.rsqrt(var + LN_EPS) * g + b


def _iota(shape, axis):
    return lax.broadcasted_iota(jnp.int32, shape, axis)


def _main_tail(block, n_main, tail_block):
    main = pl.BlockSpec(block, lambda i: (jnp.minimum(i, n_main - 1), 0))
    tail = pl.BlockSpec(block, lambda i: (tail_block, 0))
    return main, tail


def _attn_kernel(sink_ref, xm_ref, xt_ref, w_ref, b_ref, cache_ref, *rest, cb, n_cast):
    cast_in, (o_ref, kv_ref), cast_out = rest[:n_cast], rest[n_cast:n_cast + 2], rest[n_cast + 2:2 * n_cast + 2]
    q_ref, old_ref = rest[2 * n_cast + 2:]
    _attn_body(sink_ref, xm_ref, xt_ref, w_ref, b_ref, cache_ref, o_ref, kv_ref, q_ref, old_ref, cb)
    for src, dst in zip(cast_in, cast_out):
        dst[...] = src[...].astype(BF16)


def _attn_body(sink_ref, xm_ref, xt_ref, w_ref, b_ref, cache_ref, o_ref, kv_ref, q_ref, old_ref, cb):
    step = pl.program_id(0)
    is_prompt = step < NPC // cb

    @pl.when(step == 0)
    def _():
        old_ref[...] = jnp.zeros(old_ref.shape, old_ref.dtype)

    xb = jnp.where(is_prompt, xm_ref[...], xt_ref[...]).astype(BF16)
    y = jnp.dot(xb, w_ref[...], preferred_element_type=F32) + b_ref[...]
    q_ref[...] = y[:, :Q_DIM].astype(BF16)
    kv_ref[...] = y[:, Q_DIM:]

    band = 3 * CHUNK
    keys = 2 * LANES
    pairs = Q_DIM // LANES
    per_kv = pairs // N_KV_HEADS
    key = _iota((CHUNK, 2 * keys), 1) & (keys - 1)
    lo = _iota((keys, LANES), 1) < HEAD_DIM
    zero = jnp.zeros((), BF16)
    pad = jnp.zeros((keys - band, KVX_DIM), F32)

    def halves(tile):
        return jnp.concatenate([jnp.where(lo, tile, zero), jnp.where(lo, zero, tile)], axis=0)

    for t in range(cb):
        rows = slice(t * CHUNK, (t + 1) * CHUNK)
        if t >= 2:
            before_p = kv_ref[(t - 2) * CHUNK:t * CHUNK, :]
        elif t == 1:
            before_p = jnp.concatenate([old_ref[CHUNK:2 * CHUNK, :], kv_ref[0:CHUNK, :]], axis=0)
        else:
            before_p = old_ref[...]
        before = jnp.where(is_prompt, before_p, cache_ref[t * WINDOW:(t + 1) * WINDOW, :])
        kvb = jnp.concatenate([before, kv_ref[rows, :], pad], axis=0).astype(BF16)
        start = jnp.where(is_prompt, jnp.maximum(2 - (step * cb + t), 0) * CHUNK, 0)
        valid = (key >= start) & (key < band)
        k_ops = [halves(kvb[:, kh * LANES:(kh + 1) * LANES]) for kh in range(N_KV_HEADS)]
        v_ops = [halves(kvb[:, (N_KV_HEADS + kh) * LANES:(N_KV_HEADS + kh + 1) * LANES])
                 for kh in range(N_KV_HEADS)]
        scores = [lax.dot_general(q_ref[rows, i * LANES:(i + 1) * LANES], k_ops[i // per_kv],
                                  (((1,), (1,)), ((), ())), preferred_element_type=F32) for i in range(pairs)]
        probs = []
        for i in range(pairs):
            s = jnp.where(valid, scores[i] * (HEAD_DIM ** -0.5), NEG_INF)
            ps = []
            for hh in range(2):
                sh = s[:, hh * keys:(hh + 1) * keys]
                sink = sink_ref[2 * i + hh]
                m = jnp.maximum(jnp.max(sh, axis=-1, keepdims=True), sink)
                p = jnp.exp(sh - m)
                denom = jnp.sum(p, axis=-1, keepdims=True) + jnp.exp(sink - m)
                ps.append(p * (1.0 / denom))
            probs.append(jnp.concatenate(ps, axis=1).astype(BF16))
        for i in range(pairs):
            o = jnp.dot(probs[i], v_ops[i // per_kv], preferred_element_type=F32)
            o_ref[rows, i * LANES:(i + 1) * LANES] = o.astype(BF16)
    old_ref[...] = kv_ref[cb * CHUNK - WINDOW:cb * CHUNK, :]


def _attention(sinks, x_main, x_tail, w, b, cache, to_cast):
    cb = TM // CHUNK
    n_main = TP // TM
    assert cb == DEC_BATCH and TP % TM == 0, "one grid step must hold all sample streams"
    flat = [wt.reshape(-1, wt.shape[-1]) for wt in to_cast]
    slices = [pl.BlockSpec((wt.shape[0] // n_main, wt.shape[1]), lambda i: (jnp.minimum(i, n_main - 1), 0))
              for wt in flat]
    assert all(wt.shape[0] % (2 * SUBLANES * n_main) == 0 for wt in flat)
    o, kv, *cast = pl.pallas_call(
        functools.partial(_attn_kernel, cb=cb, n_cast=len(flat)),
        grid=(T // TM,),
        in_specs=[pl.BlockSpec(memory_space=pltpu.SMEM), *_main_tail((TM, D_MODEL), n_main, 0),
                  _full(w.shape), _full(b.shape), _full(cache.shape), *slices],
        out_specs=[_rows(TM, Q_DIM), _rows(TM, KVX_DIM), *slices],
        out_shape=[jax.ShapeDtypeStruct((T, Q_DIM), BF16), jax.ShapeDtypeStruct((T, KVX_DIM), F32),
                   *[jax.ShapeDtypeStruct(wt.shape, BF16) for wt in flat]],
        scratch_shapes=[pltpu.VMEM((TM, Q_DIM), BF16), pltpu.VMEM((WINDOW, KVX_DIM), F32)],
        compiler_params=_params(),
        name="swa_attn",
    )(sinks, x_main, x_tail, w, b, cache, *flat)
    return o, kv, [c.reshape(wt.shape) for c, wt in zip(cast, to_cast)]


FF_CHUNK = 256
FF_ROWS = 512


def _mix_ffn_kernel(a_ref, wo_ref, xm_ref, xt_ref, pm_ref, pt_ref, g0_ref, b0_ref,
                    wgu_ref, wd_ref, wg_ref, wp_ref, g1_ref, b1_ref, *rest, n_main, split_out):
    act_ref = rest[-1]
    is_main = pl.program_id(0) < n_main
    outs = []
    for r0 in range(0, TM, FF_ROWS):
        rows = slice(r0, r0 + FF_ROWS)
        y = jnp.dot(a_ref[rows, :], wo_ref[...], preferred_element_type=F32)
        x_res = jnp.where(is_main, xm_ref[rows, :], xt_ref[rows, :])
        x = _layer_norm(ALPHA * x_res + y, g0_ref[...], b0_ref[...])
        xb = x.astype(BF16)
        pb = jnp.where(is_main, pm_ref[rows, :], pt_ref[rows, :]).astype(BF16)
        e = _sigmoid(jnp.dot(xb, wg_ref[...], preferred_element_type=F32)) * jnp.dot(
            pb, wp_ref[...], preferred_element_type=F32)
        xe = ALPHA * x + e
        for c in range(D_FF // FF_CHUNK):
            lo = c * FF_CHUNK
            gate = jnp.dot(xb, wgu_ref[:, lo:lo + FF_CHUNK], preferred_element_type=F32)
            up = jnp.dot(xb, wgu_ref[:, D_FF + lo:D_FF + lo + FF_CHUNK], preferred_element_type=F32)
            act_ref[rows, lo:lo + FF_CHUNK] = (gate * _sigmoid(gate) * up).astype(BF16)
        f = jnp.dot(act_ref[rows, :], wd_ref[...], preferred_element_type=F32)
        out = _layer_norm(xe + f, g1_ref[...], b1_ref[...])
        if split_out:
            outs.append(out)
        else:
            rest[0][rows, :] = out
    if split_out:
        om_ref, ot_ref = rest[:2]

        @pl.when(is_main)
        def _():
            for k, out in enumerate(outs):
                om_ref[k * FF_ROWS:(k + 1) * FF_ROWS, :] = out

        @pl.when(jnp.logical_not(is_main))
        def _():
            for k, out in enumerate(outs):
                ot_ref[k * FF_ROWS:(k + 1) * FF_ROWS, :] = out


def _layer_of(stacked, layer):
    nd = stacked.ndim - 1
    return pl.BlockSpec((None,) + stacked.shape[1:], lambda i: (layer,) + (0,) * nd, pipeline_mode=pl.Buffered(1))


def _mix_ffn(a, wo, x_main, x_tail, x_tail_block, p_main, p_tail, layer, ln0, weights, ln1, *, split_out):
    n_main = TP // TM
    in_specs = [_rows(TM, a.shape[1]), _full(wo.shape),
                *_main_tail((TM, D_MODEL), n_main, x_tail_block),
                pl.BlockSpec((None, None, TM, PLE_DIM), lambda i: (layer, 0, jnp.minimum(i, n_main - 1), 0)),
                pl.BlockSpec((None, TM, PLE_DIM), lambda i: (layer, 0, 0)),
                _full(ln0[0].shape), _full(ln0[1].shape), *[_layer_of(w, layer) for w in weights],
                _full(ln1[0].shape), _full(ln1[1].shape)]
    if split_out:
        out_specs = list(_main_tail((TM, D_MODEL), n_main, 0))
        out_shape = [jax.ShapeDtypeStruct((TP, D_MODEL), F32), jax.ShapeDtypeStruct((TS, D_MODEL), F32)]
    else:
        out_specs = _rows(TM, D_MODEL)
        out_shape = jax.ShapeDtypeStruct((T, D_MODEL), F32)
    return pl.pallas_call(
        functools.partial(_mix_ffn_kernel, n_main=n_main, split_out=split_out),
        grid=(T // TM,),
        in_specs=in_specs,
        out_specs=out_specs,
        out_shape=out_shape,
        scratch_shapes=[pltpu.VMEM((TM, D_FF), BF16)],
        compiler_params=_params(),
        name="mix_ffn",
    )(a, wo, x_main, x_tail, p_main, p_tail, *ln0, *weights, *ln1)


DN_COLS = 512
HIST = SUBLANES
TAPS_BACK = CONV_WIDTH - 1


def _dn_in_kernel(x_ref, st_ref, win_ref, cw_ref, alog_ref, dtb_ref, *rest, tm, steps_per_seq, n_alias):
    qn_ref, kn_ref, v_ref, z_ref, gates_ref, cst_ref, hist_ref, tmp_ref = rest[n_alias:]
    i = pl.program_id(0)

    @pl.when(i % steps_per_seq == 0)
    def _():
        hist_ref[...] = st_ref[0]

    xb = x_ref[...].astype(BF16)
    for c in range(CONV_DIM // DN_COLS):
        lo = c * DN_COLS
        cols = slice(lo, lo + DN_COLS)
        buf = tmp_ref.at[c % 2]
        mm = jnp.dot(xb, win_ref[:, cols], preferred_element_type=F32)
        buf[0, HIST:HIST + tm, :] = mm
        for j in range(1, CONV_WIDTH):
            buf[j, HIST:2 * HIST, :] = hist_ref[j - 1, :, cols]
            buf[j, HIST + j:HIST + j + tm, :] = mm
            hist_ref[j - 1, :, cols] = buf[j, HIST + tm:2 * HIST + tm, :]
        cw = cw_ref[:, cols]
        conv = cw[3:4] * buf[0, HIST:HIST + tm, :]
        for j in range(1, CONV_WIDTH):
            conv = conv + cw[3 - j:4 - j] * buf[j, HIST:HIST + tm, :]
        y = conv * _sigmoid(conv)
        if lo < 2 * DN_KEY_DIM:
            is_q = lo < DN_KEY_DIM
            dst = qn_ref if is_q else kn_ref
            base = lo if is_q else lo - DN_KEY_DIM
            for t in range(DN_COLS // DN_DK):
                yt = y[:, t * DN_DK:(t + 1) * DN_DK]
                n = yt * lax.rsqrt(jnp.sum(yt * yt, axis=-1, keepdims=True) + 1e-6)
                if is_q:
                    n = n * (DN_DK ** -0.5)
                dst[:, base + t * DN_DK:base + (t + 1) * DN_DK] = n.astype(BF16)
        else:
            v_ref[:, lo - 2 * DN_KEY_DIM:lo - 2 * DN_KEY_DIM + DN_COLS] = y.astype(BF16)
    cst_ref[0] = hist_ref[TAPS_BACK - 1]

    for c in range(DN_VAL_DIM // DN_COLS):
        lo = c * DN_COLS
        z_ref[:, lo:lo + DN_COLS] = jnp.dot(xb, win_ref[:, CONV_DIM + lo:CONV_DIM + lo + DN_COLS],
                                            preferred_element_type=F32).astype(BF16)

    n_gate = 2 * DN_V_HEADS
    w_ab = jnp.concatenate([win_ref[:, CONV_DIM + DN_VAL_DIM:CONV_DIM + DN_VAL_DIM + n_gate],
                            jnp.zeros((D_MODEL, LANES - n_gate), BF16)], axis=1)
    ab = jnp.dot(xb, w_ab, preferred_element_type=F32)
    sp_in = ab + dtb_ref[...]
    softplus = jnp.maximum(sp_in, 0.0) + jnp.log1p(jnp.exp(-jnp.abs(sp_in)))
    gval = -jnp.exp(alog_ref[...]) * softplus
    gates_ref[...] = jnp.where(_iota(ab.shape, 1) < DN_V_HEADS, gval, _sigmoid(ab))


def _dn_in(x, state, w_in, cw, alog, dtb, *, tm, row_block0, n_seq, prev=None):
    steps_per_seq = (TP // tm) if n_seq == 1 else 1
    n_steps = steps_per_seq * n_seq
    n_alias = 0 if prev is None else len(prev)
    row = lambda width: pl.BlockSpec((tm, width), lambda i: (row_block0 + i, 0))
    state_spec = pl.BlockSpec((1, TAPS_BACK, HIST, CONV_DIM), lambda i: (i // steps_per_seq, 0, 0, 0))
    cst_spec = pl.BlockSpec((1, HIST, CONV_DIM), lambda i: (i // steps_per_seq, 0, 0))
    in_specs = [row(D_MODEL), state_spec, _full(w_in.shape), _full(cw.shape), _full(alog.shape), _full(dtb.shape)]
    in_specs += [pl.BlockSpec(memory_space=pl.ANY)] * n_alias
    out_shape = [jax.ShapeDtypeStruct((T, DN_KEY_DIM), BF16), jax.ShapeDtypeStruct((T, DN_KEY_DIM), BF16),
                 jax.ShapeDtypeStruct((T, DN_VAL_DIM), BF16), jax.ShapeDtypeStruct((T, DN_VAL_DIM), BF16),
                 jax.ShapeDtypeStruct((T, LANES), F32),
                 jax.ShapeDtypeStruct((n_seq, HIST, CONV_DIM), F32)]
    out_specs = [row(DN_KEY_DIM), row(DN_KEY_DIM), row(DN_VAL_DIM), row(DN_VAL_DIM), row(LANES), cst_spec]
    n_in = 6
    args = (x, state, w_in, cw, alog, dtb) + (tuple(prev) if prev is not None else ())
    return pl.pallas_call(
        functools.partial(_dn_in_kernel, tm=tm, steps_per_seq=steps_per_seq, n_alias=n_alias),
        grid=(n_steps,),
        in_specs=in_specs,
        out_specs=out_specs,
        out_shape=out_shape,
        scratch_shapes=[pltpu.VMEM((TAPS_BACK, HIST, CONV_DIM), F32),
                        pltpu.VMEM((2, CONV_WIDTH, tm + 2 * HIST, DN_COLS), F32)],
        input_output_aliases={n_in + k: k for k in range(n_alias)},
        compiler_params=_params(),
        name="dn_in_prompt" if n_seq == 1 else "dn_in_sample",
    )(*args)


GROUP_HEADS = MXU_DIM // CHUNK


def _split3(x):
    hi = x.astype(BF16)
    r = x - hi.astype(F32)
    mid = r.astype(BF16)
    lo = (r - mid.astype(F32)).astype(BF16)
    return hi, mid, lo


def _unit_lower_inverses(lcats, eye_cat, bd01):
    c = lcats[0].shape[0]

    def mm(a, b):
        b_bd = jnp.concatenate([b.astype(BF16)] * GROUP_HEADS, axis=0) * bd01
        return jnp.dot(a.astype(BF16), b_bd, preferred_element_type=F32)

    ms = [mm(l, l) for l in lcats]
    ps = [eye_cat - l for l in lcats]
    power = 2
    while 2 * power < c:
        pms = [mm(jnp.concatenate([p, m], axis=0), m) for p, m in zip(ps, ms)]
        ps = [p + pm[:c] for p, pm in zip(ps, pms)]
        ms = [pm[c:] for pm in pms]
        power *= 2
    return [p + mm(p, m) for p, m in zip(ps, ms)]


def _gdn_wy_kernel(q_ref, k_ref, v_ref, gates_ref, u_ref, w_ref, qd_ref, kdt_ref, qkd_ref, egl_ref, *, cb):
    c = CHUNK
    grp = GROUP_HEADS * c
    row = _iota((c, LANES), 0)
    lane = _iota((c, LANES), 1)
    col = lane & (c - 1)
    lo = lane < c
    causal2 = row >= col
    strict2 = row > col
    tril = jnp.where(_iota((c, c), 0) >= _iota((c, c), 1), 1.0, 0.0).astype(BF16)
    eye_cat = jnp.where(_iota((c, grp), 0) == (_iota((c, grp), 1) & (c - 1)), 1.0, 0.0).astype(F32)
    shift = c.bit_length() - 1
    bd01 = jnp.where((_iota((grp, grp), 0) >> shift) == (_iota((grp, grp), 1) >> shift), 1.0, 0.0).astype(BF16)
    nt = (((1,), (1,)), ((), ()))

    lcats, rhs = [], []
    for cc in range(cb):
        rows = slice(cc * c, (cc + 1) * c)
        gates = gates_ref[rows, :]
        cum = sum(jnp.dot(tril, part, preferred_element_type=F32) for part in _split3(gates))
        cum_t = jnp.concatenate([cum, cum], axis=0).T
        egl_ref[cc * DN_V_HEADS:(cc + 1) * DN_V_HEADS, :] = jnp.broadcast_to(
            jnp.exp(cum_t[0:DN_V_HEADS, c - 1:c]), (DN_V_HEADS, LANES))
        lmats = []
        for j in range(DN_QK_HEADS):
            a, b = 2 * j, 2 * j + 1
            tile = slice(j * LANES, (j + 1) * LANES)
            kb = k_ref[rows, tile]
            qb = q_ref[rows, tile]
            kq = lax.dot_general(jnp.concatenate([kb, qb], axis=0), jnp.concatenate([kb, kb], axis=0), nt,
                                 preferred_element_type=F32)
            gb_lanes = [jnp.broadcast_to(cum[:, h:h + 1], (c, LANES)) for h in (a, b)]
            beta_lanes = [jnp.broadcast_to(gates[:, DN_V_HEADS + h:DN_V_HEADS + h + 1], (c, LANES)) for h in (a, b)]
            gbc = jnp.where(lo, gb_lanes[0], gb_lanes[1])
            gbr = jnp.where(lo, cum_t[a:a + 1, :], cum_t[b:b + 1, :])
            beta2 = jnp.where(lo, beta_lanes[0], beta_lanes[1])
            decay = jnp.where(causal2, jnp.exp(jnp.where(causal2, gbc - gbr, 0.0)), 0.0)
            lmats.append(jnp.where(strict2, kq[:c] * beta2 * decay, 0.0))
            qkd_ref[rows, tile] = (kq[c:] * decay).astype(BF16)
            kf = kb.astype(F32)
            qf = qb.astype(F32)
            kdec = []
            for h, gb, beta in zip((a, b), gb_lanes, beta_lanes):
                head = slice(h * DN_DV, (h + 1) * DN_DV)
                egb = jnp.exp(gb)
                vf = v_ref[rows, head].astype(F32)
                rhs.append(jnp.concatenate([vf * beta, kf * (beta * egb)], axis=1).astype(BF16))
                qd_ref[rows, head] = (qf * egb).astype(BF16)
                kdec.append(kf * jnp.exp(gb[c - 1:c, :] - gb))
            kdt_ref[cc * DN_DK:(cc + 1) * DN_DK, tile] = jnp.concatenate(kdec, axis=0).T.astype(BF16)
        lcats += [jnp.concatenate(lmats[2 * m:2 * m + 2], axis=1) for m in range(DN_V_HEADS // GROUP_HEADS)]

    tinvs = _unit_lower_inverses(lcats, eye_cat, bd01)
    zeros = jnp.zeros((c, 2 * DN_DV), BF16)
    for cc in range(cb):
        rows = slice(cc * c, (cc + 1) * c)
        for j in range(DN_QK_HEADS):
            a = 2 * j
            h0 = cc * DN_V_HEADS + a
            pair = slice(a * DN_DV, (a + 2) * DN_DV)
            tinv = tinvs[cc * (DN_V_HEADS // GROUP_HEADS) + j // 2][:, (j % 2) * LANES:(j % 2 + 1) * LANES]
            rhs_bd = jnp.concatenate([jnp.concatenate([rhs[h0], zeros], axis=1),
                                      jnp.concatenate([zeros, rhs[h0 + 1]], axis=1)], axis=0)
            sol = jnp.dot(tinv.astype(BF16), rhs_bd, preferred_element_type=F32)
            u_ref[rows, pair] = jnp.concatenate(
                [sol[:, 0:DN_DV], sol[:, 2 * DN_DV:3 * DN_DV]], axis=1).astype(BF16)
            w_ref[rows, pair] = jnp.concatenate(
                [sol[:, DN_DV:2 * DN_DV], sol[:, 3 * DN_DV:4 * DN_DV]], axis=1).astype(BF16)


def _gdn_wy(qn, kn, v, gates, *, chunk0, n_chunks):
    cb = WY_CHUNKS
    tm = cb * CHUNK
    blk0 = chunk0 // cb
    tok_in = lambda width: pl.BlockSpec((tm, width), lambda i: (blk0 + i, 0))
    tok = lambda width: jax.ShapeDtypeStruct((n_chunks * CHUNK, width), BF16)
    return pl.pallas_call(
        functools.partial(_gdn_wy_kernel, cb=cb),
        grid=(n_chunks // cb,),
        in_specs=[tok_in(DN_KEY_DIM), tok_in(DN_KEY_DIM), tok_in(DN_VAL_DIM), tok_in(LANES)],
        out_specs=[_rows(tm, DN_VAL_DIM), _rows(tm, DN_VAL_DIM), _rows(tm, DN_VAL_DIM),
                   _rows(cb * DN_DK, DN_V_HEADS * CHUNK), _rows(tm, DN_V_HEADS * CHUNK),
                   _rows(cb * DN_V_HEADS, LANES)],
        out_shape=[tok(DN_VAL_DIM), tok(DN_VAL_DIM), tok(DN_VAL_DIM),
                   jax.ShapeDtypeStruct((n_chunks * DN_DK, DN_V_HEADS * CHUNK), BF16), tok(DN_V_HEADS * CHUNK),
                   jax.ShapeDtypeStruct((n_chunks * DN_V_HEADS, LANES), F32)],
        compiler_params=_params(),
        name="gdn_wy",
    )(qn, kn, v, gates)


def _gdn_rec_body(u_ref, w_ref, qd_ref, kdt_ref, qkd_ref, egl_ref, z_ref, nw_ref, o_ref, s_ref, cb):
    c = CHUNK
    nw = nw_ref[...]
    zero_s = jnp.zeros((DN_DK, DN_DV), BF16)
    zero_v = jnp.zeros((c, DN_DV), BF16)
    for cc in range(cb):
        rows = slice(cc * c, (cc + 1) * c)
        egl = egl_ref[cc * DN_V_HEADS:(cc + 1) * DN_V_HEADS, :]
        v_new, o_inter = [], []
        for m in range(DN_V_HEADS // 2):
            a = 2 * m
            pair = slice(a * DN_DV, (a + 2) * DN_DV)
            s_bd = jnp.concatenate(
                [jnp.concatenate([s_ref[0, a].astype(BF16), zero_s], axis=1),
                 jnp.concatenate([zero_s, s_ref[0, a + 1].astype(BF16)], axis=1)], axis=0)
            r = jnp.dot(jnp.concatenate([w_ref[rows, pair], qd_ref[rows, pair]], axis=0), s_bd,
                        preferred_element_type=F32)
            vn = u_ref[rows, pair].astype(F32) - r[:c]
            v_new += [vn[:, :DN_DV].astype(BF16), vn[:, DN_DV:].astype(BF16)]
            o_inter += [r[c:, :DN_DV], r[c:, DN_DV:]]
        for n in range(DN_V_HEADS // GROUP_HEADS):
            heads = range(GROUP_HEADS * n, GROUP_HEADS * (n + 1))
            v_bd = jnp.concatenate(
                [jnp.concatenate([v_new[h] if t == i else zero_v for t in range(GROUP_HEADS)], axis=1)
                 for i, h in enumerate(heads)], axis=0)
            grp = slice(n * GROUP_HEADS * c, (n + 1) * GROUP_HEADS * c)
            r2 = jnp.dot(jnp.concatenate([qkd_ref[rows, grp], kdt_ref[cc * DN_DK:(cc + 1) * DN_DK, grp]], axis=0),
                         v_bd, preferred_element_type=F32)
            for i, h in enumerate(heads):
                head = slice(h * DN_DV, (h + 1) * DN_DV)
                blk = slice(i * DN_DV, (i + 1) * DN_DV)
                s_ref[0, h] = s_ref[0, h] * egl[h:h + 1, :] + r2[c:, blk]
                o = o_inter[h] + r2[:c, blk]
                o = o * lax.rsqrt(jnp.mean(o * o, axis=-1, keepdims=True) + RMS_EPS) * nw
                zf = z_ref[rows, head].astype(F32)
                o_ref[rows, head] = (o * (zf * _sigmoid(zf))).astype(BF16)


def _gdn_rec_kernel(u_ref, w_ref, qd_ref, kdt_ref, qkd_ref, egl_ref, z_ref, nw_ref, sinit_ref, alias_ref,
                    o_ref, s_ref):
    del alias_ref
    s_ref[...] = sinit_ref[...]
    _gdn_rec_body(u_ref, w_ref, qd_ref, kdt_ref, qkd_ref, egl_ref, z_ref, nw_ref, o_ref, s_ref, 1)


def _gdn_rec_sample(u, w, qd, kdt, qkd, egl, z, nw, s_init, o_prev):
    rows = lambda r, width: pl.BlockSpec((r, width), lambda i: (i, 0))
    tok = pl.BlockSpec((CHUNK, DN_VAL_DIM), lambda i: (NPC + i, 0))
    state_spec = pl.BlockSpec((1, DN_V_HEADS, DN_DK, DN_DV), lambda i: (i, 0, 0, 0))
    return pl.pallas_call(
        _gdn_rec_kernel,
        grid=(DEC_BATCH,),
        in_specs=[rows(CHUNK, DN_VAL_DIM), rows(CHUNK, DN_VAL_DIM), rows(CHUNK, DN_VAL_DIM),
                  rows(DN_DK, DN_V_HEADS * CHUNK), rows(CHUNK, DN_V_HEADS * CHUNK), rows(DN_V_HEADS, LANES),
                  tok, _full(nw.shape), state_spec, pl.BlockSpec(memory_space=pl.ANY)],
        out_specs=[tok, state_spec],
        out_shape=[jax.ShapeDtypeStruct((T, DN_VAL_DIM), BF16), jax.ShapeDtypeStruct(s_init.shape, F32)],
        input_output_aliases={9: 0},
        compiler_params=_params(),
        name="gdn_rec_sample",
    )(u, w, qd, kdt, qkd, egl, z, nw, s_init, o_prev)


def _gdn_prompt_kernel(q_ref, k_ref, v_ref, gates_ref, z_ref, nw_ref, sinit_ref, o_ref, s_ref,
                       u_s, w_s, qd_s, kdt_s, qkd_s, egl_s, *, cb):
    i = pl.program_id(0)
    cur = i % 2
    prev = 1 - cur

    @pl.when(i == 0)
    def _():
        for ref in (u_s, w_s, qd_s, kdt_s, qkd_s, egl_s):
            ref[1] = jnp.zeros(ref.shape[1:], ref.dtype)

    @pl.when(i <= 1)
    def _():
        s_ref[...] = sinit_ref[...]

    _gdn_wy_kernel(q_ref, k_ref, v_ref, gates_ref, u_s.at[cur], w_s.at[cur], qd_s.at[cur], kdt_s.at[cur],
                   qkd_s.at[cur], egl_s.at[cur], cb=cb)
    _gdn_rec_body(u_s.at[prev], w_s.at[prev], qd_s.at[prev], kdt_s.at[prev], qkd_s.at[prev], egl_s.at[prev],
                  z_ref, nw_ref, o_ref, s_ref, cb)


def _gdn_prompt(qn, kn, v, gates, z, nw, s_init):
    cb = WY_CHUNKS
    tm = cb * CHUNK
    nblk = NPC // cb
    ahead = lambda width: pl.BlockSpec((tm, width), lambda i: (jnp.minimum(i, nblk - 1), 0))
    behind = lambda width: pl.BlockSpec((tm, width), lambda i: (jnp.maximum(i - 1, 0), 0))
    state_spec = pl.BlockSpec((1, DN_V_HEADS, DN_DK, DN_DV), lambda i: (0, 0, 0, 0))
    slots = lambda rows_, width, dtype: pltpu.VMEM((2, rows_, width), dtype)
    return pl.pallas_call(
        functools.partial(_gdn_prompt_kernel, cb=cb),
        grid=(nblk + 1,),
        in_specs=[ahead(DN_KEY_DIM), ahead(DN_KEY_DIM), ahead(DN_VAL_DIM), ahead(LANES), behind(DN_VAL_DIM),
                  _full(nw.shape), state_spec],
        out_specs=[behind(DN_VAL_DIM), state_spec],
        out_shape=[jax.ShapeDtypeStruct((T, DN_VAL_DIM), BF16), jax.ShapeDtypeStruct(s_init.shape, F32)],
        scratch_shapes=[slots(tm, DN_VAL_DIM, BF16), slots(tm, DN_VAL_DIM, BF16), slots(tm, DN_VAL_DIM, BF16),
                        slots(cb * DN_DK, DN_V_HEADS * CHUNK, BF16), slots(tm, DN_V_HEADS * CHUNK, BF16),
                        slots(cb * DN_V_HEADS, LANES, F32)],
        compiler_params=_params(),
        name="gdn_prompt",
    )(qn, kn, v, gates, z, nw, s_init)


def _lane_row(vec):
    return jnp.pad(vec.astype(F32), (0, LANES - vec.shape[0])).reshape(1, LANES)


def _history_tiles(state):
    tiles = [jnp.pad(state[:, TAPS_BACK - j:], ((0, 0), (0, HIST - j), (0, 0))) for j in range(1, CONV_WIDTH)]
    return jnp.stack(tiles, axis=1)


def _kvx_columns(k, v):
    parts = [t[..., h, :] for t in (k, v) for h in range(N_KV_HEADS) for _ in range(2)]
    return jnp.concatenate(parts, axis=-1)


def kernel(x_prompt, x_sample, cache_k, cache_v, state_conv, state_ssm, p_prompt, p_sample, w_qkv, b_qkv, attn_sinks, w_attn_out, w_dn_in, dn_conv_w, dn_a_log, dn_dt_bias, dn_norm_w, w_dn_out, w_ffn_gu, w_ffn_down, w_ple_gate, w_ple_proj, ln_g, ln_b):
    assert TS == TM, "the sample tokens must form exactly one row block"
    xp = x_prompt.reshape(TP, D_MODEL)
    xs = x_sample.reshape(TS, D_MODEL)
    ps = p_sample.reshape(DEPTH, TS, PLE_DIM)
    row = lambda v: v.reshape(1, -1)

    def ln(i, k):
        return row(ln_g[i, k]), row(ln_b[i, k])

    kv_heads = (N_KV_HEADS, HEAD_DIM)
    w_q, w_k, w_v = jnp.split(w_qkv[0], [Q_DIM, Q_DIM + N_KV_HEADS * HEAD_DIM], axis=1)
    b_q, b_k, b_v = jnp.split(b_qkv[0], [Q_DIM, Q_DIM + N_KV_HEADS * HEAD_DIM])
    w_x = jnp.concatenate([w_q, _kvx_columns(w_k.reshape((D_MODEL,) + kv_heads),
                                             w_v.reshape((D_MODEL,) + kv_heads))], axis=1).astype(BF16)
    b_x = jnp.concatenate([b_q, _kvx_columns(b_k.reshape(kv_heads), b_v.reshape(kv_heads))])
    cache = _kvx_columns(cache_k[0], cache_v[0]).reshape(DEC_BATCH * WINDOW, KVX_DIM)
    later_weights = (w_ffn_gu, w_ffn_down, w_ple_gate, w_ple_proj, w_attn_out[0], w_dn_out[0])
    o, kv, (*channel_weights, w_o_attn, w_o_dn) = _attention(
        attn_sinks[0], xp, xs, w_x, row(b_x), cache, later_weights)
    x = _mix_ffn(o, w_o_attn, xp, xs, 0, p_prompt, ps, 0, ln(0, 0), channel_weights, ln(0, 1), split_out=False)

    def heads_of(rows_, first_tile):
        cols = [rows_[..., (first_tile + h) * LANES:(first_tile + h) * LANES + HEAD_DIM] for h in range(N_KV_HEADS)]
        return jnp.stack(cols, axis=-2)

    kv_new_s = kv[TP:].reshape(DEC_BATCH, DEC_SEQ, KVX_DIM)
    new_k_prompt = heads_of(kv[TP - WINDOW:TP], 0)[None, None]
    new_v_prompt = heads_of(kv[TP - WINDOW:TP], N_KV_HEADS)[None, None]
    new_k_sample = jnp.concatenate([cache_k[0][:, DEC_SEQ:], heads_of(kv_new_s, 0)], axis=1)[None]
    new_v_sample = jnp.concatenate([cache_v[0][:, DEC_SEQ:], heads_of(kv_new_s, N_KV_HEADS)], axis=1)[None]

    alog = _lane_row(dn_a_log[0])
    dtb = _lane_row(dn_dt_bias[0])
    st_prompt = jnp.zeros((1, TAPS_BACK, HIST, CONV_DIM), F32)
    st_sample = _history_tiles(state_conv[0])
    dn_w = (w_dn_in[0].astype(BF16), dn_conv_w[0], alog, dtb)
    *tok_p, cst_p = _dn_in(x, st_prompt, *dn_w, tm=TM, row_block0=0, n_seq=1)
    *tok, cst_s = _dn_in(x, st_sample, *dn_w, tm=DEC_SEQ, row_block0=NPC, n_seq=DEC_BATCH, prev=tok_p)
    qn, kn, v, z, gates = tok
    nw = row(dn_norm_w[0])
    s0_prompt = jnp.zeros((1,) + state_ssm.shape[2:], F32)
    o_p, s_p = _gdn_prompt(qn, kn, v, gates, z, nw, s0_prompt)
    wy_s = _gdn_wy(qn, kn, v, gates, chunk0=NPC, n_chunks=NCH - NPC)
    o, s_s = _gdn_rec_sample(*wy_s, z, nw, state_ssm[0], o_p)
    y_prompt, y_sample = _mix_ffn(o, w_o_dn, x, x, TP // TM, p_prompt, ps, 1, ln(1, 0),
                                  channel_weights, ln(1, 1), split_out=True)

    return (y_prompt.reshape(1, TP, D_MODEL), y_sample.reshape(DEC_BATCH, DEC_SEQ, D_MODEL),
            new_k_prompt, new_v_prompt, new_k_sample, new_v_sample,
            cst_p[:, :TAPS_BACK][None], s_p[None], cst_s[:, :TAPS_BACK][None], s_s[None])
```

```python
import functools

import jax
import jax.numpy as jnp
from jax import lax
from jax.experimental import pallas as pl
from jax.experimental.pallas import tpu as pltpu

F32 = jnp.float32
BF16 = jnp.bfloat16

D_MODEL = 1024
SEQ = 16384
DEC_BATCH = 8
DEC_SEQ = 64
TP = SEQ
TS = DEC_BATCH * DEC_SEQ
T = TP + TS
CHUNK = 64
NPC = TP // CHUNK
NCH = T // CHUNK

N_HEADS = 16
N_KV_HEADS = 2
HEAD_DIM = 64
WINDOW = 128
Q_DIM = N_HEADS * HEAD_DIM
KVX_DIM = 2 * N_KV_HEADS * 2 * HEAD_DIM

DN_QK_HEADS = 8
DN_V_HEADS = 16
DN_DK = 128
DN_DV = 128
DN_KEY_DIM = DN_QK_HEADS * DN_DK
DN_VAL_DIM = DN_V_HEADS * DN_DV
CONV_DIM = 2 * DN_KEY_DIM + DN_VAL_DIM
CONV_WIDTH = 4

D_FF = 2816
PLE_DIM = 256
DEPTH = 2
ALPHA = (2 * DEPTH) ** 0.25
LN_EPS = 1e-5
RMS_EPS = 1e-6
NEG_INF = -1e30

LANES = 128
SUBLANES = 8
MXU_DIM = 256
TM = 512
WY_CHUNKS = 4
VMEM_LIMIT = 56 * 1024 * 1024


def _params(vmem=VMEM_LIMIT):
    return pltpu.CompilerParams(dimension_semantics=("arbitrary",), vmem_limit_bytes=vmem)


def _full(shape):
    nd = len(shape)
    return pl.BlockSpec(shape, lambda i: (0,) * nd, pipeline_mode=pl.Buffered(1))


def _rows(tm, width):
    return pl.BlockSpec((tm, width), lambda i: (i, 0))


def _sigmoid(x):
    return 1.0 / (1.0 + jnp.exp(-x))


def _layer_norm(y, g, b):
    mu = jnp.mean(y, axis=-1, keepdims=True)
    d = y - mu
    var = jnp.mean(d * d, axis=-1, keepdims=True)
    return d * lax.rsqrt(var + LN_EPS) * g + b


def _iota(shape, axis):
    return lax.broadcasted_iota(jnp.int32, shape, axis)


def _main_tail(block, n_main, tail_block):
    main = pl.BlockSpec(block, lambda i: (jnp.minimum(i, n_main - 1), 0))
    tail = pl.BlockSpec(block, lambda i: (tail_block, 0))
    return main, tail


def _attn_kernel(sink_ref, xm_ref, xt_ref, w_ref, b_ref, cache_ref, *rest, cb, n_cast):
    cast_in, (o_ref, kv_ref), cast_out = rest[:n_cast], rest[n_cast:n_cast + 2], rest[n_cast + 2:2 * n_cast + 2]
    q_ref, old_ref = rest[2 * n_cast + 2:]
    _attn_body(sink_ref, xm_ref, xt_ref, w_ref, b_ref, cache_ref, o_ref, kv_ref, q_ref, old_ref, cb)
    for src, dst in zip(cast_in, cast_out):
        dst[...] = src[...].astype(BF16)


def _attn_body(sink_ref, xm_ref, xt_ref, w_ref, b_ref, cache_ref, o_ref, kv_ref, q_ref, old_ref, cb):
    step = pl.program_id(0)
    is_prompt = step < NPC // cb

    @pl.when(step == 0)
    def _():
        old_ref[...] = jnp.zeros(old_ref.shape, old_ref.dtype)

    xb = jnp.where(is_prompt, xm_ref[...], xt_ref[...]).astype(BF16)
    y = jnp.dot(xb, w_ref[...], preferred_element_type=F32) + b_ref[...]
    q_ref[...] = y[:, :Q_DIM].astype(BF16)
    kv_ref[...] = y[:, Q_DIM:]

    band = 3 * CHUNK
    keys = 2 * LANES
    pairs = Q_DIM // LANES
    per_kv = pairs // N_KV_HEADS
    key = _iota((CHUNK, 2 * keys), 1) & (keys - 1)
    lo = _iota((keys, LANES), 1) < HEAD_DIM
    zero = jnp.zeros((), BF16)
    pad = jnp.zeros((keys - band, KVX_DIM), F32)

    def halves(tile):
        return jnp.concatenate([jnp.where(lo, tile, zero), jnp.where(lo, zero, tile)], axis=0)

    for t in range(cb):
        rows = slice(t * CHUNK, (t + 1) * CHUNK)
        if t >= 2:
            before_p = kv_ref[(t - 2) * CHUNK:t * CHUNK, :]
        elif t == 1:
            before_p = jnp.concatenate([old_ref[CHUNK:2 * CHUNK, :], kv_ref[0:CHUNK, :]], axis=0)
        else:
            before_p = old_ref[...]
        before = jnp.where(is_prompt, before_p, cache_ref[t * WINDOW:(t + 1) * WINDOW, :])
        kvb = jnp.concatenate([before, kv_ref[rows, :], pad], axis=0).astype(BF16)
        start = jnp.where(is_prompt, jnp.maximum(2 - (step * cb + t), 0) * CHUNK, 0)
        valid = (key >= start) & (key < band)
        k_ops = [halves(kvb[:, kh * LANES:(kh + 1) * LANES]) for kh in range(N_KV_HEADS)]
        v_ops = [halves(kvb[:, (N_KV_HEADS + kh) * LANES:(N_KV_HEADS + kh + 1) * LANES])
                 for kh in range(N_KV_HEADS)]
        scores = [lax.dot_general(q_ref[rows, i * LANES:(i + 1) * LANES], k_ops[i // per_kv],
                                  (((1,), (1,)), ((), ())), preferred_element_type=F32) for i in range(pairs)]
        probs = []
        for i in range(pairs):
            s = jnp.where(valid, scores[i] * (HEAD_DIM ** -0.5), NEG_INF)
            ps = []
            for hh in range(2):
                sh = s[:, hh * keys:(hh + 1) * keys]
                sink = sink_ref[2 * i + hh]
                m = jnp.maximum(jnp.max(sh, axis=-1, keepdims=True), sink)
                p = jnp.exp(sh - m)
                denom = jnp.sum(p, axis=-1, keepdims=True) + jnp.exp(sink - m)
                ps.append(p * (1.0 / denom))
            probs.append(jnp.concatenate(ps, axis=1).astype(BF16))
        for i in range(pairs):
            o = jnp.dot(probs[i], v_ops[i // per_kv], preferred_element_type=F32)
            o_ref[rows, i * LANES:(i + 1) * LANES] = o.astype(BF16)
    old_ref[...] = kv_ref[cb * CHUNK - WINDOW:cb * CHUNK, :]


def _attention(sinks, x_main, x_tail, w, b, cache, to_cast):
    cb = TM // CHUNK
    n_main = TP // TM
    assert cb == DEC_BATCH and TP % TM == 0, "one grid step must hold all sample streams"
    flat = [wt.reshape(-1, wt.shape[-1]) for wt in to_cast]
    slices = [pl.BlockSpec((wt.shape[0] // n_main, wt.shape[1]), lambda i: (jnp.minimum(i, n_main - 1), 0))
              for wt in flat]
    assert all(wt.shape[0] % (2 * SUBLANES * n_main) == 0 for wt in flat)
    o, kv, *cast = pl.pallas_call(
        functools.partial(_attn_kernel, cb=cb, n_cast=len(flat)),
        grid=(T // TM,),
        in_specs=[pl.BlockSpec(memory_space=pltpu.SMEM), *_main_tail((TM, D_MODEL), n_main, 0),
                  _full(w.shape), _full(b.shape), _full(cache.shape), *slices],
        out_specs=[_rows(TM, Q_DIM), _rows(TM, KVX_DIM), *slices],
        out_shape=[jax.ShapeDtypeStruct((T, Q_DIM), BF16), jax.ShapeDtypeStruct((T, KVX_DIM), F32),
                   *[jax.ShapeDtypeStruct(wt.shape, BF16) for wt in flat]],
        scratch_shapes=[pltpu.VMEM((TM, Q_DIM), BF16), pltpu.VMEM((WINDOW, KVX_DIM), F32)],
        compiler_params=_params(),
        name="swa_attn",
    )(sinks, x_main, x_tail, w, b, cache, *flat)
    return o, kv, [c.reshape(wt.shape) for c, wt in zip(cast, to_cast)]


FF_CHUNK = 256
FF_ROWS = 512


def _mix_ffn_kernel(am_ref, at_ref, wo_ref, xm_ref, xt_ref, pm_ref, pt_ref, g0_ref, b0_ref,
                    wgu_ref, wd_ref, wg_ref, wp_ref, g1_ref, b1_ref, *rest, n_main, split_out):
    act_ref = rest[-1]
    is_main = pl.program_id(0) < n_main
    outs = []
    for r0 in range(0, TM, FF_ROWS):
        rows = slice(r0, r0 + FF_ROWS)
        a = jnp.where(is_main, am_ref[rows, :], at_ref[rows, :])
        y = jnp.dot(a, wo_ref[...], preferred_element_type=F32)
        x_res = jnp.where(is_main, xm_ref[rows, :], xt_ref[rows, :])
        x = _layer_norm(ALPHA * x_res + y, g0_ref[...], b0_ref[...])
        xb = x.astype(BF16)
        pb = jnp.where(is_main, pm_ref[rows, :], pt_ref[rows, :]).astype(BF16)
        e = _sigmoid(jnp.dot(xb, wg_ref[...], preferred_element_type=F32)) * jnp.dot(
            pb, wp_ref[...], preferred_element_type=F32)
        xe = ALPHA * x + e
        for c in range(D_FF // FF_CHUNK):
            lo = c * FF_CHUNK
            gate = jnp.dot(xb, wgu_ref[:, lo:lo + FF_CHUNK], preferred_element_type=F32)
            up = jnp.dot(xb, wgu_ref[:, D_FF + lo:D_FF + lo + FF_CHUNK], preferred_element_type=F32)
            act_ref[rows, lo:lo + FF_CHUNK] = (gate * _sigmoid(gate) * up).astype(BF16)
        f = jnp.dot(act_ref[rows, :], wd_ref[...], preferred_element_type=F32)
        out = _layer_norm(xe + f, g1_ref[...], b1_ref[...])
        if split_out:
            outs.append(out)
        else:
            rest[0][rows, :] = out
    if split_out:
        om_ref, ot_ref = rest[:2]

        @pl.when(is_main)
        def _():
            for k, out in enumerate(outs):
                om_ref[k * FF_ROWS:(k + 1) * FF_ROWS, :] = out

        @pl.when(jnp.logical_not(is_main))
        def _():
            for k, out in enumerate(outs):
                ot_ref[k * FF_ROWS:(k + 1) * FF_ROWS, :] = out


def _layer_of(stacked, layer):
    nd = stacked.ndim - 1
    return pl.BlockSpec((None,) + stacked.shape[1:], lambda i: (layer,) + (0,) * nd, pipeline_mode=pl.Buffered(1))


def _mix_ffn(a_main, a_tail, a_tail_block, wo, x_main, x_tail, x_tail_block, p_main, p_tail, layer,
             ln0, weights, ln1, *, split_out):
    n_main = TP // TM
    in_specs = [*_main_tail((TM, a_main.shape[1]), n_main, a_tail_block), _full(wo.shape),
                *_main_tail((TM, D_MODEL), n_main, x_tail_block),
                pl.BlockSpec((None, None, TM, PLE_DIM), lambda i: (layer, 0, jnp.minimum(i, n_main - 1), 0)),
                pl.BlockSpec((None, TM, PLE_DIM), lambda i: (layer, 0, 0)),
                _full(ln0[0].shape), _full(ln0[1].shape), *[_layer_of(w, layer) for w in weights],
                _full(ln1[0].shape), _full(ln1[1].shape)]
    if split_out:
        out_specs = list(_main_tail((TM, D_MODEL), n_main, 0))
        out_shape = [jax.ShapeDtypeStruct((TP, D_MODEL), F32), jax.ShapeDtypeStruct((TS, D_MODEL), F32)]
    else:
        out_specs = _rows(TM, D_MODEL)
        out_shape = jax.ShapeDtypeStruct((T, D_MODEL), F32)
    return pl.pallas_call(
        functools.partial(_mix_ffn_kernel, n_main=n_main, split_out=split_out),
        grid=(T // TM,),
        in_specs=in_specs,
        out_specs=out_specs,
        out_shape=out_shape,
        scratch_shapes=[pltpu.VMEM((TM, D_FF), BF16)],
        compiler_params=_params(),
        name="mix_ffn",
    )(a_main, a_tail, wo, x_main, x_tail, p_main, p_tail, *ln0, *weights, *ln1)


DN_COLS = 512
HIST = SUBLANES
TAPS_BACK = CONV_WIDTH - 1


def _dn_in_kernel(x_ref, st_ref, win_ref, cw_ref, alog_ref, dtb_ref,
                  qn_ref, kn_ref, v_ref, z_ref, gates_ref, cst_ref, hist_ref, tmp_ref, *, tm, steps_per_seq):
    i = pl.program_id(0)

    @pl.when(i % steps_per_seq == 0)
    def _():
        hist_ref[...] = st_ref[0]
        tmp_ref[:, :, HIST + tm:, :] = jnp.zeros((2, CONV_WIDTH, HIST, DN_COLS), F32)

    xb = x_ref[...].astype(BF16)
    for c in range(CONV_DIM // DN_COLS):
        lo = c * DN_COLS
        cols = slice(lo, lo + DN_COLS)
        buf = tmp_ref.at[c % 2]
        mm = jnp.dot(xb, win_ref[:, cols], preferred_element_type=F32)
        buf[0, HIST:HIST + tm, :] = mm
        for j in range(1, CONV_WIDTH):
            buf[j, HIST:2 * HIST, :] = hist_ref[j - 1, :, cols]
            buf[j, HIST + j:HIST + j + tm, :] = mm
            hist_ref[j - 1, :, cols] = buf[j, HIST + tm:2 * HIST + tm, :]
        cw = cw_ref[:, cols]
        conv = cw[3:4] * buf[0, HIST:HIST + tm, :]
        for j in range(1, CONV_WIDTH):
            conv = conv + cw[3 - j:4 - j] * buf[j, HIST:HIST + tm, :]
        y = conv * _sigmoid(conv)
        if lo < 2 * DN_KEY_DIM:
            is_q = lo < DN_KEY_DIM
            dst = qn_ref if is_q else kn_ref
            base = lo if is_q else lo - DN_KEY_DIM
            for t in range(DN_COLS // DN_DK):
                yt = y[:, t * DN_DK:(t + 1) * DN_DK]
                n = yt * lax.rsqrt(jnp.sum(yt * yt, axis=-1, keepdims=True) + 1e-6)
                if is_q:
                    n = n * (DN_DK ** -0.5)
                dst[:, base + t * DN_DK:base + (t + 1) * DN_DK] = n.astype(BF16)
        else:
            v_ref[:, lo - 2 * DN_KEY_DIM:lo - 2 * DN_KEY_DIM + DN_COLS] = y.astype(BF16)
    cst_ref[0] = hist_ref[TAPS_BACK - 1]

    for c in range(DN_VAL_DIM // DN_COLS):
        lo = c * DN_COLS
        z_ref[:, lo:lo + DN_COLS] = jnp.dot(xb, win_ref[:, CONV_DIM + lo:CONV_DIM + lo + DN_COLS],
                                            preferred_element_type=F32).astype(BF16)

    n_gate = 2 * DN_V_HEADS
    w_ab = jnp.concatenate([win_ref[:, CONV_DIM + DN_VAL_DIM:CONV_DIM + DN_VAL_DIM + n_gate],
                            jnp.zeros((D_MODEL, LANES - n_gate), BF16)], axis=1)
    ab = jnp.dot(xb, w_ab, preferred_element_type=F32)
    sp_in = ab + dtb_ref[...]
    softplus = jnp.maximum(sp_in, 0.0) + jnp.log1p(jnp.exp(-jnp.abs(sp_in)))
    gval = -jnp.exp(alog_ref[...]) * softplus
    gates_ref[...] = jnp.where(_iota(ab.shape, 1) < DN_V_HEADS, gval, _sigmoid(ab))


def _dn_in(x, state, w_in, cw, alog, dtb, *, tm, row_block0, n_seq):
    steps_per_seq = (TP // tm) if n_seq == 1 else 1
    n_steps = steps_per_seq * n_seq
    n_rows = n_steps * tm
    state_spec = pl.BlockSpec((1, TAPS_BACK, HIST, CONV_DIM), lambda i: (i // steps_per_seq, 0, 0, 0))
    cst_spec = pl.BlockSpec((1, HIST, CONV_DIM), lambda i: (i // steps_per_seq, 0, 0))
    in_specs = [pl.BlockSpec((tm, D_MODEL), lambda i: (row_block0 + i, 0)), state_spec,
                _full(w_in.shape), _full(cw.shape), _full(alog.shape), _full(dtb.shape)]
    out_shape = [jax.ShapeDtypeStruct((n_rows, DN_KEY_DIM), BF16), jax.ShapeDtypeStruct((n_rows, DN_KEY_DIM), BF16),
                 jax.ShapeDtypeStruct((n_rows, DN_VAL_DIM), BF16), jax.ShapeDtypeStruct((n_rows, DN_VAL_DIM), BF16),
                 jax.ShapeDtypeStruct((n_rows, LANES), F32),
                 jax.ShapeDtypeStruct((n_seq, HIST, CONV_DIM), F32)]
    out_specs = [_rows(tm, DN_KEY_DIM), _rows(tm, DN_KEY_DIM), _rows(tm, DN_VAL_DIM), _rows(tm, DN_VAL_DIM),
                 _rows(tm, LANES), cst_spec]
    return pl.pallas_call(
        functools.partial(_dn_in_kernel, tm=tm, steps_per_seq=steps_per_seq),
        grid=(n_steps,),
        in_specs=in_specs,
        out_specs=out_specs,
        out_shape=out_shape,
        scratch_shapes=[pltpu.VMEM((TAPS_BACK, HIST, CONV_DIM), F32),
                        pltpu.VMEM((2, CONV_WIDTH, tm + 2 * HIST, DN_COLS), F32)],
        compiler_params=_params(),
        name="dn_in_prompt" if n_seq == 1 else "dn_in_sample",
    )(x, state, w_in, cw, alog, dtb)


GROUP_HEADS = MXU_DIM // CHUNK


def _split3(x):
    hi = x.astype(BF16)
    r = x - hi.astype(F32)
    mid = r.astype(BF16)
    lo = (r - mid.astype(F32)).astype(BF16)
    return hi, mid, lo


def _unit_lower_inverses(lcats, eye_cat, bd01):
    c = lcats[0].shape[0]

    def mm(a, b):
        b_bd = jnp.concatenate([b.astype(BF16)] * GROUP_HEADS, axis=0) * bd01
        return jnp.dot(a.astype(BF16), b_bd, preferred_element_type=F32)

    ms = [mm(l, l) for l in lcats]
    ps = [eye_cat - l for l in lcats]
    power = 2
    while 2 * power < c:
        pms = [mm(jnp.concatenate([p, m], axis=0), m) for p, m in zip(ps, ms)]
        ps = [p + pm[:c] for p, pm in zip(ps, pms)]
        ms = [pm[c:] for pm in pms]
        power *= 2
    return [p + mm(p, m) for p, m in zip(ps, ms)]


def _gdn_wy_kernel(q_ref, k_ref, v_ref, gates_ref, u_ref, w_ref, qd_ref, kdt_ref, qkd_ref, egl_ref, *, cb):
    c = CHUNK
    grp = GROUP_HEADS * c
    row = _iota((c, LANES), 0)
    lane = _iota((c, LANES), 1)
    col = lane & (c - 1)
    lo = lane < c
    causal2 = row >= col
    strict2 = row > col
    tril = jnp.where(_iota((c, c), 0) >= _iota((c, c), 1), 1.0, 0.0).astype(BF16)
    eye_cat = jnp.where(_iota((c, grp), 0) == (_iota((c, grp), 1) & (c - 1)), 1.0, 0.0).astype(F32)
    shift = c.bit_length() - 1
    bd01 = jnp.where((_iota((grp, grp), 0) >> shift) == (_iota((grp, grp), 1) >> shift), 1.0, 0.0).astype(BF16)
    nt = (((1,), (1,)), ((), ()))

    lcats, rhs = [], []
    for cc in range(cb):
        rows = slice(cc * c, (cc + 1) * c)
        gates = gates_ref[rows, :]
        cum = sum(jnp.dot(tril, part, preferred_element_type=F32) for part in _split3(gates))
        cum_t = jnp.concatenate([cum, cum], axis=0).T
        egl_ref[cc * DN_V_HEADS:(cc + 1) * DN_V_HEADS, :] = jnp.broadcast_to(
            jnp.exp(cum_t[0:DN_V_HEADS, c - 1:c]), (DN_V_HEADS, LANES))
        lmats = []
        for j in range(DN_QK_HEADS):
            a, b = 2 * j, 2 * j + 1
            tile = slice(j * LANES, (j + 1) * LANES)
            kb = k_ref[rows, tile]
            qb = q_ref[rows, tile]
            kq = lax.dot_general(jnp.concatenate([kb, qb], axis=0), jnp.concatenate([kb, kb], axis=0), nt,
                                 preferred_element_type=F32)
            gb_lanes = [jnp.broadcast_to(cum[:, h:h + 1], (c, LANES)) for h in (a, b)]
            beta_lanes = [jnp.broadcast_to(gates[:, DN_V_HEADS + h:DN_V_HEADS + h + 1], (c, LANES)) for h in (a, b)]
            gbc = jnp.where(lo, gb_lanes[0], gb_lanes[1])
            gbr = jnp.where(lo, cum_t[a:a + 1, :], cum_t[b:b + 1, :])
            beta2 = jnp.where(lo, beta_lanes[0], beta_lanes[1])
            decay = jnp.where(causal2, jnp.exp(jnp.where(causal2, gbc - gbr, 0.0)), 0.0)
            lmats.append(jnp.where(strict2, kq[:c] * beta2 * decay, 0.0))
            qkd_ref[rows, tile] = (kq[c:] * decay).astype(BF16)
            kf = kb.astype(F32)
            qf = qb.astype(F32)
            kdec = []
            for h, gb, beta in zip((a, b), gb_lanes, beta_lanes):
                head = slice(h * DN_DV, (h + 1) * DN_DV)
                egb = jnp.exp(gb)
                vf = v_ref[rows, head].astype(F32)
                rhs.append(jnp.concatenate([vf * beta, kf * (beta * egb)], axis=1).astype(BF16))
                qd_ref[rows, head] = (qf * egb).astype(BF16)
                kdec.append(kf * jnp.exp(gb[c - 1:c, :] - gb))
            kdt_ref[cc * DN_DK:(cc + 1) * DN_DK, tile] = jnp.concatenate(kdec, axis=0).T.astype(BF16)
        lcats += [jnp.concatenate(lmats[2 * m:2 * m + 2], axis=1) for m in range(DN_V_HEADS // GROUP_HEADS)]

    tinvs = _unit_lower_inverses(lcats, eye_cat, bd01)
    zeros = jnp.zeros((c, 2 * DN_DV), BF16)
    for cc in range(cb):
        rows = slice(cc * c, (cc + 1) * c)
        for j in range(DN_QK_HEADS):
            a = 2 * j
            h0 = cc * DN_V_HEADS + a
            pair = slice(a * DN_DV, (a + 2) * DN_DV)
            tinv = tinvs[cc * (DN_V_HEADS // GROUP_HEADS) + j // 2][:, (j % 2) * LANES:(j % 2 + 1) * LANES]
            rhs_bd = jnp.concatenate([jnp.concatenate([rhs[h0], zeros], axis=1),
                                      jnp.concatenate([zeros, rhs[h0 + 1]], axis=1)], axis=0)
            sol = jnp.dot(tinv.astype(BF16), rhs_bd, preferred_element_type=F32)
            u_ref[rows, pair] = jnp.concatenate(
                [sol[:, 0:DN_DV], sol[:, 2 * DN_DV:3 * DN_DV]], axis=1).astype(BF16)
            w_ref[rows, pair] = jnp.concatenate(
                [sol[:, DN_DV:2 * DN_DV], sol[:, 3 * DN_DV:4 * DN_DV]], axis=1).astype(BF16)


def _gdn_wy(qn, kn, v, gates, *, chunk0, n_chunks):
    cb = WY_CHUNKS
    tm = cb * CHUNK
    blk0 = chunk0 // cb
    tok_in = lambda width: pl.BlockSpec((tm, width), lambda i: (blk0 + i, 0))
    tok = lambda width: jax.ShapeDtypeStruct((n_chunks * CHUNK, width), BF16)
    return pl.pallas_call(
        functools.partial(_gdn_wy_kernel, cb=cb),
        grid=(n_chunks // cb,),
        in_specs=[tok_in(DN_KEY_DIM), tok_in(DN_KEY_DIM), tok_in(DN_VAL_DIM), tok_in(LANES)],
        out_specs=[_rows(tm, DN_VAL_DIM), _rows(tm, DN_VAL_DIM), _rows(tm, DN_VAL_DIM),
                   _rows(cb * DN_DK, DN_V_HEADS * CHUNK), _rows(tm, DN_V_HEADS * CHUNK),
                   _rows(cb * DN_V_HEADS, LANES)],
        out_shape=[tok(DN_VAL_DIM), tok(DN_VAL_DIM), tok(DN_VAL_DIM),
                   jax.ShapeDtypeStruct((n_chunks * DN_DK, DN_V_HEADS * CHUNK), BF16), tok(DN_V_HEADS * CHUNK),
                   jax.ShapeDtypeStruct((n_chunks * DN_V_HEADS, LANES), F32)],
        compiler_params=_params(),
        name="gdn_wy",
    )(qn, kn, v, gates)


def _gdn_rec_body(u_ref, w_ref, qd_ref, kdt_ref, qkd_ref, egl_ref, z_ref, nw_ref, o_ref, s_ref, cb):
    c = CHUNK
    nw = nw_ref[...]
    zero_s = jnp.zeros((DN_DK, DN_DV), BF16)
    zero_v = jnp.zeros((c, DN_DV), BF16)
    for cc in range(cb):
        rows = slice(cc * c, (cc + 1) * c)
        egl = egl_ref[cc * DN_V_HEADS:(cc + 1) * DN_V_HEADS, :]
        v_new, o_inter = [], []
        for m in range(DN_V_HEADS // 2):
            a = 2 * m
            pair = slice(a * DN_DV, (a + 2) * DN_DV)
            s_bd = jnp.concatenate(
                [jnp.concatenate([s_ref[0, a].astype(BF16), zero_s], axis=1),
                 jnp.concatenate([zero_s, s_ref[0, a + 1].astype(BF16)], axis=1)], axis=0)
            r = jnp.dot(jnp.concatenate([w_ref[rows, pair], qd_ref[rows, pair]], axis=0), s_bd,
                        preferred_element_type=F32)
            vn = u_ref[rows, pair].astype(F32) - r[:c]
            v_new += [vn[:, :DN_DV].astype(BF16), vn[:, DN_DV:].astype(BF16)]
            o_inter += [r[c:, :DN_DV], r[c:, DN_DV:]]
        for n in range(DN_V_HEADS // GROUP_HEADS):
            heads = range(GROUP_HEADS * n, GROUP_HEADS * (n + 1))
            v_bd = jnp.concatenate(
                [jnp.concatenate([v_new[h] if t == i else zero_v for t in range(GROUP_HEADS)], axis=1)
                 for i, h in enumerate(heads)], axis=0)
            grp = slice(n * GROUP_HEADS * c, (n + 1) * GROUP_HEADS * c)
            r2 = jnp.dot(jnp.concatenate([qkd_ref[rows, grp], kdt_ref[cc * DN_DK:(cc + 1) * DN_DK, grp]], axis=0),
                         v_bd, preferred_element_type=F32)
            for i, h in enumerate(heads):
                head = slice(h * DN_DV, (h + 1) * DN_DV)
                blk = slice(i * DN_DV, (i + 1) * DN_DV)
                s_ref[0, h] = s_ref[0, h] * egl[h:h + 1, :] + r2[c:, blk]
                o = o_inter[h] + r2[:c, blk]
                o = o * lax.rsqrt(jnp.mean(o * o, axis=-1, keepdims=True) + RMS_EPS) * nw
                zf = z_ref[rows, head].astype(F32)
                o_ref[rows, head] = (o * (zf * _sigmoid(zf))).astype(BF16)


def _gdn_rec_kernel(u_ref, w_ref, qd_ref, kdt_ref, qkd_ref, egl_ref, z_ref, nw_ref, sinit_ref, o_ref, s_ref):
    s_ref[...] = sinit_ref[...]
    _gdn_rec_body(u_ref, w_ref, qd_ref, kdt_ref, qkd_ref, egl_ref, z_ref, nw_ref, o_ref, s_ref, 1)


def _gdn_rec_sample(u, w, qd, kdt, qkd, egl, z, nw, s_init):
    state_spec = pl.BlockSpec((1, DN_V_HEADS, DN_DK, DN_DV), lambda i: (i, 0, 0, 0))
    return pl.pallas_call(
        _gdn_rec_kernel,
        grid=(DEC_BATCH,),
        in_specs=[_rows(CHUNK, DN_VAL_DIM), _rows(CHUNK, DN_VAL_DIM), _rows(CHUNK, DN_VAL_DIM),
                  _rows(DN_DK, DN_V_HEADS * CHUNK), _rows(CHUNK, DN_V_HEADS * CHUNK), _rows(DN_V_HEADS, LANES),
                  _rows(CHUNK, DN_VAL_DIM), _full(nw.shape), state_spec],
        out_specs=[_rows(CHUNK, DN_VAL_DIM), state_spec],
        out_shape=[jax.ShapeDtypeStruct((TS, DN_VAL_DIM), BF16), jax.ShapeDtypeStruct(s_init.shape, F32)],
        compiler_params=_params(),
        name="gdn_rec_sample",
    )(u, w, qd, kdt, qkd, egl, z, nw, s_init)


def _gdn_prompt_kernel(q_ref, k_ref, v_ref, gates_ref, z_ref, nw_ref, sinit_ref, o_ref, s_ref,
                       u_s, w_s, qd_s, kdt_s, qkd_s, egl_s, *, cb):
    i = pl.program_id(0)
    cur = i % 2
    prev = 1 - cur

    @pl.when(i == 0)
    def _():
        for ref in (u_s, w_s, qd_s, kdt_s, qkd_s, egl_s):
            ref[1] = jnp.zeros(ref.shape[1:], ref.dtype)

    @pl.when(i <= 1)
    def _():
        s_ref[...] = sinit_ref[...]

    _gdn_wy_kernel(q_ref, k_ref, v_ref, gates_ref, u_s.at[cur], w_s.at[cur], qd_s.at[cur], kdt_s.at[cur],
                   qkd_s.at[cur], egl_s.at[cur], cb=cb)
    _gdn_rec_body(u_s.at[prev], w_s.at[prev], qd_s.at[prev], kdt_s.at[prev], qkd_s.at[prev], egl_s.at[prev],
                  z_ref, nw_ref, o_ref, s_ref, cb)


def _gdn_prompt(qn, kn, v, gates, z, nw, s_init):
    cb = WY_CHUNKS
    tm = cb * CHUNK
    nblk = NPC // cb
    ahead = lambda width: pl.BlockSpec((tm, width), lambda i: (jnp.minimum(i, nblk - 1), 0))
    behind = lambda width: pl.BlockSpec((tm, width), lambda i: (jnp.maximum(i - 1, 0), 0))
    state_spec = pl.BlockSpec((1, DN_V_HEADS, DN_DK, DN_DV), lambda i: (0, 0, 0, 0))
    slots = lambda rows_, width, dtype: pltpu.VMEM((2, rows_, width), dtype)
    return pl.pallas_call(
        functools.partial(_gdn_prompt_kernel, cb=cb),
        grid=(nblk + 1,),
        in_specs=[ahead(DN_KEY_DIM), ahead(DN_KEY_DIM), ahead(DN_VAL_DIM), ahead(LANES), behind(DN_VAL_DIM),
                  _full(nw.shape), state_spec],
        out_specs=[behind(DN_VAL_DIM), state_spec],
        out_shape=[jax.ShapeDtypeStruct((TP, DN_VAL_DIM), BF16), jax.ShapeDtypeStruct(s_init.shape, F32)],
        scratch_shapes=[slots(tm, DN_VAL_DIM, BF16), slots(tm, DN_VAL_DIM, BF16), slots(tm, DN_VAL_DIM, BF16),
                        slots(cb * DN_DK, DN_V_HEADS * CHUNK, BF16), slots(tm, DN_V_HEADS * CHUNK, BF16),
                        slots(cb * DN_V_HEADS, LANES, F32)],
        compiler_params=_params(),
        name="gdn_prompt",
    )(qn, kn, v, gates, z, nw, s_init)


def _lane_row(vec):
    return jnp.pad(vec.astype(F32), (0, LANES - vec.shape[0])).reshape(1, LANES)


def _history_tiles(state):
    tiles = [jnp.pad(state[:, TAPS_BACK - j:], ((0, 0), (0, HIST - j), (0, 0))) for j in range(1, CONV_WIDTH)]
    return jnp.stack(tiles, axis=1)


def _kvx_columns(k, v):
    parts = [t[..., h, :] for t in (k, v) for h in range(N_KV_HEADS) for _ in range(2)]
    return jnp.concatenate(parts, axis=-1)


def kernel(x_prompt, x_sample, cache_k, cache_v, state_conv, state_ssm, p_prompt, p_sample, w_qkv, b_qkv, attn_sinks, w_attn_out, w_dn_in, dn_conv_w, dn_a_log, dn_dt_bias, dn_norm_w, w_dn_out, w_ffn_gu, w_ffn_down, w_ple_gate, w_ple_proj, ln_g, ln_b):
    assert TS == TM, "the sample tokens must form exactly one row block"
    xp = x_prompt.reshape(TP, D_MODEL)
    xs = x_sample.reshape(TS, D_MODEL)
    ps = p_sample.reshape(DEPTH, TS, PLE_DIM)
    row = lambda v: v.reshape(1, -1)

    def ln(i, k):
        return row(ln_g[i, k]), row(ln_b[i, k])

    kv_heads = (N_KV_HEADS, HEAD_DIM)
    w_q, w_k, w_v = jnp.split(w_qkv[0], [Q_DIM, Q_DIM + N_KV_HEADS * HEAD_DIM], axis=1)
    b_q, b_k, b_v = jnp.split(b_qkv[0], [Q_DIM, Q_DIM + N_KV_HEADS * HEAD_DIM])
    w_x = jnp.concatenate([w_q, _kvx_columns(w_k.reshape((D_MODEL,) + kv_heads),
                                             w_v.reshape((D_MODEL,) + kv_heads))], axis=1).astype(BF16)
    b_x = jnp.concatenate([b_q, _kvx_columns(b_k.reshape(kv_heads), b_v.reshape(kv_heads))])
    cache = _kvx_columns(cache_k[0], cache_v[0]).reshape(DEC_BATCH * WINDOW, KVX_DIM)
    later_weights = (w_ffn_gu, w_ffn_down, w_ple_gate, w_ple_proj, w_attn_out[0], w_dn_out[0])
    o, kv, (*channel_weights, w_o_attn, w_o_dn) = _attention(
        attn_sinks[0], xp, xs, w_x, row(b_x), cache, later_weights)
    x = _mix_ffn(o, o, TP // TM, w_o_attn, xp, xs, 0, p_prompt, ps, 0, ln(0, 0), channel_weights, ln(0, 1), split_out=False)

    def heads_of(rows_, first_tile):
        cols = [rows_[..., (first_tile + h) * LANES:(first_tile + h) * LANES + HEAD_DIM] for h in range(N_KV_HEADS)]
        return jnp.stack(cols, axis=-2)

    kv_new_s = kv[TP:].reshape(DEC_BATCH, DEC_SEQ, KVX_DIM)
    new_k_prompt = heads_of(kv[TP - WINDOW:TP], 0)[None, None]
    new_v_prompt = heads_of(kv[TP - WINDOW:TP], N_KV_HEADS)[None, None]
    new_k_sample = jnp.concatenate([cache_k[0][:, DEC_SEQ:], heads_of(kv_new_s, 0)], axis=1)[None]
    new_v_sample = jnp.concatenate([cache_v[0][:, DEC_SEQ:], heads_of(kv_new_s, N_KV_HEADS)], axis=1)[None]

    alog = _lane_row(dn_a_log[0])
    dtb = _lane_row(dn_dt_bias[0])
    st_prompt = jnp.zeros((1, TAPS_BACK, HIST, CONV_DIM), F32)
    st_sample = _history_tiles(state_conv[0])
    dn_w = (w_dn_in[0].astype(BF16), dn_conv_w[0], alog, dtb)
    *tok_p, cst_p = _dn_in(x, st_prompt, *dn_w, tm=TM, row_block0=0, n_seq=1)
    *tok_s, cst_s = _dn_in(x, st_sample, *dn_w, tm=DEC_SEQ, row_block0=NPC, n_seq=DEC_BATCH)
    nw = row(dn_norm_w[0])
    s0_prompt = jnp.zeros((1,) + state_ssm.shape[2:], F32)
    qn, kn, v, z, gates = tok_p
    o_p, s_p = _gdn_prompt(qn, kn, v, gates, z, nw, s0_prompt)
    qn, kn, v, z, gates = tok_s
    o_s, s_s = _gdn_rec_sample(*_gdn_wy(qn, kn, v, gates, chunk0=0, n_chunks=NCH - NPC), z, nw, state_ssm[0])
    y_prompt, y_sample = _mix_ffn(o_p, o_s, 0, w_o_dn, x, x, TP // TM, p_prompt, ps, 1, ln(1, 0),
                                  channel_weights, ln(1, 1), split_out=True)

    return (y_prompt.reshape(1, TP, D_MODEL), y_sample.reshape(DEC_BATCH, DEC_SEQ, D_MODEL),
            new_k_prompt, new_v_prompt, new_k_sample, new_v_sample,
            cst_p[:, :TAPS_BACK][None], s_p[None], cst_s[:, :TAPS_BACK][None], s_s[None])
```

```python
import functools

import jax
import jax.numpy as jnp
from jax import lax
from jax.experimental import pallas as pl
from jax.experimental.pallas import tpu as pltpu

F32 = jnp.float32
BF16 = jnp.bfloat16

D_MODEL = 1024
SEQ = 16384
DEC_BATCH = 8
DEC_SEQ = 64
TP = SEQ
TS = DEC_BATCH * DEC_SEQ
T = TP + TS
CHUNK = 64
NPC = TP // CHUNK

N_HEADS = 16
N_KV_HEADS = 2
HEAD_DIM = 64
WINDOW = 128
Q_DIM = N_HEADS * HEAD_DIM
KVX_DIM = 2 * N_KV_HEADS * 2 * HEAD_DIM

DN_QK_HEADS = 8
DN_V_HEADS = 16
DN_DK = 128
DN_DV = 128
DN_KEY_DIM = DN_QK_HEADS * DN_DK
DN_VAL_DIM = DN_V_HEADS * DN_DV
CONV_DIM = 2 * DN_KEY_DIM + DN_VAL_DIM
CONV_WIDTH = 4

D_FF = 2816
PLE_DIM = 256
DEPTH = 2
ALPHA = (2 * DEPTH) ** 0.25
LN_EPS = 1e-5
RMS_EPS = 1e-6
NEG_INF = -1e30

LANES = 128
SUBLANES = 8
MXU_DIM = 256
TM = 512
WY_CHUNKS = 4
VMEM_LIMIT = 56 * 1024 * 1024


def _params(vmem=VMEM_LIMIT):
    return pltpu.CompilerParams(dimension_semantics=("arbitrary",), vmem_limit_bytes=vmem)


def _full(shape):
    nd = len(shape)
    return pl.BlockSpec(shape, lambda i: (0,) * nd, pipeline_mode=pl.Buffered(1))


def _rows(tm, width):
    return pl.BlockSpec((tm, width), lambda i: (i, 0))


def _sigmoid(x):
    return 1.0 / (1.0 + jnp.exp(-x))


def _layer_norm(y, g, b):
    mu = jnp.mean(y, axis=-1, keepdims=True)
    d = y - mu
    var = jnp.mean(d * d, axis=-1, keepdims=True)
    return d * lax.rsqrt(var + LN_EPS) * g + b


def _iota(shape, axis):
    return lax.broadcasted_iota(jnp.int32, shape, axis)


def _main_tail(block, n_main, tail_block):
    main = pl.BlockSpec(block, lambda i: (jnp.minimum(i, n_main - 1), 0))
    tail = pl.BlockSpec(block, lambda i: (tail_block, 0))
    return main, tail


def _attn_kernel(sink_ref, xm_ref, xt_ref, w_ref, b_ref, cache_ref, *rest, cb, n_cast):
    cast_in, (o_ref, kv_ref), cast_out = rest[:n_cast], rest[n_cast:n_cast + 2], rest[n_cast + 2:2 * n_cast + 2]
    q_ref, old_ref = rest[2 * n_cast + 2:]
    _attn_body(sink_ref, xm_ref, xt_ref, w_ref, b_ref, cache_ref, o_ref, kv_ref, q_ref, old_ref, cb)
    for src, dst in zip(cast_in, cast_out):
        dst[...] = src[...].astype(BF16)


def _attn_body(sink_ref, xm_ref, xt_ref, w_ref, b_ref, cache_ref, o_ref, kv_ref, q_ref, old_ref, cb):
    step = pl.program_id(0)
    is_prompt = step < NPC // cb

    @pl.when(step == 0)
    def _():
        old_ref[...] = jnp.zeros(old_ref.shape, old_ref.dtype)

    xb = jnp.where(is_prompt, xm_ref[...], xt_ref[...]).astype(BF16)
    y = jnp.dot(xb, w_ref[...], preferred_element_type=F32) + b_ref[...]
    q_ref[...] = y[:, :Q_DIM].astype(BF16)
    kv_ref[...] = y[:, Q_DIM:]

    band = 3 * CHUNK
    keys = 2 * LANES
    pairs = Q_DIM // LANES
    per_kv = pairs // N_KV_HEADS
    key = _iota((CHUNK, 2 * keys), 1) & (keys - 1)
    lo = _iota((keys, LANES), 1) < HEAD_DIM
    zero = jnp.zeros((), BF16)
    pad = jnp.zeros((keys - band, KVX_DIM), F32)

    def halves(tile):
        return jnp.concatenate([jnp.where(lo, tile, zero), jnp.where(lo, zero, tile)], axis=0)

    for t in range(cb):
        rows = slice(t * CHUNK, (t + 1) * CHUNK)
        if t >= 2:
            before_p = kv_ref[(t - 2) * CHUNK:t * CHUNK, :]
        elif t == 1:
            before_p = jnp.concatenate([old_ref[CHUNK:2 * CHUNK, :], kv_ref[0:CHUNK, :]], axis=0)
        else:
            before_p = old_ref[...]
        before = jnp.where(is_prompt, before_p, cache_ref[t * WINDOW:(t + 1) * WINDOW, :])
        kvb = jnp.concatenate([before, kv_ref[rows, :], pad], axis=0).astype(BF16)
        start = jnp.where(is_prompt, jnp.maximum(2 - (step * cb + t), 0) * CHUNK, 0)
        valid = (key >= start) & (key < band)
        k_ops = [halves(kvb[:, kh * LANES:(kh + 1) * LANES]) for kh in range(N_KV_HEADS)]
        v_ops = [halves(kvb[:, (N_KV_HEADS + kh) * LANES:(N_KV_HEADS + kh + 1) * LANES])
                 for kh in range(N_KV_HEADS)]
        scores = [lax.dot_general(q_ref[rows, i * LANES:(i + 1) * LANES], k_ops[i // per_kv],
                                  (((1,), (1,)), ((), ())), preferred_element_type=F32) for i in range(pairs)]
        probs = []
        for i in range(pairs):
            s = jnp.where(valid, scores[i] * (HEAD_DIM ** -0.5), NEG_INF)
            ps = []
            for hh in range(2):
                sh = s[:, hh * keys:(hh + 1) * keys]
                sink = sink_ref[2 * i + hh]
                m = jnp.maximum(jnp.max(sh, axis=-1, keepdims=True), sink)
                p = jnp.exp(sh - m)
                denom = jnp.sum(p, axis=-1, keepdims=True) + jnp.exp(sink - m)
                ps.append(p * (1.0 / denom))
            probs.append(jnp.concatenate(ps, axis=1).astype(BF16))
        for i in range(pairs):
            o = jnp.dot(probs[i], v_ops[i // per_kv], preferred_element_type=F32)
            o_ref[rows, i * LANES:(i + 1) * LANES] = o.astype(BF16)
    old_ref[...] = kv_ref[cb * CHUNK - WINDOW:cb * CHUNK, :]


def _attention(sinks, x_main, x_tail, w, b, cache, to_cast):
    cb = TM // CHUNK
    n_main = TP // TM
    assert cb == DEC_BATCH and TP % TM == 0, "one grid step must hold all sample streams"
    flat = [wt.reshape(-1, wt.shape[-1]) for wt in to_cast]
    slices = [pl.BlockSpec((wt.shape[0] // n_main, wt.shape[1]), lambda i: (jnp.minimum(i, n_main - 1), 0))
              for wt in flat]
    assert all(wt.shape[0] % (2 * SUBLANES * n_main) == 0 for wt in flat)
    o, kv, *cast = pl.pallas_call(
        functools.partial(_attn_kernel, cb=cb, n_cast=len(flat)),
        grid=(T // TM,),
        in_specs=[pl.BlockSpec(memory_space=pltpu.SMEM), *_main_tail((TM, D_MODEL), n_main, 0),
                  _full(w.shape), _full(b.shape), _full(cache.shape), *slices],
        out_specs=[_rows(TM, Q_DIM), _rows(TM, KVX_DIM), *slices],
        out_shape=[jax.ShapeDtypeStruct((T, Q_DIM), BF16), jax.ShapeDtypeStruct((T, KVX_DIM), F32),
                   *[jax.ShapeDtypeStruct(wt.shape, BF16) for wt in flat]],
        scratch_shapes=[pltpu.VMEM((TM, Q_DIM), BF16), pltpu.VMEM((WINDOW, KVX_DIM), F32)],
        compiler_params=_params(),
        name="swa_attn",
    )(sinks, x_main, x_tail, w, b, cache, *flat)
    return o, kv, [c.reshape(wt.shape) for c, wt in zip(cast, to_cast)]


FF_CHUNK = 256
FF_ROWS = 512


def _mix_ffn_kernel(am_ref, at_ref, wo_ref, xm_ref, xt_ref, pm_ref, pt_ref, g0_ref, b0_ref,
                    wgu_ref, wd_ref, wg_ref, wp_ref, g1_ref, b1_ref, *rest, n_main, split_out):
    act_ref = rest[-1]
    is_main = pl.program_id(0) < n_main
    outs = []
    for r0 in range(0, TM, FF_ROWS):
        rows = slice(r0, r0 + FF_ROWS)
        a = jnp.where(is_main, am_ref[rows, :], at_ref[rows, :])
        y = jnp.dot(a, wo_ref[...], preferred_element_type=F32)
        x_res = jnp.where(is_main, xm_ref[rows, :], xt_ref[rows, :])
        x = _layer_norm(ALPHA * x_res + y, g0_ref[...], b0_ref[...])
        xb = x.astype(BF16)
        pb = jnp.where(is_main, pm_ref[rows, :], pt_ref[rows, :]).astype(BF16)
        e = _sigmoid(jnp.dot(xb, wg_ref[...], preferred_element_type=F32)) * jnp.dot(
            pb, wp_ref[...], preferred_element_type=F32)
        xe = ALPHA * x + e
        for c in range(D_FF // FF_CHUNK):
            lo = c * FF_CHUNK
            gate = jnp.dot(xb, wgu_ref[:, lo:lo + FF_CHUNK], preferred_element_type=F32)
            up = jnp.dot(xb, wgu_ref[:, D_FF + lo:D_FF + lo + FF_CHUNK], preferred_element_type=F32)
            act_ref[rows, lo:lo + FF_CHUNK] = (gate * _sigmoid(gate) * up).astype(BF16)
        f = jnp.dot(act_ref[rows, :], wd_ref[...], preferred_element_type=F32)
        out = _layer_norm(xe + f, g1_ref[...], b1_ref[...])
        if split_out:
            outs.append(out)
        else:
            rest[0][rows, :] = out
    if split_out:
        om_ref, ot_ref = rest[:2]

        @pl.when(is_main)
        def _():
            for k, out in enumerate(outs):
                om_ref[k * FF_ROWS:(k + 1) * FF_ROWS, :] = out

        @pl.when(jnp.logical_not(is_main))
        def _():
            for k, out in enumerate(outs):
                ot_ref[k * FF_ROWS:(k + 1) * FF_ROWS, :] = out


def _layer_of(stacked, layer):
    nd = stacked.ndim - 1
    return pl.BlockSpec((None,) + stacked.shape[1:], lambda i: (layer,) + (0,) * nd, pipeline_mode=pl.Buffered(1))


def _mix_ffn(a_main, a_tail, a_tail_block, wo, x_main, x_tail, x_tail_block, p_main, p_tail, layer,
             ln0, weights, ln1, *, split_out):
    n_main = TP // TM
    in_specs = [*_main_tail((TM, a_main.shape[1]), n_main, a_tail_block), _full(wo.shape),
                *_main_tail((TM, D_MODEL), n_main, x_tail_block),
                pl.BlockSpec((None, None, TM, PLE_DIM), lambda i: (layer, 0, jnp.minimum(i, n_main - 1), 0)),
                pl.BlockSpec((None, TM, PLE_DIM), lambda i: (layer, 0, 0)),
                _full(ln0[0].shape), _full(ln0[1].shape), *[_layer_of(w, layer) for w in weights],
                _full(ln1[0].shape), _full(ln1[1].shape)]
    if split_out:
        out_specs = list(_main_tail((TM, D_MODEL), n_main, 0))
        out_shape = [jax.ShapeDtypeStruct((TP, D_MODEL), F32), jax.ShapeDtypeStruct((TS, D_MODEL), F32)]
    else:
        out_specs = _rows(TM, D_MODEL)
        out_shape = jax.ShapeDtypeStruct((T, D_MODEL), F32)
    return pl.pallas_call(
        functools.partial(_mix_ffn_kernel, n_main=n_main, split_out=split_out),
        grid=(T // TM,),
        in_specs=in_specs,
        out_specs=out_specs,
        out_shape=out_shape,
        scratch_shapes=[pltpu.VMEM((TM, D_FF), BF16)],
        compiler_params=_params(),
        name="mix_ffn",
    )(a_main, a_tail, wo, x_main, x_tail, p_main, p_tail, *ln0, *weights, *ln1)


DN_COLS = 512
HIST = SUBLANES
TAPS_BACK = CONV_WIDTH - 1


def _dn_in_kernel(x_ref, st_ref, win_ref, cw_ref, alog_ref, dtb_ref,
                  qn_ref, kn_ref, v_ref, z_ref, gates_ref, cst_ref, hist_ref, tmp_ref, *, tm, steps_per_seq):
    i = pl.program_id(0)

    @pl.when(i % steps_per_seq == 0)
    def _():
        hist_ref[...] = st_ref[0]
        tmp_ref[:, :, HIST + tm:, :] = jnp.zeros((2, CONV_WIDTH, HIST, DN_COLS), F32)

    xb = x_ref[...].astype(BF16)
    for c in range(CONV_DIM // DN_COLS):
        lo = c * DN_COLS
        cols = slice(lo, lo + DN_COLS)
        buf = tmp_ref.at[c % 2]
        mm = jnp.dot(xb, win_ref[:, cols], preferred_element_type=F32)
        buf[0, HIST:HIST + tm, :] = mm
        for j in range(1, CONV_WIDTH):
            buf[j, HIST:2 * HIST, :] = hist_ref[j - 1, :, cols]
            buf[j, HIST + j:HIST + j + tm, :] = mm
            hist_ref[j - 1, :, cols] = buf[j, HIST + tm:2 * HIST + tm, :]
        cw = cw_ref[:, cols]
        conv = cw[3:4] * buf[0, HIST:HIST + tm, :]
        for j in range(1, CONV_WIDTH):
            conv = conv + cw[3 - j:4 - j] * buf[j, HIST:HIST + tm, :]
        y = conv * _sigmoid(conv)
        if lo < 2 * DN_KEY_DIM:
            is_q = lo < DN_KEY_DIM
            dst = qn_ref if is_q else kn_ref
            base = lo if is_q else lo - DN_KEY_DIM
            for t in range(DN_COLS // DN_DK):
                yt = y[:, t * DN_DK:(t + 1) * DN_DK]
                n = yt * lax.rsqrt(jnp.sum(yt * yt, axis=-1, keepdims=True) + 1e-6)
                if is_q:
                    n = n * (DN_DK ** -0.5)
                dst[:, base + t * DN_DK:base + (t + 1) * DN_DK] = n.astype(BF16)
        else:
            v_ref[:, lo - 2 * DN_KEY_DIM:lo - 2 * DN_KEY_DIM + DN_COLS] = y.astype(BF16)
    cst_ref[0] = hist_ref[TAPS_BACK - 1]

    for c in range(DN_VAL_DIM // DN_COLS):
        lo = c * DN_COLS
        z_ref[:, lo:lo + DN_COLS] = jnp.dot(xb, win_ref[:, CONV_DIM + lo:CONV_DIM + lo + DN_COLS],
                                            preferred_element_type=F32).astype(BF16)

    n_gate = 2 * DN_V_HEADS
    w_ab = jnp.concatenate([win_ref[:, CONV_DIM + DN_VAL_DIM:CONV_DIM + DN_VAL_DIM + n_gate],
                            jnp.zeros((D_MODEL, LANES - n_gate), BF16)], axis=1)
    ab = jnp.dot(xb, w_ab, preferred_element_type=F32)
    sp_in = ab + dtb_ref[...]
    softplus = jnp.maximum(sp_in, 0.0) + jnp.log1p(jnp.exp(-jnp.abs(sp_in)))
    gval = -jnp.exp(alog_ref[...]) * softplus
    gates_ref[...] = jnp.where(_iota(ab.shape, 1) < DN_V_HEADS, gval, _sigmoid(ab))


def _dn_in(x, state, w_in, cw, alog, dtb, *, tm, row_block0, n_seq):
    steps_per_seq = (TP // tm) if n_seq == 1 else 1
    n_steps = steps_per_seq * n_seq
    n_rows = n_steps * tm
    state_spec = pl.BlockSpec((1, TAPS_BACK, HIST, CONV_DIM), lambda i: (i // steps_per_seq, 0, 0, 0))
    cst_spec = pl.BlockSpec((1, HIST, CONV_DIM), lambda i: (i // steps_per_seq, 0, 0))
    in_specs = [pl.BlockSpec((tm, D_MODEL), lambda i: (row_block0 + i, 0)), state_spec,
                _full(w_in.shape), _full(cw.shape), _full(alog.shape), _full(dtb.shape)]
    out_shape = [jax.ShapeDtypeStruct((n_rows, DN_KEY_DIM), BF16), jax.ShapeDtypeStruct((n_rows, DN_KEY_DIM), BF16),
                 jax.ShapeDtypeStruct((n_rows, DN_VAL_DIM), BF16), jax.ShapeDtypeStruct((n_rows, DN_VAL_DIM), BF16),
                 jax.ShapeDtypeStruct((n_rows, LANES), F32),
                 jax.ShapeDtypeStruct((n_seq, HIST, CONV_DIM), F32)]
    out_specs = [_rows(tm, DN_KEY_DIM), _rows(tm, DN_KEY_DIM), _rows(tm, DN_VAL_DIM), _rows(tm, DN_VAL_DIM),
                 _rows(tm, LANES), cst_spec]
    return pl.pallas_call(
        functools.partial(_dn_in_kernel, tm=tm, steps_per_seq=steps_per_seq),
        grid=(n_steps,),
        in_specs=in_specs,
        out_specs=out_specs,
        out_shape=out_shape,
        scratch_shapes=[pltpu.VMEM((TAPS_BACK, HIST, CONV_DIM), F32),
                        pltpu.VMEM((2, CONV_WIDTH, tm + 2 * HIST, DN_COLS), F32)],
        compiler_params=_params(),
        name="dn_in_prompt" if n_seq == 1 else "dn_in_sample",
    )(x, state, w_in, cw, alog, dtb)


GROUP_HEADS = MXU_DIM // CHUNK


def _split3(x):
    hi = x.astype(BF16)
    r = x - hi.astype(F32)
    mid = r.astype(BF16)
    lo = (r - mid.astype(F32)).astype(BF16)
    return hi, mid, lo


def _unit_lower_inverses(lcats, eye_cat, bd01):
    c = lcats[0].shape[0]

    def mm(a, b):
        b_bd = jnp.concatenate([b.astype(BF16)] * GROUP_HEADS, axis=0) * bd01
        return jnp.dot(a.astype(BF16), b_bd, preferred_element_type=F32)

    ms = [mm(l, l) for l in lcats]
    ps = [eye_cat - l for l in lcats]
    power = 2
    while 2 * power < c:
        pms = [mm(jnp.concatenate([p, m], axis=0), m) for p, m in zip(ps, ms)]
        ps = [p + pm[:c] for p, pm in zip(ps, pms)]
        ms = [pm[c:] for pm in pms]
        power *= 2
    return [p + mm(p, m) for p, m in zip(ps, ms)]


def _gdn_wy_kernel(q_ref, k_ref, v_ref, gates_ref, u_ref, w_ref, qd_ref, kdt_ref, qkd_ref, egl_ref, *, cb):
    c = CHUNK
    grp = GROUP_HEADS * c
    row = _iota((c, LANES), 0)
    lane = _iota((c, LANES), 1)
    col = lane & (c - 1)
    lo = lane < c
    causal2 = row >= col
    strict2 = row > col
    tril = jnp.where(_iota((c, c), 0) >= _iota((c, c), 1), 1.0, 0.0).astype(BF16)
    eye_cat = jnp.where(_iota((c, grp), 0) == (_iota((c, grp), 1) & (c - 1)), 1.0, 0.0).astype(F32)
    shift = c.bit_length() - 1
    bd01 = jnp.where((_iota((grp, grp), 0) >> shift) == (_iota((grp, grp), 1) >> shift), 1.0, 0.0).astype(BF16)
    nt = (((1,), (1,)), ((), ()))

    lcats, rhs = [], []
    for cc in range(cb):
        rows = slice(cc * c, (cc + 1) * c)
        gates = gates_ref[rows, :]
        cum = sum(jnp.dot(tril, part, preferred_element_type=F32) for part in _split3(gates))
        cum_t = jnp.concatenate([cum, cum], axis=0).T
        egl_ref[cc * DN_V_HEADS:(cc + 1) * DN_V_HEADS, :] = jnp.broadcast_to(
            jnp.exp(cum_t[0:DN_V_HEADS, c - 1:c]), (DN_V_HEADS, LANES))
        lmats = []
        for j in range(DN_QK_HEADS):
            a, b = 2 * j, 2 * j + 1
            tile = slice(j * LANES, (j + 1) * LANES)
            kb = k_ref[rows, tile]
            qb = q_ref[rows, tile]
            kq = lax.dot_general(jnp.concatenate([kb, qb], axis=0), jnp.concatenate([kb, kb], axis=0), nt,
                                 preferred_element_type=F32)
            gb_lanes = [jnp.broadcast_to(cum[:, h:h + 1], (c, LANES)) for h in (a, b)]
            beta_lanes = [jnp.broadcast_to(gates[:, DN_V_HEADS + h:DN_V_HEADS + h + 1], (c, LANES)) for h in (a, b)]
            gbc = jnp.where(lo, gb_lanes[0], gb_lanes[1])
            gbr = jnp.where(lo, cum_t[a:a + 1, :], cum_t[b:b + 1, :])
            beta2 = jnp.where(lo, beta_lanes[0], beta_lanes[1])
            decay = jnp.where(causal2, jnp.exp(jnp.where(causal2, gbc - gbr, 0.0)), 0.0)
            lmats.append(jnp.where(strict2, kq[:c] * beta2 * decay, 0.0))
            qkd_ref[rows, tile] = (kq[c:] * decay).astype(BF16)
            kf = kb.astype(F32)
            qf = qb.astype(F32)
            kdec = []
            for h, gb, beta in zip((a, b), gb_lanes, beta_lanes):
                head = slice(h * DN_DV, (h + 1) * DN_DV)
                egb = jnp.exp(gb)
                vf = v_ref[rows, head].astype(F32)
                rhs.append(jnp.concatenate([vf * beta, kf * (beta * egb)], axis=1).astype(BF16))
                qd_ref[rows, head] = (qf * egb).astype(BF16)
                kdec.append(kf * jnp.exp(gb[c - 1:c, :] - gb))
            kdt_ref[cc * DN_DK:(cc + 1) * DN_DK, tile] = jnp.concatenate(kdec, axis=0).T.astype(BF16)
        lcats += [jnp.concatenate(lmats[2 * m:2 * m + 2], axis=1) for m in range(DN_V_HEADS // GROUP_HEADS)]

    tinvs = _unit_lower_inverses(lcats, eye_cat, bd01)
    zeros = jnp.zeros((c, 2 * DN_DV), BF16)
    for cc in range(cb):
        rows = slice(cc * c, (cc + 1) * c)
        for j in range(DN_QK_HEADS):
            a = 2 * j
            h0 = cc * DN_V_HEADS + a
            pair = slice(a * DN_DV, (a + 2) * DN_DV)
            tinv = tinvs[cc * (DN_V_HEADS // GROUP_HEADS) + j // 2][:, (j % 2) * LANES:(j % 2 + 1) * LANES]
            rhs_bd = jnp.concatenate([jnp.concatenate([rhs[h0], zeros], axis=1),
                                      jnp.concatenate([zeros, rhs[h0 + 1]], axis=1)], axis=0)
            sol = jnp.dot(tinv.astype(BF16), rhs_bd, preferred_element_type=F32)
            u_ref[rows, pair] = jnp.concatenate(
                [sol[:, 0:DN_DV], sol[:, 2 * DN_DV:3 * DN_DV]], axis=1).astype(BF16)
            w_ref[rows, pair] = jnp.concatenate(
                [sol[:, DN_DV:2 * DN_DV], sol[:, 3 * DN_DV:4 * DN_DV]], axis=1).astype(BF16)


def _gdn_rec_body(u_ref, w_ref, qd_ref, kdt_ref, qkd_ref, egl_ref, z_ref, nw_ref, o_ref, s_ref, cb):
    c = CHUNK
    nw = nw_ref[...]
    zero_s = jnp.zeros((DN_DK, DN_DV), BF16)
    zero_v = jnp.zeros((c, DN_DV), BF16)
    for cc in range(cb):
        rows = slice(cc * c, (cc + 1) * c)
        egl = egl_ref[cc * DN_V_HEADS:(cc + 1) * DN_V_HEADS, :]
        v_new, o_inter = [], []
        for m in range(DN_V_HEADS // 2):
            a = 2 * m
            pair = slice(a * DN_DV, (a + 2) * DN_DV)
            s_bd = jnp.concatenate(
                [jnp.concatenate([s_ref[0, a].astype(BF16), zero_s], axis=1),
                 jnp.concatenate([zero_s, s_ref[0, a + 1].astype(BF16)], axis=1)], axis=0)
            r = jnp.dot(jnp.concatenate([w_ref[rows, pair], qd_ref[rows, pair]], axis=0), s_bd,
                        preferred_element_type=F32)
            vn = u_ref[rows, pair].astype(F32) - r[:c]
            v_new += [vn[:, :DN_DV].astype(BF16), vn[:, DN_DV:].astype(BF16)]
            o_inter += [r[c:, :DN_DV], r[c:, DN_DV:]]
        for n in range(DN_V_HEADS // GROUP_HEADS):
            heads = range(GROUP_HEADS * n, GROUP_HEADS * (n + 1))
            v_bd = jnp.concatenate(
                [jnp.concatenate([v_new[h] if t == i else zero_v for t in range(GROUP_HEADS)], axis=1)
                 for i, h in enumerate(heads)], axis=0)
            grp = slice(n * GROUP_HEADS * c, (n + 1) * GROUP_HEADS * c)
            r2 = jnp.dot(jnp.concatenate([qkd_ref[rows, grp], kdt_ref[cc * DN_DK:(cc + 1) * DN_DK, grp]], axis=0),
                         v_bd, preferred_element_type=F32)
            for i, h in enumerate(heads):
                head = slice(h * DN_DV, (h + 1) * DN_DV)
                blk = slice(i * DN_DV, (i + 1) * DN_DV)
                s_ref[0, h] = s_ref[0, h] * egl[h:h + 1, :] + r2[c:, blk]
                o = o_inter[h] + r2[:c, blk]
                o = o * lax.rsqrt(jnp.mean(o * o, axis=-1, keepdims=True) + RMS_EPS) * nw
                zf = z_ref[rows, head].astype(F32)
                o_ref[rows, head] = (o * (zf * _sigmoid(zf))).astype(BF16)


def _gdn_sample_kernel(q_ref, k_ref, v_ref, gates_ref, z_ref, nw_ref, sinit_ref, o_ref, s_ref,
                       u_s, w_s, qd_s, kdt_s, qkd_s, egl_s):
    s_ref[...] = sinit_ref[...]
    _gdn_wy_kernel(q_ref, k_ref, v_ref, gates_ref, u_s, w_s, qd_s, kdt_s, qkd_s, egl_s, cb=1)
    _gdn_rec_body(u_s, w_s, qd_s, kdt_s, qkd_s, egl_s, z_ref, nw_ref, o_ref, s_ref, 1)


def _gdn_sample(qn, kn, v, gates, z, nw, s_init):
    state_spec = pl.BlockSpec((1, DN_V_HEADS, DN_DK, DN_DV), lambda i: (i, 0, 0, 0))
    return pl.pallas_call(
        _gdn_sample_kernel,
        grid=(DEC_BATCH,),
        in_specs=[_rows(CHUNK, DN_KEY_DIM), _rows(CHUNK, DN_KEY_DIM), _rows(CHUNK, DN_VAL_DIM), _rows(CHUNK, LANES),
                  _rows(CHUNK, DN_VAL_DIM), _full(nw.shape), state_spec],
        out_specs=[_rows(CHUNK, DN_VAL_DIM), state_spec],
        out_shape=[jax.ShapeDtypeStruct((TS, DN_VAL_DIM), BF16), jax.ShapeDtypeStruct(s_init.shape, F32)],
        scratch_shapes=[pltpu.VMEM((CHUNK, DN_VAL_DIM), BF16), pltpu.VMEM((CHUNK, DN_VAL_DIM), BF16),
                        pltpu.VMEM((CHUNK, DN_VAL_DIM), BF16), pltpu.VMEM((DN_DK, DN_V_HEADS * CHUNK), BF16),
                        pltpu.VMEM((CHUNK, DN_V_HEADS * CHUNK), BF16), pltpu.VMEM((DN_V_HEADS, LANES), F32)],
        compiler_params=_params(),
        name="gdn_sample",
    )(qn, kn, v, gates, z, nw, s_init)


def _gdn_prompt_kernel(q_ref, k_ref, v_ref, gates_ref, z_ref, nw_ref, sinit_ref, o_ref, s_ref,
                       u_s, w_s, qd_s, kdt_s, qkd_s, egl_s, *, cb):
    i = pl.program_id(0)
    cur = i % 2
    prev = 1 - cur

    @pl.when(i == 0)
    def _():
        for ref in (u_s, w_s, qd_s, kdt_s, qkd_s, egl_s):
            ref[1] = jnp.zeros(ref.shape[1:], ref.dtype)

    @pl.when(i <= 1)
    def _():
        s_ref[...] = sinit_ref[...]

    _gdn_wy_kernel(q_ref, k_ref, v_ref, gates_ref, u_s.at[cur], w_s.at[cur], qd_s.at[cur], kdt_s.at[cur],
                   qkd_s.at[cur], egl_s.at[cur], cb=cb)
    _gdn_rec_body(u_s.at[prev], w_s.at[prev], qd_s.at[prev], kdt_s.at[prev], qkd_s.at[prev], egl_s.at[prev],
                  z_ref, nw_ref, o_ref, s_ref, cb)


def _gdn_prompt(qn, kn, v, gates, z, nw, s_init):
    cb = WY_CHUNKS
    tm = cb * CHUNK
    nblk = NPC // cb
    ahead = lambda width: pl.BlockSpec((tm, width), lambda i: (jnp.minimum(i, nblk - 1), 0))
    behind = lambda width: pl.BlockSpec((tm, width), lambda i: (jnp.maximum(i - 1, 0), 0))
    state_spec = pl.BlockSpec((1, DN_V_HEADS, DN_DK, DN_DV), lambda i: (0, 0, 0, 0))
    slots = lambda rows_, width, dtype: pltpu.VMEM((2, rows_, width), dtype)
    return pl.pallas_call(
        functools.partial(_gdn_prompt_kernel, cb=cb),
        grid=(nblk + 1,),
        in_specs=[ahead(DN_KEY_DIM), ahead(DN_KEY_DIM), ahead(DN_VAL_DIM), ahead(LANES), behind(DN_VAL_DIM),
                  _full(nw.shape), state_spec],
        out_specs=[behind(DN_VAL_DIM), state_spec],
        out_shape=[jax.ShapeDtypeStruct((TP, DN_VAL_DIM), BF16), jax.ShapeDtypeStruct(s_init.shape, F32)],
        scratch_shapes=[slots(tm, DN_VAL_DIM, BF16), slots(tm, DN_VAL_DIM, BF16), slots(tm, DN_VAL_DIM, BF16),
                        slots(cb * DN_DK, DN_V_HEADS * CHUNK, BF16), slots(tm, DN_V_HEADS * CHUNK, BF16),
                        slots(cb * DN_V_HEADS, LANES, F32)],
        compiler_params=_params(),
        name="gdn_prompt",
    )(qn, kn, v, gates, z, nw, s_init)


def _lane_row(vec):
    return jnp.pad(vec.astype(F32), (0, LANES - vec.shape[0])).reshape(1, LANES)


def _history_tiles(state):
    tiles = [jnp.pad(state[:, TAPS_BACK - j:], ((0, 0), (0, HIST - j), (0, 0))) for j in range(1, CONV_WIDTH)]
    return jnp.stack(tiles, axis=1)


def _kvx_columns(k, v):
    parts = [t[..., h, :] for t in (k, v) for h in range(N_KV_HEADS) for _ in range(2)]
    return jnp.concatenate(parts, axis=-1)


def kernel(x_prompt, x_sample, cache_k, cache_v, state_conv, state_ssm, p_prompt, p_sample, w_qkv, b_qkv, attn_sinks, w_attn_out, w_dn_in, dn_conv_w, dn_a_log, dn_dt_bias, dn_norm_w, w_dn_out, w_ffn_gu, w_ffn_down, w_ple_gate, w_ple_proj, ln_g, ln_b):
    assert TS == TM, "the sample tokens must form exactly one row block"
    xp = x_prompt.reshape(TP, D_MODEL)
    xs = x_sample.reshape(TS, D_MODEL)
    ps = p_sample.reshape(DEPTH, TS, PLE_DIM)
    row = lambda v: v.reshape(1, -1)

    def ln(i, k):
        return row(ln_g[i, k]), row(ln_b[i, k])

    kv_heads = (N_KV_HEADS, HEAD_DIM)
    w_q, w_k, w_v = jnp.split(w_qkv[0], [Q_DIM, Q_DIM + N_KV_HEADS * HEAD_DIM], axis=1)
    b_q, b_k, b_v = jnp.split(b_qkv[0], [Q_DIM, Q_DIM + N_KV_HEADS * HEAD_DIM])
    w_x = jnp.concatenate([w_q, _kvx_columns(w_k.reshape((D_MODEL,) + kv_heads),
                                             w_v.reshape((D_MODEL,) + kv_heads))], axis=1).astype(BF16)
    b_x = jnp.concatenate([b_q, _kvx_columns(b_k.reshape(kv_heads), b_v.reshape(kv_heads))])
    cache = _kvx_columns(cache_k[0], cache_v[0]).reshape(DEC_BATCH * WINDOW, KVX_DIM)
    later_weights = (w_ffn_gu, w_ffn_down, w_ple_gate, w_ple_proj, w_attn_out[0], w_dn_out[0])
    o, kv, (*channel_weights, w_o_attn, w_o_dn) = _attention(
        attn_sinks[0], xp, xs, w_x, row(b_x), cache, later_weights)
    x = _mix_ffn(o, o, TP // TM, w_o_attn, xp, xs, 0, p_prompt, ps, 0, ln(0, 0), channel_weights, ln(0, 1), split_out=False)

    def heads_of(rows_, first_tile):
        cols = [rows_[..., (first_tile + h) * LANES:(first_tile + h) * LANES + HEAD_DIM] for h in range(N_KV_HEADS)]
        return jnp.stack(cols, axis=-2)

    kv_new_s = kv[TP:].reshape(DEC_BATCH, DEC_SEQ, KVX_DIM)
    new_k_prompt = heads_of(kv[TP - WINDOW:TP], 0)[None, None]
    new_v_prompt = heads_of(kv[TP - WINDOW:TP], N_KV_HEADS)[None, None]
    new_k_sample = jnp.concatenate([cache_k[0][:, DEC_SEQ:], heads_of(kv_new_s, 0)], axis=1)[None]
    new_v_sample = jnp.concatenate([cache_v[0][:, DEC_SEQ:], heads_of(kv_new_s, N_KV_HEADS)], axis=1)[None]

    alog = _lane_row(dn_a_log[0])
    dtb = _lane_row(dn_dt_bias[0])
    st_prompt = jnp.zeros((1, TAPS_BACK, HIST, CONV_DIM), F32)
    st_sample = _history_tiles(state_conv[0])
    dn_w = (w_dn_in[0].astype(BF16), dn_conv_w[0], alog, dtb)
    *tok_p, cst_p = _dn_in(x, st_prompt, *dn_w, tm=TM, row_block0=0, n_seq=1)
    *tok_s, cst_s = _dn_in(x, st_sample, *dn_w, tm=DEC_SEQ, row_block0=NPC, n_seq=DEC_BATCH)
    nw = row(dn_norm_w[0])
    s0_prompt = jnp.zeros((1,) + state_ssm.shape[2:], F32)
    qn, kn, v, z, gates = tok_p
    o_p, s_p = _gdn_prompt(qn, kn, v, gates, z, nw, s0_prompt)
    qn, kn, v, z, gates = tok_s
    o_s, s_s = _gdn_sample(qn, kn, v, gates, z, nw, state_ssm[0])
    y_prompt, y_sample = _mix_ffn(o_p, o_s, 0, w_o_dn, x, x, TP // TM, p_prompt, ps, 1, ln(1, 0),
                                  channel_weights, ln(1, 1), split_out=True)

    return (y_prompt.reshape(1, TP, D_MODEL), y_sample.reshape(DEC_BATCH, DEC_SEQ, D_MODEL),
            new_k_prompt, new_v_prompt, new_k_sample, new_v_sample,
            cst_p[:, :TAPS_BACK][None], s_p[None], cst_s[:, :TAPS_BACK][None], s_s[None])
```

```python
import functools

import jax
import jax.numpy as jnp
from jax import lax
from jax.experimental import pallas as pl
from jax.experimental.pallas import tpu as pltpu

F32 = jnp.float32
BF16 = jnp.bfloat16

D_MODEL = 1024
SEQ = 16384
DEC_BATCH = 8
DEC_SEQ = 64
TP = SEQ
TS = DEC_BATCH * DEC_SEQ
T = TP + TS
CHUNK = 64
NPC = TP // CHUNK

N_HEADS = 16
N_KV_HEADS = 2
HEAD_DIM = 64
WINDOW = 128
Q_DIM = N_HEADS * HEAD_DIM
KVX_DIM = 2 * N_KV_HEADS * 2 * HEAD_DIM

DN_QK_HEADS = 8
DN_V_HEADS = 16
DN_DK = 128
DN_DV = 128
DN_KEY_DIM = DN_QK_HEADS * DN_DK
DN_VAL_DIM = DN_V_HEADS * DN_DV
CONV_DIM = 2 * DN_KEY_DIM + DN_VAL_DIM
CONV_WIDTH = 4

D_FF = 2816
PLE_DIM = 256
DEPTH = 2
ALPHA = (2 * DEPTH) ** 0.25
LN_EPS = 1e-5
RMS_EPS = 1e-6
NEG_INF = -1e30

LANES = 128
SUBLANES = 8
MXU_DIM = 256
TM = 512
WY_CHUNKS = 8
VMEM_LIMIT = 56 * 1024 * 1024


def _params(vmem=VMEM_LIMIT):
    return pltpu.CompilerParams(dimension_semantics=("arbitrary",), vmem_limit_bytes=vmem)


def _full(shape):
    nd = len(shape)
    return pl.BlockSpec(shape, lambda i: (0,) * nd, pipeline_mode=pl.Buffered(1))


def _rows(tm, width):
    return pl.BlockSpec((tm, width), lambda i: (i, 0))


def _sigmoid(x):
    return 1.0 / (1.0 + jnp.exp(-x))


def _layer_norm(y, g, b):
    mu = jnp.mean(y, axis=-1, keepdims=True)
    d = y - mu
    var = jnp.mean(d * d, axis=-1, keepdims=True)
    return d * lax.rsqrt(var + LN_EPS) * g + b


def _iota(shape, axis):
    return lax.broadcasted_iota(jnp.int32, shape, axis)


def _main_tail(block, n_main, tail_block):
    main = pl.BlockSpec(block, lambda i: (jnp.minimum(i, n_main - 1), 0))
    tail = pl.BlockSpec(block, lambda i: (tail_block, 0))
    return main, tail


def _attn_kernel(sink_ref, xm_ref, xt_ref, w_ref, b_ref, cache_ref, *rest, cb, n_cast):
    cast_in, (o_ref, kv_ref), cast_out = rest[:n_cast], rest[n_cast:n_cast + 2], rest[n_cast + 2:2 * n_cast + 2]
    q_ref, old_ref = rest[2 * n_cast + 2:]
    _attn_body(sink_ref, xm_ref, xt_ref, w_ref, b_ref, cache_ref, o_ref, kv_ref, q_ref, old_ref, cb)
    for src, dst in zip(cast_in, cast_out):
        dst[...] = src[...].astype(BF16)


def _attn_body(sink_ref, xm_ref, xt_ref, w_ref, b_ref, cache_ref, o_ref, kv_ref, q_ref, old_ref, cb):
    step = pl.program_id(0)
    is_prompt = step < NPC // cb

    @pl.when(step == 0)
    def _():
        old_ref[...] = jnp.zeros(old_ref.shape, old_ref.dtype)

    xb = jnp.where(is_prompt, xm_ref[...], xt_ref[...]).astype(BF16)
    y = jnp.dot(xb, w_ref[...], preferred_element_type=F32) + b_ref[...]
    q_ref[...] = y[:, :Q_DIM].astype(BF16)
    kv_ref[...] = y[:, Q_DIM:]

    band = 3 * CHUNK
    keys = 2 * LANES
    pairs = Q_DIM // LANES
    per_kv = pairs // N_KV_HEADS
    key = _iota((CHUNK, 2 * keys), 1) & (keys - 1)
    lo = _iota((keys, LANES), 1) < HEAD_DIM
    zero = jnp.zeros((), BF16)
    pad = jnp.zeros((keys - band, KVX_DIM), F32)

    def halves(tile):
        return jnp.concatenate([jnp.where(lo, tile, zero), jnp.where(lo, zero, tile)], axis=0)

    for t in range(cb):
        rows = slice(t * CHUNK, (t + 1) * CHUNK)
        if t >= 2:
            before_p = kv_ref[(t - 2) * CHUNK:t * CHUNK, :]
        elif t == 1:
            before_p = jnp.concatenate([old_ref[CHUNK:2 * CHUNK, :], kv_ref[0:CHUNK, :]], axis=0)
        else:
            before_p = old_ref[...]
        before = jnp.where(is_prompt, before_p, cache_ref[t * WINDOW:(t + 1) * WINDOW, :])
        kvb = jnp.concatenate([before, kv_ref[rows, :], pad], axis=0).astype(BF16)
        start = jnp.where(is_prompt, jnp.maximum(2 - (step * cb + t), 0) * CHUNK, 0)
        valid = (key >= start) & (key < band)
        k_ops = [halves(kvb[:, kh * LANES:(kh + 1) * LANES]) for kh in range(N_KV_HEADS)]
        v_ops = [halves(kvb[:, (N_KV_HEADS + kh) * LANES:(N_KV_HEADS + kh + 1) * LANES])
                 for kh in range(N_KV_HEADS)]
        scores = [lax.dot_general(q_ref[rows, i * LANES:(i + 1) * LANES], k_ops[i // per_kv],
                                  (((1,), (1,)), ((), ())), preferred_element_type=F32) for i in range(pairs)]
        probs = []
        for i in range(pairs):
            s = jnp.where(valid, scores[i] * (HEAD_DIM ** -0.5), NEG_INF)
            ps = []
            for hh in range(2):
                sh = s[:, hh * keys:(hh + 1) * keys]
                sink = sink_ref[2 * i + hh]
                m = jnp.maximum(jnp.max(sh, axis=-1, keepdims=True), sink)
                p = jnp.exp(sh - m)
                denom = jnp.sum(p, axis=-1, keepdims=True) + jnp.exp(sink - m)
                ps.append(p * (1.0 / denom))
            probs.append(jnp.concatenate(ps, axis=1).astype(BF16))
        for i in range(pairs):
            o = jnp.dot(probs[i], v_ops[i // per_kv], preferred_element_type=F32)
            o_ref[rows, i * LANES:(i + 1) * LANES] = o.astype(BF16)
    old_ref[...] = kv_ref[cb * CHUNK - WINDOW:cb * CHUNK, :]


def _attention(sinks, x_main, x_tail, w, b, cache, to_cast):
    cb = TM // CHUNK
    n_main = TP // TM
    assert cb == DEC_BATCH and TP % TM == 0, "one grid step must hold all sample streams"
    flat = [wt.reshape(-1, wt.shape[-1]) for wt in to_cast]
    slices = [pl.BlockSpec((wt.shape[0] // n_main, wt.shape[1]), lambda i: (jnp.minimum(i, n_main - 1), 0))
              for wt in flat]
    assert all(wt.shape[0] % (2 * SUBLANES * n_main) == 0 for wt in flat)
    o, kv, *cast = pl.pallas_call(
        functools.partial(_attn_kernel, cb=cb, n_cast=len(flat)),
        grid=(T // TM,),
        in_specs=[pl.BlockSpec(memory_space=pltpu.SMEM), *_main_tail((TM, D_MODEL), n_main, 0),
                  _full(w.shape), _full(b.shape), _full(cache.shape), *slices],
        out_specs=[_rows(TM, Q_DIM), _rows(TM, KVX_DIM), *slices],
        out_shape=[jax.ShapeDtypeStruct((T, Q_DIM), BF16), jax.ShapeDtypeStruct((T, KVX_DIM), F32),
                   *[jax.ShapeDtypeStruct(wt.shape, BF16) for wt in flat]],
        scratch_shapes=[pltpu.VMEM((TM, Q_DIM), BF16), pltpu.VMEM((WINDOW, KVX_DIM), F32)],
        compiler_params=_params(),
        name="swa_attn",
    )(sinks, x_main, x_tail, w, b, cache, *flat)
    return o, kv, [c.reshape(wt.shape) for c, wt in zip(cast, to_cast)]


FF_CHUNK = 256
FF_ROWS = 512


def _mix_ffn_kernel(am_ref, at_ref, wo_ref, xm_ref, xt_ref, pm_ref, pt_ref, g0_ref, b0_ref,
                    wgu_ref, wd_ref, wg_ref, wp_ref, g1_ref, b1_ref, *rest, n_main, split_out):
    act_ref = rest[-1]
    is_main = pl.program_id(0) < n_main
    outs = []
    for r0 in range(0, TM, FF_ROWS):
        rows = slice(r0, r0 + FF_ROWS)
        a = jnp.where(is_main, am_ref[rows, :], at_ref[rows, :])
        y = jnp.dot(a, wo_ref[...], preferred_element_type=F32)
        x_res = jnp.where(is_main, xm_ref[rows, :], xt_ref[rows, :])
        x = _layer_norm(ALPHA * x_res + y, g0_ref[...], b0_ref[...])
        xb = x.astype(BF16)
        pb = jnp.where(is_main, pm_ref[rows, :], pt_ref[rows, :]).astype(BF16)
        e = _sigmoid(jnp.dot(xb, wg_ref[...], preferred_element_type=F32)) * jnp.dot(
            pb, wp_ref[...], preferred_element_type=F32)
        xe = ALPHA * x + e
        for c in range(D_FF // FF_CHUNK):
            lo = c * FF_CHUNK
            gate = jnp.dot(xb, wgu_ref[:, lo:lo + FF_CHUNK], preferred_element_type=F32)
            up = jnp.dot(xb, wgu_ref[:, D_FF + lo:D_FF + lo + FF_CHUNK], preferred_element_type=F32)
            act_ref[rows, lo:lo + FF_CHUNK] = (gate * _sigmoid(gate) * up).astype(BF16)
        f = jnp.dot(act_ref[rows, :], wd_ref[...], preferred_element_type=F32)
        out = _layer_norm(xe + f, g1_ref[...], b1_ref[...])
        if split_out:
            outs.append(out)
        else:
            rest[0][rows, :] = out
    if split_out:
        om_ref, ot_ref = rest[:2]

        @pl.when(is_main)
        def _():
            for k, out in enumerate(outs):
                om_ref[k * FF_ROWS:(k + 1) * FF_ROWS, :] = out

        @pl.when(jnp.logical_not(is_main))
        def _():
            for k, out in enumerate(outs):
                ot_ref[k * FF_ROWS:(k + 1) * FF_ROWS, :] = out


def _layer_of(stacked, layer):
    nd = stacked.ndim - 1
    return pl.BlockSpec((None,) + stacked.shape[1:], lambda i: (layer,) + (0,) * nd, pipeline_mode=pl.Buffered(1))


def _mix_ffn(a_main, a_tail, a_tail_block, wo, x_main, x_tail, x_tail_block, p_main, p_tail, layer,
             ln0, weights, ln1, *, split_out):
    n_main = TP // TM
    in_specs = [*_main_tail((TM, a_main.shape[1]), n_main, a_tail_block), _full(wo.shape),
                *_main_tail((TM, D_MODEL), n_main, x_tail_block),
                pl.BlockSpec((None, None, TM, PLE_DIM), lambda i: (layer, 0, jnp.minimum(i, n_main - 1), 0)),
                pl.BlockSpec((None, TM, PLE_DIM), lambda i: (layer, 0, 0)),
                _full(ln0[0].shape), _full(ln0[1].shape), *[_layer_of(w, layer) for w in weights],
                _full(ln1[0].shape), _full(ln1[1].shape)]
    if split_out:
        out_specs = list(_main_tail((TM, D_MODEL), n_main, 0))
        out_shape = [jax.ShapeDtypeStruct((TP, D_MODEL), F32), jax.ShapeDtypeStruct((TS, D_MODEL), F32)]
    else:
        out_specs = _rows(TM, D_MODEL)
        out_shape = jax.ShapeDtypeStruct((T, D_MODEL), F32)
    return pl.pallas_call(
        functools.partial(_mix_ffn_kernel, n_main=n_main, split_out=split_out),
        grid=(T // TM,),
        in_specs=in_specs,
        out_specs=out_specs,
        out_shape=out_shape,
        scratch_shapes=[pltpu.VMEM((TM, D_FF), BF16)],
        compiler_params=_params(),
        name="mix_ffn",
    )(a_main, a_tail, wo, x_main, x_tail, p_main, p_tail, *ln0, *weights, *ln1)


DN_COLS = 512
HIST = SUBLANES
TAPS_BACK = CONV_WIDTH - 1


def _dn_in_kernel(x_ref, st_ref, win_ref, cw_ref, alog_ref, dtb_ref,
                  qn_ref, kn_ref, v_ref, z_ref, gates_ref, cst_ref, hist_ref, tmp_ref, *, tm, steps_per_seq):
    i = pl.program_id(0)

    @pl.when(i % steps_per_seq == 0)
    def _():
        hist_ref[...] = st_ref[0]
        tmp_ref[:, :, HIST + tm:, :] = jnp.zeros((2, CONV_WIDTH, HIST, DN_COLS), F32)

    xb = x_ref[...].astype(BF16)
    for c in range(CONV_DIM // DN_COLS):
        lo = c * DN_COLS
        cols = slice(lo, lo + DN_COLS)
        buf = tmp_ref.at[c % 2]
        mm = jnp.dot(xb, win_ref[:, cols], preferred_element_type=F32)
        buf[0, HIST:HIST + tm, :] = mm
        for j in range(1, CONV_WIDTH):
            buf[j, HIST:2 * HIST, :] = hist_ref[j - 1, :, cols]
            buf[j, HIST + j:HIST + j + tm, :] = mm
            hist_ref[j - 1, :, cols] = buf[j, HIST + tm:2 * HIST + tm, :]
        cw = cw_ref[:, cols]
        conv = cw[3:4] * buf[0, HIST:HIST + tm, :]
        for j in range(1, CONV_WIDTH):
            conv = conv + cw[3 - j:4 - j] * buf[j, HIST:HIST + tm, :]
        y = conv * _sigmoid(conv)
        if lo < 2 * DN_KEY_DIM:
            is_q = lo < DN_KEY_DIM
            dst = qn_ref if is_q else kn_ref
            base = lo if is_q else lo - DN_KEY_DIM
            for t in range(DN_COLS // DN_DK):
                yt = y[:, t * DN_DK:(t + 1) * DN_DK]
                n = yt * lax.rsqrt(jnp.sum(yt * yt, axis=-1, keepdims=True) + 1e-6)
                if is_q:
                    n = n * (DN_DK ** -0.5)
                dst[:, base + t * DN_DK:base + (t + 1) * DN_DK] = n.astype(BF16)
        else:
            v_ref[:, lo - 2 * DN_KEY_DIM:lo - 2 * DN_KEY_DIM + DN_COLS] = y.astype(BF16)
    cst_ref[0] = hist_ref[TAPS_BACK - 1]

    for c in range(DN_VAL_DIM // DN_COLS):
        lo = c * DN_COLS
        z_ref[:, lo:lo + DN_COLS] = jnp.dot(xb, win_ref[:, CONV_DIM + lo:CONV_DIM + lo + DN_COLS],
                                            preferred_element_type=F32).astype(BF16)

    n_gate = 2 * DN_V_HEADS
    w_ab = jnp.concatenate([win_ref[:, CONV_DIM + DN_VAL_DIM:CONV_DIM + DN_VAL_DIM + n_gate],
                            jnp.zeros((D_MODEL, LANES - n_gate), BF16)], axis=1)
    ab = jnp.dot(xb, w_ab, preferred_element_type=F32)
    sp_in = ab + dtb_ref[...]
    softplus = jnp.maximum(sp_in, 0.0) + jnp.log1p(jnp.exp(-jnp.abs(sp_in)))
    gval = -jnp.exp(alog_ref[...]) * softplus
    gates_ref[...] = jnp.where(_iota(ab.shape, 1) < DN_V_HEADS, gval, _sigmoid(ab))


def _dn_in(x, state, w_in, cw, alog, dtb, *, tm, row_block0, n_seq):
    steps_per_seq = (TP // tm) if n_seq == 1 else 1
    n_steps = steps_per_seq * n_seq
    n_rows = n_steps * tm
    state_spec = pl.BlockSpec((1, TAPS_BACK, HIST, CONV_DIM), lambda i: (i // steps_per_seq, 0, 0, 0))
    cst_spec = pl.BlockSpec((1, HIST, CONV_DIM), lambda i: (i // steps_per_seq, 0, 0))
    in_specs = [pl.BlockSpec((tm, D_MODEL), lambda i: (row_block0 + i, 0)), state_spec,
                _full(w_in.shape), _full(cw.shape), _full(alog.shape), _full(dtb.shape)]
    out_shape = [jax.ShapeDtypeStruct((n_rows, DN_KEY_DIM), BF16), jax.ShapeDtypeStruct((n_rows, DN_KEY_DIM), BF16),
                 jax.ShapeDtypeStruct((n_rows, DN_VAL_DIM), BF16), jax.ShapeDtypeStruct((n_rows, DN_VAL_DIM), BF16),
                 jax.ShapeDtypeStruct((n_rows, LANES), F32),
                 jax.ShapeDtypeStruct((n_seq, HIST, CONV_DIM), F32)]
    out_specs = [_rows(tm, DN_KEY_DIM), _rows(tm, DN_KEY_DIM), _rows(tm, DN_VAL_DIM), _rows(tm, DN_VAL_DIM),
                 _rows(tm, LANES), cst_spec]
    return pl.pallas_call(
        functools.partial(_dn_in_kernel, tm=tm, steps_per_seq=steps_per_seq),
        grid=(n_steps,),
        in_specs=in_specs,
        out_specs=out_specs,
        out_shape=out_shape,
        scratch_shapes=[pltpu.VMEM((TAPS_BACK, HIST, CONV_DIM), F32),
                        pltpu.VMEM((2, CONV_WIDTH, tm + 2 * HIST, DN_COLS), F32)],
        compiler_params=_params(),
        name="dn_in_prompt" if n_seq == 1 else "dn_in_sample",
    )(x, state, w_in, cw, alog, dtb)


GROUP_HEADS = MXU_DIM // CHUNK


def _split3(x):
    hi = x.astype(BF16)
    r = x - hi.astype(F32)
    mid = r.astype(BF16)
    lo = (r - mid.astype(F32)).astype(BF16)
    return hi, mid, lo


def _unit_lower_inverses(lcats, eye_cat, bd01):
    c = lcats[0].shape[0]

    def mm(a, b):
        b_bd = jnp.concatenate([b.astype(BF16)] * GROUP_HEADS, axis=0) * bd01
        return jnp.dot(a.astype(BF16), b_bd, preferred_element_type=F32)

    ms = [mm(l, l) for l in lcats]
    ps = [eye_cat - l for l in lcats]
    power = 2
    while 2 * power < c:
        pms = [mm(jnp.concatenate([p, m], axis=0), m) for p, m in zip(ps, ms)]
        ps = [p + pm[:c] for p, pm in zip(ps, pms)]
        ms = [pm[c:] for pm in pms]
        power *= 2
    return [p + mm(p, m) for p, m in zip(ps, ms)]


def _gdn_wy_kernel(q_ref, k_ref, v_ref, gates_ref, u_ref, w_ref, qd_ref, kdt_ref, qkd_ref, egl_ref, *, cb):
    c = CHUNK
    grp = GROUP_HEADS * c
    row = _iota((c, LANES), 0)
    lane = _iota((c, LANES), 1)
    col = lane & (c - 1)
    lo = lane < c
    causal2 = row >= col
    strict2 = row > col
    tril = jnp.where(_iota((c, c), 0) >= _iota((c, c), 1), 1.0, 0.0).astype(BF16)
    eye_cat = jnp.where(_iota((c, grp), 0) == (_iota((c, grp), 1) & (c - 1)), 1.0, 0.0).astype(F32)
    shift = c.bit_length() - 1
    bd01 = jnp.where((_iota((grp, grp), 0) >> shift) == (_iota((grp, grp), 1) >> shift), 1.0, 0.0).astype(BF16)
    nt = (((1,), (1,)), ((), ()))

    lcats, rhs = [], []
    for cc in range(cb):
        rows = slice(cc * c, (cc + 1) * c)
        gates = gates_ref[rows, :]
        cum = sum(jnp.dot(tril, part, preferred_element_type=F32) for part in _split3(gates))
        cum_t = jnp.concatenate([cum, cum], axis=0).T
        egl_ref[cc * DN_V_HEADS:(cc + 1) * DN_V_HEADS, :] = jnp.broadcast_to(
            jnp.exp(cum_t[0:DN_V_HEADS, c - 1:c]), (DN_V_HEADS, LANES))
        lmats = []
        for j in range(DN_QK_HEADS):
            a, b = 2 * j, 2 * j + 1
            tile = slice(j * LANES, (j + 1) * LANES)
            kb = k_ref[rows, tile]
            qb = q_ref[rows, tile]
            kq = lax.dot_general(jnp.concatenate([kb, qb], axis=0), jnp.concatenate([kb, kb], axis=0), nt,
                                 preferred_element_type=F32)
            gb_lanes = [jnp.broadcast_to(cum[:, h:h + 1], (c, LANES)) for h in (a, b)]
            beta_lanes = [jnp.broadcast_to(gates[:, DN_V_HEADS + h:DN_V_HEADS + h + 1], (c, LANES)) for h in (a, b)]
            gbc = jnp.where(lo, gb_lanes[0], gb_lanes[1])
            gbr = jnp.where(lo, cum_t[a:a + 1, :], cum_t[b:b + 1, :])
            beta2 = jnp.where(lo, beta_lanes[0], beta_lanes[1])
            decay = jnp.where(causal2, jnp.exp(jnp.where(causal2, gbc - gbr, 0.0)), 0.0)
            lmats.append(jnp.where(strict2, kq[:c] * beta2 * decay, 0.0))
            qkd_ref[rows, tile] = (kq[c:] * decay).astype(BF16)
            kf = kb.astype(F32)
            qf = qb.astype(F32)
            kdec = []
            for h, gb, beta in zip((a, b), gb_lanes, beta_lanes):
                head = slice(h * DN_DV, (h + 1) * DN_DV)
                egb = jnp.exp(gb)
                vf = v_ref[rows, head].astype(F32)
                rhs.append(jnp.concatenate([vf * beta, kf * (beta * egb)], axis=1).astype(BF16))
                qd_ref[rows, head] = (qf * egb).astype(BF16)
                kdec.append(kf * jnp.exp(gb[c - 1:c, :] - gb))
            kdt_ref[cc * DN_DK:(cc + 1) * DN_DK, tile] = jnp.concatenate(kdec, axis=0).T.astype(BF16)
        lcats += [jnp.concatenate(lmats[2 * m:2 * m + 2], axis=1) for m in range(DN_V_HEADS // GROUP_HEADS)]

    tinvs = _unit_lower_inverses(lcats, eye_cat, bd01)
    zeros = jnp.zeros((c, 2 * DN_DV), BF16)
    for cc in range(cb):
        rows = slice(cc * c, (cc + 1) * c)
        for j in range(DN_QK_HEADS):
            a = 2 * j
            h0 = cc * DN_V_HEADS + a
            pair = slice(a * DN_DV, (a + 2) * DN_DV)
            tinv = tinvs[cc * (DN_V_HEADS // GROUP_HEADS) + j // 2][:, (j % 2) * LANES:(j % 2 + 1) * LANES]
            rhs_bd = jnp.concatenate([jnp.concatenate([rhs[h0], zeros], axis=1),
                                      jnp.concatenate([zeros, rhs[h0 + 1]], axis=1)], axis=0)
            sol = jnp.dot(tinv.astype(BF16), rhs_bd, preferred_element_type=F32)
            u_ref[rows, pair] = jnp.concatenate(
                [sol[:, 0:DN_DV], sol[:, 2 * DN_DV:3 * DN_DV]], axis=1).astype(BF16)
            w_ref[rows, pair] = jnp.concatenate(
                [sol[:, DN_DV:2 * DN_DV], sol[:, 3 * DN_DV:4 * DN_DV]], axis=1).astype(BF16)


def _gdn_rec_body(u_ref, w_ref, qd_ref, kdt_ref, qkd_ref, egl_ref, z_ref, nw_ref, o_ref, s_ref, cb):
    c = CHUNK
    nw = nw_ref[...]
    zero_s = jnp.zeros((DN_DK, DN_DV), BF16)
    zero_v = jnp.zeros((c, DN_DV), BF16)
    for cc in range(cb):
        rows = slice(cc * c, (cc + 1) * c)
        egl = egl_ref[cc * DN_V_HEADS:(cc + 1) * DN_V_HEADS, :]
        v_new, o_inter = [], []
        for m in range(DN_V_HEADS // 2):
            a = 2 * m
            pair = slice(a * DN_DV, (a + 2) * DN_DV)
            s_bd = jnp.concatenate(
                [jnp.concatenate([s_ref[0, a].astype(BF16), zero_s], axis=1),
                 jnp.concatenate([zero_s, s_ref[0, a + 1].astype(BF16)], axis=1)], axis=0)
            r = jnp.dot(jnp.concatenate([w_ref[rows, pair], qd_ref[rows, pair]], axis=0), s_bd,
                        preferred_element_type=F32)
            vn = u_ref[rows, pair].astype(F32) - r[:c]
            v_new += [vn[:, :DN_DV].astype(BF16), vn[:, DN_DV:].astype(BF16)]
            o_inter += [r[c:, :DN_DV], r[c:, DN_DV:]]
        for n in range(DN_V_HEADS // GROUP_HEADS):
            heads = range(GROUP_HEADS * n, GROUP_HEADS * (n + 1))
            v_bd = jnp.concatenate(
                [jnp.concatenate([v_new[h] if t == i else zero_v for t in range(GROUP_HEADS)], axis=1)
                 for i, h in enumerate(heads)], axis=0)
            grp = slice(n * GROUP_HEADS * c, (n + 1) * GROUP_HEADS * c)
            r2 = jnp.dot(jnp.concatenate([qkd_ref[rows, grp], kdt_ref[cc * DN_DK:(cc + 1) * DN_DK, grp]], axis=0),
                         v_bd, preferred_element_type=F32)
            for i, h in enumerate(heads):
                head = slice(h * DN_DV, (h + 1) * DN_DV)
                blk = slice(i * DN_DV, (i + 1) * DN_DV)
                s_ref[0, h] = s_ref[0, h] * egl[h:h + 1, :] + r2[c:, blk]
                o = o_inter[h] + r2[:c, blk]
                o = o * lax.rsqrt(jnp.mean(o * o, axis=-1, keepdims=True) + RMS_EPS) * nw
                zf = z_ref[rows, head].astype(F32)
                o_ref[rows, head] = (o * (zf * _sigmoid(zf))).astype(BF16)


def _gdn_sample_kernel(q_ref, k_ref, v_ref, gates_ref, z_ref, nw_ref, sinit_ref, o_ref, s_ref,
                       u_s, w_s, qd_s, kdt_s, qkd_s, egl_s):
    s_ref[...] = sinit_ref[...]
    _gdn_wy_kernel(q_ref, k_ref, v_ref, gates_ref, u_s, w_s, qd_s, kdt_s, qkd_s, egl_s, cb=1)
    _gdn_rec_body(u_s, w_s, qd_s, kdt_s, qkd_s, egl_s, z_ref, nw_ref, o_ref, s_ref, 1)


def _gdn_sample(qn, kn, v, gates, z, nw, s_init):
    state_spec = pl.BlockSpec((1, DN_V_HEADS, DN_DK, DN_DV), lambda i: (i, 0, 0, 0))
    return pl.pallas_call(
        _gdn_sample_kernel,
        grid=(DEC_BATCH,),
        in_specs=[_rows(CHUNK, DN_KEY_DIM), _rows(CHUNK, DN_KEY_DIM), _rows(CHUNK, DN_VAL_DIM), _rows(CHUNK, LANES),
                  _rows(CHUNK, DN_VAL_DIM), _full(nw.shape), state_spec],
        out_specs=[_rows(CHUNK, DN_VAL_DIM), state_spec],
        out_shape=[jax.ShapeDtypeStruct((TS, DN_VAL_DIM), BF16), jax.ShapeDtypeStruct(s_init.shape, F32)],
        scratch_shapes=[pltpu.VMEM((CHUNK, DN_VAL_DIM), BF16), pltpu.VMEM((CHUNK, DN_VAL_DIM), BF16),
                        pltpu.VMEM((CHUNK, DN_VAL_DIM), BF16), pltpu.VMEM((DN_DK, DN_V_HEADS * CHUNK), BF16),
                        pltpu.VMEM((CHUNK, DN_V_HEADS * CHUNK), BF16), pltpu.VMEM((DN_V_HEADS, LANES), F32)],
        compiler_params=_params(),
        name="gdn_sample",
    )(qn, kn, v, gates, z, nw, s_init)


def _gdn_prompt_kernel(q_ref, k_ref, v_ref, gates_ref, z_ref, nw_ref, sinit_ref, o_ref, s_ref,
                       u_s, w_s, qd_s, kdt_s, qkd_s, egl_s, *, cb):
    i = pl.program_id(0)
    cur = i % 2
    prev = 1 - cur

    @pl.when(i == 0)
    def _():
        for ref in (u_s, w_s, qd_s, kdt_s, qkd_s, egl_s):
            ref[1] = jnp.zeros(ref.shape[1:], ref.dtype)

    @pl.when(i <= 1)
    def _():
        s_ref[...] = sinit_ref[...]

    _gdn_wy_kernel(q_ref, k_ref, v_ref, gates_ref, u_s.at[cur], w_s.at[cur], qd_s.at[cur], kdt_s.at[cur],
                   qkd_s.at[cur], egl_s.at[cur], cb=cb)
    _gdn_rec_body(u_s.at[prev], w_s.at[prev], qd_s.at[prev], kdt_s.at[prev], qkd_s.at[prev], egl_s.at[prev],
                  z_ref, nw_ref, o_ref, s_ref, cb)


def _gdn_prompt(qn, kn, v, gates, z, nw, s_init):
    cb = WY_CHUNKS
    tm = cb * CHUNK
    nblk = NPC // cb
    ahead = lambda width: pl.BlockSpec((tm, width), lambda i: (jnp.minimum(i, nblk - 1), 0))
    behind = lambda width: pl.BlockSpec((tm, width), lambda i: (jnp.maximum(i - 1, 0), 0))
    state_spec = pl.BlockSpec((1, DN_V_HEADS, DN_DK, DN_DV), lambda i: (0, 0, 0, 0))
    slots = lambda rows_, width, dtype: pltpu.VMEM((2, rows_, width), dtype)
    return pl.pallas_call(
        functools.partial(_gdn_prompt_kernel, cb=cb),
        grid=(nblk + 1,),
        in_specs=[ahead(DN_KEY_DIM), ahead(DN_KEY_DIM), ahead(DN_VAL_DIM), ahead(LANES), behind(DN_VAL_DIM),
                  _full(nw.shape), state_spec],
        out_specs=[behind(DN_VAL_DIM), state_spec],
        out_shape=[jax.ShapeDtypeStruct((TP, DN_VAL_DIM), BF16), jax.ShapeDtypeStruct(s_init.shape, F32)],
        scratch_shapes=[slots(tm, DN_VAL_DIM, BF16), slots(tm, DN_VAL_DIM, BF16), slots(tm, DN_VAL_DIM, BF16),
                        slots(cb * DN_DK, DN_V_HEADS * CHUNK, BF16), slots(tm, DN_V_HEADS * CHUNK, BF16),
                        slots(cb * DN_V_HEADS, LANES, F32)],
        compiler_params=_params(),
        name="gdn_prompt",
    )(qn, kn, v, gates, z, nw, s_init)


def _lane_row(vec):
    return jnp.pad(vec.astype(F32), (0, LANES - vec.shape[0])).reshape(1, LANES)


def _history_tiles(state):
    tiles = [jnp.pad(state[:, TAPS_BACK - j:], ((0, 0), (0, HIST - j), (0, 0))) for j in range(1, CONV_WIDTH)]
    return jnp.stack(tiles, axis=1)


def _kvx_columns(k, v):
    parts = [t[..., h, :] for t in (k, v) for h in range(N_KV_HEADS) for _ in range(2)]
    return jnp.concatenate(parts, axis=-1)


def kernel(x_prompt, x_sample, cache_k, cache_v, state_conv, state_ssm, p_prompt, p_sample, w_qkv, b_qkv, attn_sinks, w_attn_out, w_dn_in, dn_conv_w, dn_a_log, dn_dt_bias, dn_norm_w, w_dn_out, w_ffn_gu, w_ffn_down, w_ple_gate, w_ple_proj, ln_g, ln_b):
    assert TS == TM, "the sample tokens must form exactly one row block"
    xp = x_prompt.reshape(TP, D_MODEL)
    xs = x_sample.reshape(TS, D_MODEL)
    ps = p_sample.reshape(DEPTH, TS, PLE_DIM)
    row = lambda v: v.reshape(1, -1)

    def ln(i, k):
        return row(ln_g[i, k]), row(ln_b[i, k])

    kv_heads = (N_KV_HEADS, HEAD_DIM)
    w_q, w_k, w_v = jnp.split(w_qkv[0], [Q_DIM, Q_DIM + N_KV_HEADS * HEAD_DIM], axis=1)
    b_q, b_k, b_v = jnp.split(b_qkv[0], [Q_DIM, Q_DIM + N_KV_HEADS * HEAD_DIM])
    w_x = jnp.concatenate([w_q, _kvx_columns(w_k.reshape((D_MODEL,) + kv_heads),
                                             w_v.reshape((D_MODEL,) + kv_heads))], axis=1).astype(BF16)
    b_x = jnp.concatenate([b_q, _kvx_columns(b_k.reshape(kv_heads), b_v.reshape(kv_heads))])
    cache = _kvx_columns(cache_k[0], cache_v[0]).reshape(DEC_BATCH * WINDOW, KVX_DIM)
    later_weights = (w_ffn_gu, w_ffn_down, w_ple_gate, w_ple_proj, w_attn_out[0], w_dn_out[0])
    o, kv, (*channel_weights, w_o_attn, w_o_dn) = _attention(
        attn_sinks[0], xp, xs, w_x, row(b_x), cache, later_weights)
    x = _mix_ffn(o, o, TP // TM, w_o_attn, xp, xs, 0, p_prompt, ps, 0, ln(0, 0), channel_weights, ln(0, 1), split_out=False)

    def heads_of(rows_, first_tile):
        cols = [rows_[..., (first_tile + h) * LANES:(first_tile + h) * LANES + HEAD_DIM] for h in range(N_KV_HEADS)]
        return jnp.stack(cols, axis=-2)

    kv_new_s = kv[TP:].reshape(DEC_BATCH, DEC_SEQ, KVX_DIM)
    new_k_prompt = heads_of(kv[TP - WINDOW:TP], 0)[None, None]
    new_v_prompt = heads_of(kv[TP - WINDOW:TP], N_KV_HEADS)[None, None]
    new_k_sample = jnp.concatenate([cache_k[0][:, DEC_SEQ:], heads_of(kv_new_s, 0)], axis=1)[None]
    new_v_sample = jnp.concatenate([cache_v[0][:, DEC_SEQ:], heads_of(kv_new_s, N_KV_HEADS)], axis=1)[None]

    alog = _lane_row(dn_a_log[0])
    dtb = _lane_row(dn_dt_bias[0])
    st_prompt = jnp.zeros((1, TAPS_BACK, HIST, CONV_DIM), F32)
    st_sample = _history_tiles(state_conv[0])
    dn_w = (w_dn_in[0].astype(BF16), dn_conv_w[0], alog, dtb)
    *tok_p, cst_p = _dn_in(x, st_prompt, *dn_w, tm=TM, row_block0=0, n_seq=1)
    *tok_s, cst_s = _dn_in(x, st_sample, *dn_w, tm=DEC_SEQ, row_block0=NPC, n_seq=DEC_BATCH)
    nw = row(dn_norm_w[0])
    s0_prompt = jnp.zeros((1,) + state_ssm.shape[2:], F32)
    qn, kn, v, z, gates = tok_p
    o_p, s_p = _gdn_prompt(qn, kn, v, gates, z, nw, s0_prompt)
    qn, kn, v, z, gates = tok_s
    o_s, s_s = _gdn_sample(qn, kn, v, gates, z, nw, state_ssm[0])
    y_prompt, y_sample = _mix_ffn(o_p, o_s, 0, w_o_dn, x, x, TP // TM, p_prompt, ps, 1, ln(1, 0),
                                  channel_weights, ln(1, 1), split_out=True)

    return (y_prompt.reshape(1, TP, D_MODEL), y_sample.reshape(DEC_BATCH, DEC_SEQ, D_MODEL),
            new_k_prompt, new_v_prompt, new_k_sample, new_v_sample,
            cst_p[:, :TAPS_BACK][None], s_p[None], cst_s[:, :TAPS_BACK][None], s_s[None])
```

```python
import functools

import jax
import jax.numpy as jnp
from jax import lax
from jax.experimental import pallas as pl
from jax.experimental.pallas import tpu as pltpu

F32 = jnp.float32
BF16 = jnp.bfloat16

D_MODEL = 1024
SEQ = 16384
DEC_BATCH = 8
DEC_SEQ = 64
TP = SEQ
TS = DEC_BATCH * DEC_SEQ
T = TP + TS
CHUNK = 64
NPC = TP // CHUNK

N_HEADS = 16
N_KV_HEADS = 2
HEAD_DIM = 64
WINDOW = 128
Q_DIM = N_HEADS * HEAD_DIM
KVX_DIM = 2 * N_KV_HEADS * 2 * HEAD_DIM

DN_QK_HEADS = 8
DN_V_HEADS = 16
DN_DK = 128
DN_DV = 128
DN_KEY_DIM = DN_QK_HEADS * DN_DK
DN_VAL_DIM = DN_V_HEADS * DN_DV
CONV_DIM = 2 * DN_KEY_DIM + DN_VAL_DIM
CONV_WIDTH = 4

D_FF = 2816
PLE_DIM = 256
DEPTH = 2
ALPHA = (2 * DEPTH) ** 0.25
LN_EPS = 1e-5
RMS_EPS = 1e-6
NEG_INF = -1e30

LANES = 128
SUBLANES = 8
MXU_DIM = 256
TM = 512
WY_CHUNKS = 4
VMEM_LIMIT = 56 * 1024 * 1024


def _params(vmem=VMEM_LIMIT):
    return pltpu.CompilerParams(dimension_semantics=("arbitrary",), vmem_limit_bytes=vmem)


def _full(shape):
    nd = len(shape)
    return pl.BlockSpec(shape, lambda i: (0,) * nd, pipeline_mode=pl.Buffered(1))


def _rows(tm, width):
    return pl.BlockSpec((tm, width), lambda i: (i, 0))


def _sigmoid(x):
    return 1.0 / (1.0 + jnp.exp(-x))


def _layer_norm(y, g, b):
    mu = jnp.mean(y, axis=-1, keepdims=True)
    d = y - mu
    var = jnp.mean(d * d, axis=-1, keepdims=True)
    return d * lax.rsqrt(var + LN_EPS) * g + b


def _iota(shape, axis):
    return lax.broadcasted_iota(jnp.int32, shape, axis)


def _main_tail(block, n_main, tail_block):
    main = pl.BlockSpec(block, lambda i: (jnp.minimum(i, n_main - 1), 0))
    tail = pl.BlockSpec(block, lambda i: (tail_block, 0))
    return main, tail


def _attn_kernel(sink_ref, xm_ref, xt_ref, w_ref, b_ref, cache_ref, *rest, cb, n_cast):
    cast_in, (o_ref, kv_ref), cast_out = rest[:n_cast], rest[n_cast:n_cast + 2], rest[n_cast + 2:2 * n_cast + 2]
    q_ref, old_ref = rest[2 * n_cast + 2:]
    _attn_body(sink_ref, xm_ref, xt_ref, w_ref, b_ref, cache_ref, o_ref, kv_ref, q_ref, old_ref, cb)
    for src, dst in zip(cast_in, cast_out):
        dst[...] = src[...].astype(BF16)


def _attn_body(sink_ref, xm_ref, xt_ref, w_ref, b_ref, cache_ref, o_ref, kv_ref, q_ref, old_ref, cb):
    step = pl.program_id(0)
    is_prompt = step < NPC // cb

    @pl.when(step == 0)
    def _():
        old_ref[...] = jnp.zeros(old_ref.shape, old_ref.dtype)

    xb = jnp.where(is_prompt, xm_ref[...], xt_ref[...]).astype(BF16)
    y = jnp.dot(xb, w_ref[...], preferred_element_type=F32) + b_ref[...]
    q_ref[...] = y[:, :Q_DIM].astype(BF16)
    kv_ref[...] = y[:, Q_DIM:]

    band = 3 * CHUNK
    keys = 2 * LANES
    pairs = Q_DIM // LANES
    per_kv = pairs // N_KV_HEADS
    key = _iota((CHUNK, 2 * keys), 1) & (keys - 1)
    lo = _iota((keys, LANES), 1) < HEAD_DIM
    zero = jnp.zeros((), BF16)
    pad = jnp.zeros((keys - band, KVX_DIM), F32)

    def halves(tile):
        return jnp.concatenate([jnp.where(lo, tile, zero), jnp.where(lo, zero, tile)], axis=0)

    for t in range(cb):
        rows = slice(t * CHUNK, (t + 1) * CHUNK)
        if t >= 2:
            before_p = kv_ref[(t - 2) * CHUNK:t * CHUNK, :]
        elif t == 1:
            before_p = jnp.concatenate([old_ref[CHUNK:2 * CHUNK, :], kv_ref[0:CHUNK, :]], axis=0)
        else:
            before_p = old_ref[...]
        before = jnp.where(is_prompt, before_p, cache_ref[t * WINDOW:(t + 1) * WINDOW, :])
        kvb = jnp.concatenate([before, kv_ref[rows, :], pad], axis=0).astype(BF16)
        start = jnp.where(is_prompt, jnp.maximum(2 - (step * cb + t), 0) * CHUNK, 0)
        valid = (key >= start) & (key < band)
        k_ops = [halves(kvb[:, kh * LANES:(kh + 1) * LANES]) for kh in range(N_KV_HEADS)]
        v_ops = [halves(kvb[:, (N_KV_HEADS + kh) * LANES:(N_KV_HEADS + kh + 1) * LANES])
                 for kh in range(N_KV_HEADS)]
        scores = [lax.dot_general(q_ref[rows, i * LANES:(i + 1) * LANES], k_ops[i // per_kv],
                                  (((1,), (1,)), ((), ())), preferred_element_type=F32) for i in range(pairs)]
        probs = []
        for i in range(pairs):
            s = jnp.where(valid, scores[i] * (HEAD_DIM ** -0.5), NEG_INF)
            ps = []
            for hh in range(2):
                sh = s[:, hh * keys:(hh + 1) * keys]
                sink = sink_ref[2 * i + hh]
                m = jnp.maximum(jnp.max(sh, axis=-1, keepdims=True), sink)
                p = jnp.exp(sh - m)
                denom = jnp.sum(p, axis=-1, keepdims=True) + jnp.exp(sink - m)
                ps.append(p * (1.0 / denom))
            probs.append(jnp.concatenate(ps, axis=1).astype(BF16))
        for i in range(pairs):
            o = jnp.dot(probs[i], v_ops[i // per_kv], preferred_element_type=F32)
            o_ref[rows, i * LANES:(i + 1) * LANES] = o.astype(BF16)
    old_ref[...] = kv_ref[cb * CHUNK - WINDOW:cb * CHUNK, :]


def _attention(sinks, x_main, x_tail, w, b, cache, to_cast):
    cb = TM // CHUNK
    n_main = TP // TM
    assert cb == DEC_BATCH and TP % TM == 0, "one grid step must hold all sample streams"
    flat = [wt.reshape(-1, wt.shape[-1]) for wt in to_cast]
    slices = [pl.BlockSpec((wt.shape[0] // n_main, wt.shape[1]), lambda i: (jnp.minimum(i, n_main - 1), 0))
              for wt in flat]
    assert all(wt.shape[0] % (2 * SUBLANES * n_main) == 0 for wt in flat)
    o, kv, *cast = pl.pallas_call(
        functools.partial(_attn_kernel, cb=cb, n_cast=len(flat)),
        grid=(T // TM,),
        in_specs=[pl.BlockSpec(memory_space=pltpu.SMEM), *_main_tail((TM, D_MODEL), n_main, 0),
                  _full(w.shape), _full(b.shape), _full(cache.shape), *slices],
        out_specs=[_rows(TM, Q_DIM), _rows(TM, KVX_DIM), *slices],
        out_shape=[jax.ShapeDtypeStruct((T, Q_DIM), BF16), jax.ShapeDtypeStruct((T, KVX_DIM), F32),
                   *[jax.ShapeDtypeStruct(wt.shape, BF16) for wt in flat]],
        scratch_shapes=[pltpu.VMEM((TM, Q_DIM), BF16), pltpu.VMEM((WINDOW, KVX_DIM), F32)],
        compiler_params=_params(),
        name="swa_attn",
    )(sinks, x_main, x_tail, w, b, cache, *flat)
    return o, kv, [c.reshape(wt.shape) for c, wt in zip(cast, to_cast)]


FF_CHUNK = 256
FF_ROWS = 512


def _mix_ffn_kernel(am_ref, at_ref, wo_ref, xm_ref, xt_ref, pm_ref, pt_ref, g0_ref, b0_ref,
                    wgu_ref, wd_ref, wg_ref, wp_ref, g1_ref, b1_ref, *rest, n_main, split_out):
    act_ref = rest[-1]
    is_main = pl.program_id(0) < n_main
    outs = []
    for r0 in range(0, TM, FF_ROWS):
        rows = slice(r0, r0 + FF_ROWS)
        a = jnp.where(is_main, am_ref[rows, :], at_ref[rows, :])
        y = jnp.dot(a, wo_ref[...], preferred_element_type=F32)
        x_res = jnp.where(is_main, xm_ref[rows, :], xt_ref[rows, :])
        x = _layer_norm(ALPHA * x_res + y, g0_ref[...], b0_ref[...])
        xb = x.astype(BF16)
        pb = jnp.where(is_main, pm_ref[rows, :], pt_ref[rows, :]).astype(BF16)
        e = _sigmoid(jnp.dot(xb, wg_ref[...], preferred_element_type=F32)) * jnp.dot(
            pb, wp_ref[...], preferred_element_type=F32)
        xe = ALPHA * x + e
        for c in range(D_FF // FF_CHUNK):
            lo = c * FF_CHUNK
            gate = jnp.dot(xb, wgu_ref[:, lo:lo + FF_CHUNK], preferred_element_type=F32)
            up = jnp.dot(xb, wgu_ref[:, D_FF + lo:D_FF + lo + FF_CHUNK], preferred_element_type=F32)
            act_ref[rows, lo:lo + FF_CHUNK] = (gate * _sigmoid(gate) * up).astype(BF16)
        f = jnp.dot(act_ref[rows, :], wd_ref[...], preferred_element_type=F32)
        out = _layer_norm(xe + f, g1_ref[...], b1_ref[...])
        if split_out:
            outs.append(out)
        else:
            rest[0][rows, :] = out
    if split_out:
        om_ref, ot_ref = rest[:2]

        @pl.when(is_main)
        def _():
            for k, out in enumerate(outs):
                om_ref[k * FF_ROWS:(k + 1) * FF_ROWS, :] = out

        @pl.when(jnp.logical_not(is_main))
        def _():
            for k, out in enumerate(outs):
                ot_ref[k * FF_ROWS:(k + 1) * FF_ROWS, :] = out


def _layer_of(stacked, layer):
    nd = stacked.ndim - 1
    return pl.BlockSpec((None,) + stacked.shape[1:], lambda i: (layer,) + (0,) * nd, pipeline_mode=pl.Buffered(1))


def _mix_ffn(a_main, a_tail, a_tail_block, wo, x_main, x_tail, x_tail_block, p_main, p_tail, layer,
             ln0, weights, ln1, *, split_out):
    n_main = TP // TM
    in_specs = [*_main_tail((TM, a_main.shape[1]), n_main, a_tail_block), _full(wo.shape),
                *_main_tail((TM, D_MODEL), n_main, x_tail_block),
                pl.BlockSpec((None, None, TM, PLE_DIM), lambda i: (layer, 0, jnp.minimum(i, n_main - 1), 0)),
                pl.BlockSpec((None, TM, PLE_DIM), lambda i: (layer, 0, 0)),
                _full(ln0[0].shape), _full(ln0[1].shape), *[_layer_of(w, layer) for w in weights],
                _full(ln1[0].shape), _full(ln1[1].shape)]
    if split_out:
        out_specs = list(_main_tail((TM, D_MODEL), n_main, 0))
        out_shape = [jax.ShapeDtypeStruct((TP, D_MODEL), F32), jax.ShapeDtypeStruct((TS, D_MODEL), F32)]
    else:
        out_specs = _rows(TM, D_MODEL)
        out_shape = jax.ShapeDtypeStruct((T, D_MODEL), F32)
    return pl.pallas_call(
        functools.partial(_mix_ffn_kernel, n_main=n_main, split_out=split_out),
        grid=(T // TM,),
        in_specs=in_specs,
        out_specs=out_specs,
        out_shape=out_shape,
        scratch_shapes=[pltpu.VMEM((TM, D_FF), BF16)],
        compiler_params=_params(),
        name="mix_ffn",
    )(a_main, a_tail, wo, x_main, x_tail, p_main, p_tail, *ln0, *weights, *ln1)


DN_COLS = 512
HIST = SUBLANES
TAPS_BACK = CONV_WIDTH - 1


def _dn_in_kernel(x_ref, st_ref, win_ref, cw_ref, alog_ref, dtb_ref,
                  qn_ref, kn_ref, v_ref, z_ref, gates_ref, cst_ref, hist_ref, tmp_ref, *, tm, steps_per_seq):
    i = pl.program_id(0)

    @pl.when(i % steps_per_seq == 0)
    def _():
        hist_ref[...] = st_ref[0]
        tmp_ref[:, :, HIST + tm:, :] = jnp.zeros((2, CONV_WIDTH, HIST, DN_COLS), F32)

    xb = x_ref[...].astype(BF16)
    for c in range(CONV_DIM // DN_COLS):
        lo = c * DN_COLS
        cols = slice(lo, lo + DN_COLS)
        buf = tmp_ref.at[c % 2]
        mm = jnp.dot(xb, win_ref[:, cols], preferred_element_type=F32)
        buf[0, HIST:HIST + tm, :] = mm
        for j in range(1, CONV_WIDTH):
            buf[j, HIST:2 * HIST, :] = hist_ref[j - 1, :, cols]
            buf[j, HIST + j:HIST + j + tm, :] = mm
            hist_ref[j - 1, :, cols] = buf[j, HIST + tm:2 * HIST + tm, :]
        cw = cw_ref[:, cols]
        conv = cw[3:4] * buf[0, HIST:HIST + tm, :]
        for j in range(1, CONV_WIDTH):
            conv = conv + cw[3 - j:4 - j] * buf[j, HIST:HIST + tm, :]
        y = conv * _sigmoid(conv)
        if lo < 2 * DN_KEY_DIM:
            is_q = lo < DN_KEY_DIM
            dst = qn_ref if is_q else kn_ref
            base = lo if is_q else lo - DN_KEY_DIM
            for t in range(DN_COLS // DN_DK):
                yt = y[:, t * DN_DK:(t + 1) * DN_DK]
                n = yt * lax.rsqrt(jnp.sum(yt * yt, axis=-1, keepdims=True) + 1e-6)
                if is_q:
                    n = n * (DN_DK ** -0.5)
                dst[:, base + t * DN_DK:base + (t + 1) * DN_DK] = n.astype(BF16)
        else:
            v_ref[:, lo - 2 * DN_KEY_DIM:lo - 2 * DN_KEY_DIM + DN_COLS] = y.astype(BF16)
    cst_ref[0] = hist_ref[TAPS_BACK - 1]

    for c in range(DN_VAL_DIM // DN_COLS):
        lo = c * DN_COLS
        z_ref[:, lo:lo + DN_COLS] = jnp.dot(xb, win_ref[:, CONV_DIM + lo:CONV_DIM + lo + DN_COLS],
                                            preferred_element_type=F32).astype(BF16)

    n_gate = 2 * DN_V_HEADS
    w_ab = jnp.concatenate([win_ref[:, CONV_DIM + DN_VAL_DIM:CONV_DIM + DN_VAL_DIM + n_gate],
                            jnp.zeros((D_MODEL, LANES - n_gate), BF16)], axis=1)
    ab = jnp.dot(xb, w_ab, preferred_element_type=F32)
    sp_in = ab + dtb_ref[...]
    softplus = jnp.maximum(sp_in, 0.0) + jnp.log1p(jnp.exp(-jnp.abs(sp_in)))
    gval = -jnp.exp(alog_ref[...]) * softplus
    gates_ref[...] = jnp.where(_iota(ab.shape, 1) < DN_V_HEADS, gval, _sigmoid(ab))


def _dn_in(x, state, w_in, cw, alog, dtb, *, tm, row_block0, n_seq):
    steps_per_seq = (TP // tm) if n_seq == 1 else 1
    n_steps = steps_per_seq * n_seq
    n_rows = n_steps * tm
    state_spec = pl.BlockSpec((1, TAPS_BACK, HIST, CONV_DIM), lambda i: (i // steps_per_seq, 0, 0, 0))
    cst_spec = pl.BlockSpec((1, HIST, CONV_DIM), lambda i: (i // steps_per_seq, 0, 0))
    in_specs = [pl.BlockSpec((tm, D_MODEL), lambda i: (row_block0 + i, 0)), state_spec,
                _full(w_in.shape), _full(cw.shape), _full(alog.shape), _full(dtb.shape)]
    out_shape = [jax.ShapeDtypeStruct((n_rows, DN_KEY_DIM), BF16), jax.ShapeDtypeStruct((n_rows, DN_KEY_DIM), BF16),
                 jax.ShapeDtypeStruct((n_rows, DN_VAL_DIM), BF16), jax.ShapeDtypeStruct((n_rows, DN_VAL_DIM), BF16),
                 jax.ShapeDtypeStruct((n_rows, LANES), F32),
                 jax.ShapeDtypeStruct((n_seq, HIST, CONV_DIM), F32)]
    out_specs = [_rows(tm, DN_KEY_DIM), _rows(tm, DN_KEY_DIM), _rows(tm, DN_VAL_DIM), _rows(tm, DN_VAL_DIM),
                 _rows(tm, LANES), cst_spec]
    return pl.pallas_call(
        functools.partial(_dn_in_kernel, tm=tm, steps_per_seq=steps_per_seq),
        grid=(n_steps,),
        in_specs=in_specs,
        out_specs=out_specs,
        out_shape=out_shape,
        scratch_shapes=[pltpu.VMEM((TAPS_BACK, HIST, CONV_DIM), F32),
                        pltpu.VMEM((2, CONV_WIDTH, tm + 2 * HIST, DN_COLS), F32)],
        compiler_params=_params(),
        name="dn_in_prompt" if n_seq == 1 else "dn_in_sample",
    )(x, state, w_in, cw, alog, dtb)


GROUP_HEADS = MXU_DIM // CHUNK


def _split3(x):
    hi = x.astype(BF16)
    r = x - hi.astype(F32)
    mid = r.astype(BF16)
    lo = (r - mid.astype(F32)).astype(BF16)
    return hi, mid, lo


def _unit_lower_inverses(lcats, eye_cat, bd01):
    c = lcats[0].shape[0]

    def mm(a, b):
        b_bd = jnp.concatenate([b.astype(BF16)] * GROUP_HEADS, axis=0) * bd01
        return jnp.dot(a.astype(BF16), b_bd, preferred_element_type=F32)

    ms = [mm(l, l) for l in lcats]
    ps = [eye_cat - l for l in lcats]
    power = 2
    while 2 * power < c:
        pms = [mm(jnp.concatenate([p, m], axis=0), m) for p, m in zip(ps, ms)]
        ps = [p + pm[:c] for p, pm in zip(ps, pms)]
        ms = [pm[c:] for pm in pms]
        power *= 2
    return [p + mm(p, m) for p, m in zip(ps, ms)]


def _gdn_wy_kernel(q_ref, k_ref, v_ref, gates_ref, u_ref, w_ref, qd_ref, kdt_ref, qkd_ref, egl_ref, *, cb):
    c = CHUNK
    grp = GROUP_HEADS * c
    row = _iota((c, LANES), 0)
    lane = _iota((c, LANES), 1)
    col = lane & (c - 1)
    lo = lane < c
    causal2 = row >= col
    strict2 = row > col
    tril = jnp.where(_iota((c, c), 0) >= _iota((c, c), 1), 1.0, 0.0).astype(BF16)
    eye_cat = jnp.where(_iota((c, grp), 0) == (_iota((c, grp), 1) & (c - 1)), 1.0, 0.0).astype(F32)
    shift = c.bit_length() - 1
    bd01 = jnp.where((_iota((grp, grp), 0) >> shift) == (_iota((grp, grp), 1) >> shift), 1.0, 0.0).astype(BF16)
    nt = (((1,), (1,)), ((), ()))

    lcats, rhs = [], []
    for cc in range(cb):
        rows = slice(cc * c, (cc + 1) * c)
        gates = gates_ref[rows, :]
        cum = sum(jnp.dot(tril, part, preferred_element_type=F32) for part in _split3(gates))
        cum_t = jnp.concatenate([cum, cum], axis=0).T
        egl_ref[cc * DN_V_HEADS:(cc + 1) * DN_V_HEADS, :] = jnp.broadcast_to(
            jnp.exp(cum_t[0:DN_V_HEADS, c - 1:c]), (DN_V_HEADS, LANES))
        lmats = []
        for j in range(DN_QK_HEADS):
            a, b = 2 * j, 2 * j + 1
            tile = slice(j * LANES, (j + 1) * LANES)
            kb = k_ref[rows, tile]
            qb = q_ref[rows, tile]
            kq = lax.dot_general(jnp.concatenate([kb, qb], axis=0), jnp.concatenate([kb, kb], axis=0), nt,
                                 preferred_element_type=F32)
            gb_lanes = [jnp.broadcast_to(cum[:, h:h + 1], (c, LANES)) for h in (a, b)]
            beta_lanes = [jnp.broadcast_to(gates[:, DN_V_HEADS + h:DN_V_HEADS + h + 1], (c, LANES)) for h in (a, b)]
            gbc = jnp.where(lo, gb_lanes[0], gb_lanes[1])
            gbr = jnp.where(lo, cum_t[a:a + 1, :], cum_t[b:b + 1, :])
            beta2 = jnp.where(lo, beta_lanes[0], beta_lanes[1])
            decay = jnp.where(causal2, jnp.exp(jnp.where(causal2, gbc - gbr, 0.0)), 0.0)
            lmats.append(jnp.where(strict2, kq[:c] * beta2 * decay, 0.0))
            qkd_ref[rows, tile] = (kq[c:] * decay).astype(BF16)
            kf = kb.astype(F32)
            qf = qb.astype(F32)
            kdec = []
            for h, gb, beta in zip((a, b), gb_lanes, beta_lanes):
                head = slice(h * DN_DV, (h + 1) * DN_DV)
                egb = jnp.exp(gb)
                vf = v_ref[rows, head].astype(F32)
                rhs.append(jnp.concatenate([vf * beta, kf * (beta * egb)], axis=1).astype(BF16))
                qd_ref[rows, head] = (qf * egb).astype(BF16)
                kdec.append(kf * jnp.exp(gb[c - 1:c, :] - gb))
            kdt_ref[cc * DN_DK:(cc + 1) * DN_DK, tile] = jnp.concatenate(kdec, axis=0).T.astype(BF16)
        lcats += [jnp.concatenate(lmats[2 * m:2 * m + 2], axis=1) for m in range(DN_V_HEADS // GROUP_HEADS)]

    tinvs = _unit_lower_inverses(lcats, eye_cat, bd01)
    zeros = jnp.zeros((c, 2 * DN_DV), BF16)
    for cc in range(cb):
        rows = slice(cc * c, (cc + 1) * c)
        for j in range(DN_QK_HEADS):
            a = 2 * j
            h0 = cc * DN_V_HEADS + a
            pair = slice(a * DN_DV, (a + 2) * DN_DV)
            tinv = tinvs[cc * (DN_V_HEADS // GROUP_HEADS) + j // 2][:, (j % 2) * LANES:(j % 2 + 1) * LANES]
            rhs_bd = jnp.concatenate([jnp.concatenate([rhs[h0], zeros], axis=1),
                                      jnp.concatenate([zeros, rhs[h0 + 1]], axis=1)], axis=0)
            sol = jnp.dot(tinv.astype(BF16), rhs_bd, preferred_element_type=F32)
            u_ref[rows, pair] = jnp.concatenate(
                [sol[:, 0:DN_DV], sol[:, 2 * DN_DV:3 * DN_DV]], axis=1).astype(BF16)
            w_ref[rows, pair] = jnp.concatenate(
                [sol[:, DN_DV:2 * DN_DV], sol[:, 3 * DN_DV:4 * DN_DV]], axis=1).astype(BF16)


def _gdn_rec_body(u_ref, w_ref, qd_ref, kdt_ref, qkd_ref, egl_ref, z_ref, nw_ref, o_ref, s_ref, cb):
    c = CHUNK
    nw = nw_ref[...]
    zero_s = jnp.zeros((DN_DK, DN_DV), BF16)
    zero_v = jnp.zeros((c, DN_DV), BF16)
    for cc in range(cb):
        rows = slice(cc * c, (cc + 1) * c)
        egl = egl_ref[cc * DN_V_HEADS:(cc + 1) * DN_V_HEADS, :]
        v_new, o_inter = [], []
        for m in range(DN_V_HEADS // 2):
            a = 2 * m
            pair = slice(a * DN_DV, (a + 2) * DN_DV)
            s_bd = jnp.concatenate(
                [jnp.concatenate([s_ref[0, a].astype(BF16), zero_s], axis=1),
                 jnp.concatenate([zero_s, s_ref[0, a + 1].astype(BF16)], axis=1)], axis=0)
            r = jnp.dot(jnp.concatenate([w_ref[rows, pair], qd_ref[rows, pair]], axis=0), s_bd,
                        preferred_element_type=F32)
            vn = u_ref[rows, pair].astype(F32) - r[:c]
            v_new += [vn[:, :DN_DV].astype(BF16), vn[:, DN_DV:].astype(BF16)]
            o_inter += [r[c:, :DN_DV], r[c:, DN_DV:]]
        for n in range(DN_V_HEADS // GROUP_HEADS):
            heads = range(GROUP_HEADS * n, GROUP_HEADS * (n + 1))
            v_bd = jnp.concatenate(
                [jnp.concatenate([v_new[h] if t == i else zero_v for t in range(GROUP_HEADS)], axis=1)
                 for i, h in enumerate(heads)], axis=0)
            grp = slice(n * GROUP_HEADS * c, (n + 1) * GROUP_HEADS * c)
            r2 = jnp.dot(jnp.concatenate([qkd_ref[rows, grp], kdt_ref[cc * DN_DK:(cc + 1) * DN_DK, grp]], axis=0),
                         v_bd, preferred_element_type=F32)
            for i, h in enumerate(heads):
                head = slice(h * DN_DV, (h + 1) * DN_DV)
                blk = slice(i * DN_DV, (i + 1) * DN_DV)
                s_ref[0, h] = s_ref[0, h] * egl[h:h + 1, :] + r2[c:, blk]
                o = o_inter[h] + r2[:c, blk]
                o = o * lax.rsqrt(jnp.mean(o * o, axis=-1, keepdims=True) + RMS_EPS) * nw
                zf = z_ref[rows, head].astype(F32)
                o_ref[rows, head] = (o * (zf * _sigmoid(zf))).astype(BF16)


def _gdn_sample_kernel(q_ref, k_ref, v_ref, gates_ref, z_ref, nw_ref, sinit_ref, o_ref, s_ref,
                       u_s, w_s, qd_s, kdt_s, qkd_s, egl_s):
    s_ref[...] = sinit_ref[...]
    _gdn_wy_kernel(q_ref, k_ref, v_ref, gates_ref, u_s, w_s, qd_s, kdt_s, qkd_s, egl_s, cb=1)
    _gdn_rec_body(u_s, w_s, qd_s, kdt_s, qkd_s, egl_s, z_ref, nw_ref, o_ref, s_ref, 1)


def _gdn_sample(qn, kn, v, gates, z, nw, s_init):
    state_spec = pl.BlockSpec((1, DN_V_HEADS, DN_DK, DN_DV), lambda i: (i, 0, 0, 0))
    return pl.pallas_call(
        _gdn_sample_kernel,
        grid=(DEC_BATCH,),
        in_specs=[_rows(CHUNK, DN_KEY_DIM), _rows(CHUNK, DN_KEY_DIM), _rows(CHUNK, DN_VAL_DIM), _rows(CHUNK, LANES),
                  _rows(CHUNK, DN_VAL_DIM), _full(nw.shape), state_spec],
        out_specs=[_rows(CHUNK, DN_VAL_DIM), state_spec],
        out_shape=[jax.ShapeDtypeStruct((TS, DN_VAL_DIM), BF16), jax.ShapeDtypeStruct(s_init.shape, F32)],
        scratch_shapes=[pltpu.VMEM((CHUNK, DN_VAL_DIM), BF16), pltpu.VMEM((CHUNK, DN_VAL_DIM), BF16),
                        pltpu.VMEM((CHUNK, DN_VAL_DIM), BF16), pltpu.VMEM((DN_DK, DN_V_HEADS * CHUNK), BF16),
                        pltpu.VMEM((CHUNK, DN_V_HEADS * CHUNK), BF16), pltpu.VMEM((DN_V_HEADS, LANES), F32)],
        compiler_params=_params(),
        name="gdn_sample",
    )(qn, kn, v, gates, z, nw, s_init)


def _gdn_prompt_kernel(q_ref, k_ref, v_ref, gates_ref, z_ref, nw_ref, sinit_ref, o_ref, s_ref,
                       u_s, w_s, qd_s, kdt_s, qkd_s, egl_s, *, cb):
    i = pl.program_id(0)
    cur = i % 2
    prev = 1 - cur

    @pl.when(i == 0)
    def _():
        for ref in (u_s, w_s, qd_s, kdt_s, qkd_s, egl_s):
            ref[1] = jnp.zeros(ref.shape[1:], ref.dtype)

    @pl.when(i <= 1)
    def _():
        s_ref[...] = sinit_ref[...]

    _gdn_rec_body(u_s.at[prev], w_s.at[prev], qd_s.at[prev], kdt_s.at[prev], qkd_s.at[prev], egl_s.at[prev],
                  z_ref, nw_ref, o_ref, s_ref, cb)
    _gdn_wy_kernel(q_ref, k_ref, v_ref, gates_ref, u_s.at[cur], w_s.at[cur], qd_s.at[cur], kdt_s.at[cur],
                   qkd_s.at[cur], egl_s.at[cur], cb=cb)


def _gdn_prompt(qn, kn, v, gates, z, nw, s_init):
    cb = WY_CHUNKS
    tm = cb * CHUNK
    nblk = NPC // cb
    ahead = lambda width: pl.BlockSpec((tm, width), lambda i: (jnp.minimum(i, nblk - 1), 0))
    behind = lambda width: pl.BlockSpec((tm, width), lambda i: (jnp.maximum(i - 1, 0), 0))
    state_spec = pl.BlockSpec((1, DN_V_HEADS, DN_DK, DN_DV), lambda i: (0, 0, 0, 0))
    slots = lambda rows_, width, dtype: pltpu.VMEM((2, rows_, width), dtype)
    return pl.pallas_call(
        functools.partial(_gdn_prompt_kernel, cb=cb),
        grid=(nblk + 1,),
        in_specs=[ahead(DN_KEY_DIM), ahead(DN_KEY_DIM), ahead(DN_VAL_DIM), ahead(LANES), behind(DN_VAL_DIM),
                  _full(nw.shape), state_spec],
        out_specs=[behind(DN_VAL_DIM), state_spec],
        out_shape=[jax.ShapeDtypeStruct((TP, DN_VAL_DIM), BF16), jax.ShapeDtypeStruct(s_init.shape, F32)],
        scratch_shapes=[slots(tm, DN_VAL_DIM, BF16), slots(tm, DN_VAL_DIM, BF16), slots(tm, DN_VAL_DIM, BF16),
                        slots(cb * DN_DK, DN_V_HEADS * CHUNK, BF16), slots(tm, DN_V_HEADS * CHUNK, BF16),
                        slots(cb * DN_V_HEADS, LANES, F32)],
        compiler_params=_params(),
        name="gdn_prompt",
    )(qn, kn, v, gates, z, nw, s_init)


def _lane_row(vec):
    return jnp.pad(vec.astype(F32), (0, LANES - vec.shape[0])).reshape(1, LANES)


def _history_tiles(state):
    tiles = [jnp.pad(state[:, TAPS_BACK - j:], ((0, 0), (0, HIST - j), (0, 0))) for j in range(1, CONV_WIDTH)]
    return jnp.stack(tiles, axis=1)


def _kvx_columns(k, v):
    parts = [t[..., h, :] for t in (k, v) for h in range(N_KV_HEADS) for _ in range(2)]
    return jnp.concatenate(parts, axis=-1)


def kernel(x_prompt, x_sample, cache_k, cache_v, state_conv, state_ssm, p_prompt, p_sample, w_qkv, b_qkv, attn_sinks, w_attn_out, w_dn_in, dn_conv_w, dn_a_log, dn_dt_bias, dn_norm_w, w_dn_out, w_ffn_gu, w_ffn_down, w_ple_gate, w_ple_proj, ln_g, ln_b):
    assert TS == TM, "the sample tokens must form exactly one row block"
    xp = x_prompt.reshape(TP, D_MODEL)
    xs = x_sample.reshape(TS, D_MODEL)
    ps = p_sample.reshape(DEPTH, TS, PLE_DIM)
    row = lambda v: v.reshape(1, -1)

    def ln(i, k):
        return row(ln_g[i, k]), row(ln_b[i, k])

    kv_heads = (N_KV_HEADS, HEAD_DIM)
    w_q, w_k, w_v = jnp.split(w_qkv[0], [Q_DIM, Q_DIM + N_KV_HEADS * HEAD_DIM], axis=1)
    b_q, b_k, b_v = jnp.split(b_qkv[0], [Q_DIM, Q_DIM + N_KV_HEADS * HEAD_DIM])
    w_x = jnp.concatenate([w_q, _kvx_columns(w_k.reshape((D_MODEL,) + kv_heads),
                                             w_v.reshape((D_MODEL,) + kv_heads))], axis=1).astype(BF16)
    b_x = jnp.concatenate([b_q, _kvx_columns(b_k.reshape(kv_heads), b_v.reshape(kv_heads))])
    cache = _kvx_columns(cache_k[0], cache_v[0]).reshape(DEC_BATCH * WINDOW, KVX_DIM)
    later_weights = (w_ffn_gu, w_ffn_down, w_ple_gate, w_ple_proj, w_attn_out[0], w_dn_out[0])
    o, kv, (*channel_weights, w_o_attn, w_o_dn) = _attention(
        attn_sinks[0], xp, xs, w_x, row(b_x), cache, later_weights)
    x = _mix_ffn(o, o, TP // TM, w_o_attn, xp, xs, 0, p_prompt, ps, 0, ln(0, 0), channel_weights, ln(0, 1), split_out=False)

    def heads_of(rows_, first_tile):
        cols = [rows_[..., (first_tile + h) * LANES:(first_tile + h) * LANES + HEAD_DIM] for h in range(N_KV_HEADS)]
        return jnp.stack(cols, axis=-2)

    kv_new_s = kv[TP:].reshape(DEC_BATCH, DEC_SEQ, KVX_DIM)
    new_k_prompt = heads_of(kv[TP - WINDOW:TP], 0)[None, None]
    new_v_prompt = heads_of(kv[TP - WINDOW:TP], N_KV_HEADS)[None, None]
    new_k_sample = jnp.concatenate([cache_k[0][:, DEC_SEQ:], heads_of(kv_new_s, 0)], axis=1)[None]
    new_v_sample = jnp.concatenate([cache_v[0][:, DEC_SEQ:], heads_of(kv_new_s, N_KV_HEADS)], axis=1)[None]

    alog = _lane_row(dn_a_log[0])
    dtb = _lane_row(dn_dt_bias[0])
    st_prompt = jnp.zeros((1, TAPS_BACK, HIST, CONV_DIM), F32)
    st_sample = _history_tiles(state_conv[0])
    dn_w = (w_dn_in[0].astype(BF16), dn_conv_w[0], alog, dtb)
    *tok_p, cst_p = _dn_in(x, st_prompt, *dn_w, tm=TM, row_block0=0, n_seq=1)
    *tok_s, cst_s = _dn_in(x, st_sample, *dn_w, tm=DEC_SEQ, row_block0=NPC, n_seq=DEC_BATCH)
    nw = row(dn_norm_w[0])
    s0_prompt = jnp.zeros((1,) + state_ssm.shape[2:], F32)
    qn, kn, v, z, gates = tok_p
    o_p, s_p = _gdn_prompt(qn, kn, v, gates, z, nw, s0_prompt)
    qn, kn, v, z, gates = tok_s
    o_s, s_s = _gdn_sample(qn, kn, v, gates, z, nw, state_ssm[0])
    y_prompt, y_sample = _mix_ffn(o_p, o_s, 0, w_o_dn, x, x, TP // TM, p_prompt, ps, 1, ln(1, 0),
                                  channel_weights, ln(1, 1), split_out=True)

    return (y_prompt.reshape(1, TP, D_MODEL), y_sample.reshape(DEC_BATCH, DEC_SEQ, D_MODEL),
            new_k_prompt, new_v_prompt, new_k_sample, new_v_sample,
            cst_p[:, :TAPS_BACK][None], s_p[None], cst_s[:, :TAPS_BACK][None], s_s[None])
```
